```python
import math, functools
import jax, jax.numpy as jnp
from jax import lax
import numpy as np

D_MODEL = 1024
BATCH = 4
SEQ = 4096
DEPTH = 4
DEC_BATCH = 128
DEC_SEQ = 1
PAST_LEN = 2048
PAGE_SIZE = 128

MIX = D_MODEL
A_CH = MIX // 4
A_GROUP = 16
A_NG = A_CH // A_GROUP
A_P = 64
B_HD = 64
B_CH = 3 * MIX // 8
B_HEADS = B_CH // B_HD
B_NG = 2
B_N = 128
B_CONV = 4
B_CHUNK = 128
B_XBC = B_CH + 2 * B_NG * B_N
C_HD = 64
C_CH = MIX - A_CH - B_CH
C_HEADS = C_CH // C_HD
Q_BLOCK = 128
D_FF = 4 * D_MODEL
IN_WIDTHS = (A_CH, B_CH, B_XBC, B_HEADS, C_CH, C_CH, C_CH, C_HEADS)
D_IN = sum(IN_WIDTHS)
EPS = 1e-6

kernel_name = "hybrid_s5_ssd_fox_decoder_step"


def rmsnorm(x, g):
    xf = x.astype(jnp.float32)
    y = xf * lax.rsqrt(jnp.mean(xf * xf, axis=-1, keepdims=True) + EPS)
    return (y * g.astype(jnp.float32)).astype(x.dtype)


def segsum(x):
    t = x.shape[-1]
    idx = jnp.arange(t)
    xx = jnp.broadcast_to(x[..., :, None], x.shape + (t,))
    xx = jnp.where(idx[:, None] > idx[None, :], xx, 0.0)
    cs = jnp.cumsum(xx, axis=-2)
    return jnp.where(idx[:, None] >= idx[None, :], cs, -jnp.inf)


def s5_mixer(u, h_re, h_im, a_re, a_im, log_dt, b_re, b_im, c_re, c_im, d, w_glu, b_glu):
    f32 = jnp.float32
    n, l, _ = u.shape
    uf = u.astype(f32).reshape(n, l, A_NG, A_GROUP)
    ar, ai = a_re.astype(f32), a_im.astype(f32)
    dt = jnp.exp(log_dt.astype(f32))[:, None]
    mag = jnp.exp(ar * dt)
    abar_re, abar_im = mag * jnp.cos(ai * dt), mag * jnp.sin(ai * dt)
    den = ar * ar + ai * ai
    nr, ni = abar_re - 1.0, abar_im
    fr, fi = (nr * ar + ni * ai) / den, (ni * ar - nr * ai) / den
    br, bi = b_re.astype(f32), b_im.astype(f32)
    bbar_re = fr[..., None] * br - fi[..., None] * bi
    bbar_im = fr[..., None] * bi + fi[..., None] * br
    bu_re = jnp.einsum("nlgc,gpc->nlgp", uf, bbar_re)
    bu_im = jnp.einsum("nlgc,gpc->nlgp", uf, bbar_im)
    hr, hi = h_re.astype(f32), h_im.astype(f32)
    bu_re = bu_re.at[:, 0].add(abar_re * hr - abar_im * hi)
    bu_im = bu_im.at[:, 0].add(abar_re * hi + abar_im * hr)
    aa_re = jnp.broadcast_to(abar_re, bu_re.shape)
    aa_im = jnp.broadcast_to(abar_im, bu_im.shape)

    def combine(e1, e2):
        a1r, a1i, b1r, b1i = e1
        a2r, a2i, b2r, b2i = e2
        return (a2r * a1r - a2i * a1i, a2r * a1i + a2i * a1r,
                a2r * b1r - a2i * b1i + b2r, a2r * b1i + a2i * b1r + b2i)

    _, _, hs_re, hs_im = lax.associative_scan(combine, (aa_re, aa_im, bu_re, bu_im), axis=1)
    y = (jnp.einsum("nlgp,gcp->nlgc", hs_re, c_re.astype(f32))
         - jnp.einsum("nlgp,gcp->nlgc", hs_im, c_im.astype(f32))
         + d.astype(f32).reshape(A_NG, A_GROUP) * uf).reshape(n, l, A_CH)
    y = jax.nn.gelu(y)
    out = y * jax.nn.sigmoid(y @ w_glu.astype(f32) + b_glu.astype(f32))
    return out.astype(u.dtype), hs_re[:, -1], hs_im[:, -1]


def ssd_scan(x, dt, a, bm, cm, h0):
    n, length = x.shape[:2]
    q = min(B_CHUNK, length)
    nc = -(-length // q)
    pad = nc * q - length
    padt = lambda t: jnp.pad(t, [(0, 0), (0, pad)] + [(0, 0)] * (t.ndim - 2))
    x, dt, bm, cm = padt(x), padt(dt), padt(bm), padt(cm)
    r = B_HEADS // B_NG
    x = x.reshape(n, nc, q, B_NG, r, B_HD)
    dt = dt.reshape(n, nc, q, B_NG, r)
    bm = bm.reshape(n, nc, q, B_NG, B_N)
    cm = cm.reshape(n, nc, q, B_NG, B_N)
    da = jnp.moveaxis(dt * a.reshape(B_NG, r), 2, -1)
    acum = jnp.cumsum(da, axis=-1)
    lmat = jnp.exp(segsum(da))
    xdt = x * dt[..., None]
    cb = jnp.einsum("nctgk,ncsgk->ncgts", cm, bm)
    y_diag = jnp.einsum("ncgts,ncgrts,ncsgrp->nctgrp", cb, lmat, xdt)
    decay_states = jnp.exp(acum[..., -1:] - acum)
    states = jnp.einsum("ncsgk,ncgrs,ncsgrp->ncgrpk", bm, decay_states, xdt)
    states = jnp.concatenate([h0.reshape(n, 1, B_NG, r, B_HD, B_N), states], axis=1)
    chunk_da = jnp.pad(acum[..., -1], ((0, 0), (1, 0), (0, 0), (0, 0)))
    decay_chunk = jnp.exp(segsum(jnp.moveaxis(chunk_da, 1, -1)))
    new_states = jnp.einsum("ngrzc,ncgrpk->nzgrpk", decay_chunk, states)
    prev, final = new_states[:, :-1], new_states[:, -1]
    y_off = jnp.einsum("nctgk,ncgrpk,ncgrt->nctgrp", cm, prev, jnp.exp(acum))
    y = (y_diag + y_off).reshape(n, nc * q, B_HEADS, B_HD)[:, :length]
    return y, final.reshape(n, B_HEADS, B_HD, B_N)


def ssd_mixer(z, xbc, dt_raw, conv_prev, ssm_prev, conv_w, conv_b, dt_bias, a_log, d_skip, norm_g):
    f32 = jnp.float32
    n, length, _ = xbc.shape
    full = jnp.concatenate([conv_prev.astype(xbc.dtype), xbc], axis=1)
    conv = conv_b + sum(full[:, k:k + length] * conv_w[k] for k in range(B_CONV))
    new_conv = full[:, -(B_CONV - 1):]
    xbc_c = jax.nn.silu(conv.astype(f32))
    xs = xbc_c[..., :B_CH].reshape(n, length, B_HEADS, B_HD)
    bm = xbc_c[..., B_CH:B_CH + B_NG * B_N].reshape(n, length, B_NG, B_N)
    cm = xbc_c[..., B_CH + B_NG * B_N:].reshape(n, length, B_NG, B_N)
    dt = jax.nn.softplus(dt_raw.astype(f32) + dt_bias.astype(f32))
    a = -jnp.exp(a_log.astype(f32))
    y, h = ssd_scan(xs, dt, a, bm, cm, ssm_prev.astype(f32))
    y = (y + d_skip.astype(f32)[:, None] * xs).reshape(n, length, B_CH)
    y = rmsnorm(y * jax.nn.silu(z.astype(f32)), norm_g)
    return y.astype(xbc.dtype), new_conv, h


def fox_attend(q, k, v, cq, ck, q_pos, k_pos):
    s = jnp.einsum("nqhd,nkhd->nhqk", q, k).astype(jnp.float32) * (C_HD ** -0.5)
    s = s + (jnp.swapaxes(cq, 1, 2)[..., :, None] - jnp.swapaxes(ck, 1, 2)[..., None, :])
    s = jnp.where(q_pos[:, None] >= k_pos[None, :], s, -jnp.inf)
    p = jax.nn.softmax(s, axis=-1)
    return jnp.einsum("nhqk,nkhd->nqhd", p.astype(v.dtype), v)


def fox_prompt(q, k, v, logf):
    n, length = q.shape[:2]
    c = jnp.cumsum(logf, axis=1)
    nb = length // Q_BLOCK
    k_pos = jnp.arange(length)
    qb = jnp.swapaxes(q.reshape(n, nb, Q_BLOCK, C_HEADS, C_HD), 0, 1)
    cb = jnp.swapaxes(c.reshape(n, nb, Q_BLOCK, C_HEADS), 0, 1)

    def one_block(args):
        qi, ci, i = args
        q_pos = i * Q_BLOCK + jnp.arange(Q_BLOCK)
        return fox_attend(qi, k, v, ci, c, q_pos, k_pos)

    out = lax.map(one_block, (qb, cb, jnp.arange(nb)))
    return jnp.swapaxes(out, 0, 1).reshape(n, length, C_HEADS, C_HD)


def fox_sample(cache_k, cache_v, cache_logf, page_table, layer, q, k, v, logf):
    n, lq = q.shape[:2]
    kp = cache_k[layer, page_table].reshape(n, -1, C_HEADS, C_HD)
    vp = cache_v[layer, page_table].reshape(n, -1, C_HEADS, C_HD)
    lfp = cache_logf[layer, page_table].reshape(n, -1, C_HEADS)
    past = kp.shape[1]
    k_all = jnp.concatenate([kp.astype(k.dtype), k], axis=1)
    v_all = jnp.concatenate([vp.astype(v.dtype), v], axis=1)
    c_all = jnp.cumsum(jnp.concatenate([lfp.astype(jnp.float32), logf], axis=1), axis=1)
    q_pos = past + jnp.arange(lq)
    k_pos = jnp.arange(past + lq)
    return fox_attend(q, k_all, v_all, c_all[:, past:], c_all, q_pos, k_pos)


def sublayers(x, cond, attend, conv_prev, ssm_prev, s5_re0, s5_im0,
              w_ada, b_ada, g_pre_mix, g_post_mix, g_pre_ffn, g_post_ffn, w_in, w_out,
              a_re, a_im, log_dt, b_re, b_im, c_re, c_im, s5_d, w_glu, b_glu,
              conv_w, conv_b, dt_bias, a_log, ssd_d, ssd_norm_g, b_f, w_ff1, w_ff2):
    n, length, _ = x.shape
    mod = jax.nn.silu(cond) @ w_ada + b_ada
    sh1, sc1, g1, sh2, sc2, g2 = jnp.split(mod[:, None, :], 6, axis=-1)
    h = rmsnorm(x, g_pre_mix) * (1 + sc1) + sh1
    proj = h @ w_in
    offs = [0]
    for wd in IN_WIDTHS:
        offs.append(offs[-1] + wd)
    u_a, z, xbc, dt_raw, q, k, v, f_raw = [proj[..., offs[i]:offs[i + 1]] for i in range(len(IN_WIDTHS))]
    ya, s5_re, s5_im = s5_mixer(u_a, s5_re0, s5_im0, a_re, a_im, log_dt, b_re, b_im,
                                c_re, c_im, s5_d, w_glu, b_glu)
    yb, conv_new, ssm_new = ssd_mixer(z, xbc, dt_raw, conv_prev, ssm_prev, conv_w, conv_b,
                                      dt_bias, a_log, ssd_d, ssd_norm_g)
    q = q.reshape(n, length, C_HEADS, C_HD)
    k = k.reshape(n, length, C_HEADS, C_HD)
    v = v.reshape(n, length, C_HEADS, C_HD)
    logf = jax.nn.log_sigmoid(f_raw.astype(jnp.float32) + b_f.astype(jnp.float32))
    yc = attend(q, k, v, logf).reshape(n, length, C_CH)
    mix = jnp.concatenate([ya, yb.astype(ya.dtype), yc.astype(ya.dtype)], axis=-1) @ w_out
    x = x + g1 * rmsnorm(mix, g_post_mix)
    h = rmsnorm(x, g_pre_ffn) * (1 + sc2) + sh2
    f = jnp.square(jax.nn.relu(h @ w_ff1)) @ w_ff2
    x = x + g2 * rmsnorm(f, g_post_ffn)
    return x, (k, v, logf, s5_re, s5_im, conv_new, ssm_new)


def setup_inputs(seed: int = 0) -> dict:
    key = jax.random.key(seed)
    ks = iter(jax.random.split(key, 48))
    f32 = jnp.float32
    nrm = lambda shape, s=1.0: s * jax.random.normal(next(ks), shape, f32)
    uni = lambda shape, lo, hi: jax.random.uniform(next(ks), shape, f32, minval=lo, maxval=hi)
    n_pages = PAST_LEN // PAGE_SIZE
    n_used = DEC_BATCH * n_pages
    n_pool = n_used + max(1, n_used // 4)
    x_prompt = nrm((BATCH, SEQ, D_MODEL))
    x_sample = nrm((DEC_BATCH, DEC_SEQ, D_MODEL))
    c_prompt = nrm((BATCH, D_MODEL))
    c_sample = nrm((DEC_BATCH, D_MODEL))
    cache_k = nrm((DEPTH, n_pool, PAGE_SIZE, C_HEADS, C_HD))
    cache_v = nrm((DEPTH, n_pool, PAGE_SIZE, C_HEADS, C_HD))
    cache_logf = jax.nn.log_sigmoid(nrm((DEPTH, n_pool, PAGE_SIZE, C_HEADS)) + 3.0)
    page_table = jax.random.permutation(next(ks), n_pool)[:n_used].reshape(DEC_BATCH, n_pages).astype(jnp.int32)
    state_s5_re = nrm((DEPTH, DEC_BATCH, A_NG, A_P), 0.5)
    state_s5_im = nrm((DEPTH, DEC_BATCH, A_NG, A_P), 0.5)
    state_conv = nrm((DEPTH, DEC_BATCH, B_CONV - 1, B_XBC))
    state_ssm = nrm((DEPTH, DEC_BATCH, B_HEADS, B_HD, B_N), 0.1)
    w_ada = nrm((DEPTH, D_MODEL, 6 * D_MODEL), 0.5 * D_MODEL ** -0.5)
    b_ada = nrm((DEPTH, 6 * D_MODEL), 0.02)
    g_pre_mix = 1.0 + nrm((DEPTH, D_MODEL), 0.02)
    g_post_mix = 1.0 + nrm((DEPTH, D_MODEL), 0.02)
    g_pre_ffn = 1.0 + nrm((DEPTH, D_MODEL), 0.02)
    g_post_ffn = 1.0 + nrm((DEPTH, D_MODEL), 0.02)
    w_in = nrm((DEPTH, D_MODEL, D_IN), D_MODEL ** -0.5)
    w_out = nrm((DEPTH, MIX, D_MODEL), MIX ** -0.5)
    s5_a_re = -0.5 + nrm((DEPTH, A_NG, A_P), 0.01)
    s5_a_im = jnp.pi * jnp.arange(A_P, dtype=f32) + nrm((DEPTH, A_NG, A_P), 0.01)
    s5_log_dt = uni((DEPTH, A_NG), math.log(1e-3), math.log(1e-1))
    s5_b_re = nrm((DEPTH, A_NG, A_P, A_GROUP), (2 * A_GROUP) ** -0.5)
    s5_b_im = nrm((DEPTH, A_NG, A_P, A_GROUP), (2 * A_GROUP) ** -0.5)
    s5_c_re = nrm((DEPTH, A_NG, A_GROUP, A_P), (2 * A_P) ** -0.5)
    s5_c_im = nrm((DEPTH, A_NG, A_GROUP, A_P), (2 * A_P) ** -0.5)
    s5_d = nrm((DEPTH, A_CH))
    s5_w_glu = nrm((DEPTH, A_CH, A_CH), A_CH ** -0.5)
    s5_b_glu = nrm((DEPTH, A_CH), 0.02)
    ssd_conv_w = nrm((DEPTH, B_CONV, B_XBC), B_CONV ** -0.5)
    ssd_conv_b = nrm((DEPTH, B_XBC), 0.02)
    dt0 = jnp.exp(uni((DEPTH, B_HEADS), math.log(1e-3), math.log(1e-1)))
    ssd_dt_bias = dt0 + jnp.log(-jnp.expm1(-dt0))
    ssd_a_log = jnp.log(uni((DEPTH, B_HEADS), 1.0, 16.0))
    ssd_d = 1.0 + nrm((DEPTH, B_HEADS), 0.1)
    ssd_norm_g = 1.0 + nrm((DEPTH, B_CH), 0.02)
    fox_b_f = 3.0 + nrm((DEPTH, C_HEADS), 0.5)
    w_ff1 = nrm((DEPTH, D_MODEL, D_FF), D_MODEL ** -0.5)
    w_ff2 = nrm((DEPTH, D_FF, D_MODEL), D_FF ** -0.5)
    return {"x_prompt": x_prompt, "x_sample": x_sample, "c_prompt": c_prompt, "c_sample": c_sample,
            "cache_k": cache_k, "cache_v": cache_v, "cache_logf": cache_logf, "page_table": page_table,
            "state_s5_re": state_s5_re, "state_s5_im": state_s5_im, "state_conv": state_conv,
            "state_ssm": state_ssm, "w_ada": w_ada, "b_ada": b_ada, "g_pre_mix": g_pre_mix,
            "g_post_mix": g_post_mix, "g_pre_ffn": g_pre_ffn, "g_post_ffn": g_post_ffn,
            "w_in": w_in, "w_out": w_out, "s5_a_re": s5_a_re, "s5_a_im": s5_a_im,
            "s5_log_dt": s5_log_dt, "s5_b_re": s5_b_re, "s5_b_im": s5_b_im, "s5_c_re": s5_c_re,
            "s5_c_im": s5_c_im, "s5_d": s5_d, "s5_w_glu": s5_w_glu, "s5_b_glu": s5_b_glu,
            "ssd_conv_w": ssd_conv_w, "ssd_conv_b": ssd_conv_b, "ssd_dt_bias": ssd_dt_bias,
            "ssd_a_log": ssd_a_log, "ssd_d": ssd_d, "ssd_norm_g": ssd_norm_g, "fox_b_f": fox_b_f,
            "w_ff1": w_ff1, "w_ff2": w_ff2}


def reference(x_prompt, x_sample, c_prompt, c_sample, cache_k, cache_v, cache_logf, page_table,
              state_s5_re, state_s5_im, state_conv, state_ssm, w_ada, b_ada, g_pre_mix, g_post_mix,
              g_pre_ffn, g_post_ffn, w_in, w_out, s5_a_re, s5_a_im, s5_log_dt, s5_b_re, s5_b_im,
              s5_c_re, s5_c_im, s5_d, s5_w_glu, s5_b_glu, ssd_conv_w, ssd_conv_b, ssd_dt_bias,
              ssd_a_log, ssd_d, ssd_norm_g, fox_b_f, w_ff1, w_ff2):
    f32 = jnp.float32
    nb = x_prompt.shape[0]
    yp, ys = x_prompt, x_sample
    p_states = [[] for _ in range(7)]
    s_states = [[] for _ in range(7)]
    for l in range(DEPTH):
        lw = (w_ada[l], b_ada[l], g_pre_mix[l], g_post_mix[l], g_pre_ffn[l], g_post_ffn[l],
              w_in[l], w_out[l], s5_a_re[l], s5_a_im[l], s5_log_dt[l], s5_b_re[l], s5_b_im[l],
              s5_c_re[l], s5_c_im[l], s5_d[l], s5_w_glu[l], s5_b_glu[l], ssd_conv_w[l],
              ssd_conv_b[l], ssd_dt_bias[l], ssd_a_log[l], ssd_d[l], ssd_norm_g[l], fox_b_f[l],
              w_ff1[l], w_ff2[l])
        yp, st_p = sublayers(yp, c_prompt, fox_prompt,
                             jnp.zeros((nb, B_CONV - 1, B_XBC), yp.dtype),
                             jnp.zeros((nb, B_HEADS, B_HD, B_N), f32),
                             jnp.zeros((nb, A_NG, A_P), f32), jnp.zeros((nb, A_NG, A_P), f32), *lw)
        attend_s = functools.partial(fox_sample, cache_k, cache_v, cache_logf, page_table, l)
        ys, st_s = sublayers(ys, c_sample, attend_s, state_conv[l], state_ssm[l],
                             state_s5_re[l], state_s5_im[l], *lw)
        for lst, a in zip(p_states, st_p):
            lst.append(a)
        for lst, a in zip(s_states, st_s):
            lst.append(a)
    pk, pv, plf, ps5r, ps5i, pconv, pssm = [jnp.stack(a) for a in p_states]
    sk, sv, slf, ss5r, ss5i, sconv, sssm = [jnp.stack(a) for a in s_states]
    return (yp, ys, pk, pv, plf, ps5r, ps5i, pconv, pssm, sk, sv, slf, ss5r, ss5i, sconv, sssm)
```

```python
import functools
import math

import jax
import jax.numpy as jnp
from jax import lax
from jax.experimental import pallas as pl
from jax.experimental.pallas import tpu as pltpu

F32 = jnp.float32
BF16 = jnp.bfloat16
HI = lax.Precision.HIGHEST

D_MODEL = 1024
A_CH = 256
A_GROUP = 16
A_NG = 16
A_P = 64
S5_T = 16
S5_ROW = S5_T * A_GROUP
S5_LEVELS = 8
B_HD = 64
B_CH = 384
B_HEADS = 6
B_NG = 2
B_N = 128
B_CONV = 4
B_XBC = 896
B_CHUNK = 128
C_HD = 64
C_CH = 384
C_HEADS = 6
PAGE = 128
D_FF = 4096
EPS = 1e-6
PAD_W = 128
F_OFF = 6
N_ALL = A_CH + B_CH + B_XBC + 3 * C_CH + PAD_W
VMEM_LIMIT = 56 * 1024 * 1024


def _cp(*sem):
    return pltpu.CompilerParams(dimension_semantics=sem, vmem_limit_bytes=VMEM_LIMIT)


def _sds(shape, dtype):
    return jax.ShapeDtypeStruct(shape, dtype)


def _dot(a, b, precision=None):
    return jnp.dot(a, b, preferred_element_type=F32, precision=precision)


def _dot_nt(a, b, precision=None):
    return lax.dot_general(a, b, (((1,), (1,)), ((), ())), preferred_element_type=F32, precision=precision)


def _dot_tn(a, b, precision=None):
    return lax.dot_general(a, b, (((0,), (0,)), ((), ())), preferred_element_type=F32, precision=precision)


def _silu(x):
    return x * jax.nn.sigmoid(x)


def _softplus(x):
    return jnp.maximum(x, 0.0) + jnp.log1p(jnp.exp(-jnp.abs(x)))


def _log_sigmoid(x):
    return jnp.minimum(x, 0.0) - jnp.log1p(jnp.exp(-jnp.abs(x)))


def _gelu_tanh(x):
    return 0.5 * x * (1.0 + jnp.tanh(math.sqrt(2.0 / math.pi) * (x + 0.044715 * (x * x * x))))


def _rms(x, g):
    return x * lax.rsqrt(jnp.mean(x * x, axis=-1, keepdims=True) + EPS) * g


def _iota(shape, dim):
    return lax.broadcasted_iota(jnp.int32, shape, dim)


def _ada_body(c_ref, w_ref, b_ref, o_ref):
    s = _silu(c_ref[...]).astype(BF16)
    o_ref[...] = _dot(s, w_ref[...].astype(BF16)) + b_ref[...]


def _ada_mod(cond, w_ada, b_ada):
    depth = w_ada.shape[0]
    rows = cond.shape[0]
    tn = 1536
    return pl.pallas_call(
        _ada_body,
        out_shape=_sds((depth, rows, 6 * D_MODEL), F32),
        grid=(depth, 6 * D_MODEL // tn),
        in_specs=[pl.BlockSpec((rows, D_MODEL), lambda l, j: (0, 0)),
                  pl.BlockSpec((None, D_MODEL, tn), lambda l, j: (l, 0, j)),
                  pl.BlockSpec((None, 1, tn), lambda l, j: (l, 0, j))],
        out_specs=pl.BlockSpec((None, rows, tn), lambda l, j: (l, 0, j)),
        compiler_params=_cp("arbitrary", "arbitrary"),
        name="ada_mod",
    )(cond, w_ada, b_ada.reshape(depth, 1, 6 * D_MODEL))


def _mod_spec(rows, base, tiles_per_group, piece):
    return pl.BlockSpec((None, rows, D_MODEL), lambda i: (base + i // tiles_per_group, 0, piece))


def _inproj_body(x_ref, g_ref, sh_ref, sc_ref, w_ref,
                 u_ref, z_ref, xbc_ref, q_ref, kb_ref, vb_ref, k_ref, v_ref, dtf_ref, dtft_ref):
    h = _rms(x_ref[...], g_ref[...]) * (1.0 + sc_ref[...]) + sh_ref[...]
    hb = h.astype(BF16)

    def mm(a, b):
        return _dot(hb, w_ref[:, a:b])

    o = 0
    u_ref[...] = mm(o, o + A_CH)
    o += A_CH
    z_ref[...] = mm(o, o + B_CH)
    o += B_CH
    xbc_ref[...] = mm(o, o + B_XBC)
    o += B_XBC
    q_ref[...] = (mm(o, o + C_CH) * (C_HD ** -0.5)).astype(BF16)
    o += C_CH
    k = mm(o, o + C_CH)
    k_ref[...] = k
    kb_ref[...] = k.astype(BF16)
    o += C_CH
    v = mm(o, o + C_CH)
    v_ref[...] = v
    vb_ref[...] = v.astype(BF16)
    o += C_CH
    dtf = mm(o, o + PAD_W)
    dtf_ref[...] = dtf
    dtft_ref[...] = dtf.T[:16, :]


def _in_proj(x, mod, mod_rows, mod_base, tiles_per_group, g_pre, w_all, tm):
    t = x.shape[0]
    row = lambda w: pl.BlockSpec((tm, w), lambda i: (i, 0))
    outs = [_sds((t, A_CH), F32), _sds((t, B_CH), F32), _sds((t, B_XBC), F32),
            _sds((t, C_CH), BF16), _sds((t, C_CH), BF16), _sds((t, C_CH), BF16),
            _sds((t, C_CH), F32), _sds((t, C_CH), F32), _sds((t, PAD_W), F32), _sds((16, t), F32)]
    out_specs = [row(A_CH), row(B_CH), row(B_XBC), row(C_CH), row(C_CH), row(C_CH),
                 row(C_CH), row(C_CH), row(PAD_W), pl.BlockSpec((16, tm), lambda i: (0, i))]
    return pl.pallas_call(
        _inproj_body,
        out_shape=outs,
        grid=(t // tm,),
        in_specs=[row(D_MODEL),
                  pl.BlockSpec((1, D_MODEL), lambda i: (0, 0)),
                  _mod_spec(mod_rows, mod_base, tiles_per_group, 0),
                  _mod_spec(mod_rows, mod_base, tiles_per_group, 1),
                  pl.BlockSpec((D_MODEL, N_ALL), lambda i: (0, 0))],
        out_specs=out_specs,
        compiler_params=_cp("arbitrary"),
        name="in_proj",
    )(x, g_pre, mod, mod, w_all)


def _s5_param_body(pw_ref, ldt_ref, arr_ref, air_ref, arc_ref, aic_ref, btr_ref, bti_ref, ctr_ref, cti_ref,
                   m_ref, sre_ref, sim_ref, w_ref, err_ref, eii_ref, abr_ref, abi_ref, bbr_ref, bbi_ref):
    dt = jnp.exp(ldt_ref[...])
    ar, ai = arr_ref[...], air_ref[...]
    mag = jnp.exp(ar * dt)
    abr, abi = mag * jnp.cos(ai * dt), mag * jnp.sin(ai * dt)
    den = ar * ar + ai * ai
    nr, ni = abr - 1.0, abi
    fr, fi = (nr * ar + ni * ai) / den, (ni * ar - nr * ai) / den
    btr, bti = btr_ref[...], bti_ref[...]
    bbr, bbi = fr * btr - fi * bti, fr * bti + fi * btr
    abr_ref[...] = abr
    abi_ref[...] = abi
    bbr_ref[...] = bbr[0:A_GROUP]
    bbi_ref[...] = bbi[0:A_GROUP]

    def epow(k, a_r, a_i):
        m = jnp.exp(k * a_r * dt)
        th = k * a_i * dt
        return m * jnp.cos(th), m * jnp.sin(th)

    s = jnp.right_shift(_iota((S5_ROW, 1), 0), 4).astype(F32)
    enr, eni = epow(-s, ar, ai)
    a_re, a_im = bbr * enr - bbi * eni, bbr * eni + bbi * enr
    elr, eli = epow((S5_T - 1.0) - s, ar, ai)
    sre_ref[...] = (bbr * elr - bbi * eli).astype(BF16)
    sim_ref[...] = (bbr * eli + bbi * elr).astype(BF16)

    arc, aic = arc_ref[...], aic_ref[...]
    t = jnp.right_shift(_iota((1, S5_ROW), 1), 4).astype(F32)
    etr, eti = epow(t, arc, aic)
    ctr, cti = ctr_ref[...], cti_ref[...]
    bm_re, bm_im = ctr * etr - cti * eti, ctr * eti + cti * etr
    m = _dot(a_re, bm_re, HI) - _dot(a_im, bm_im, HI)
    m_ref[...] = jnp.where(s <= t, m, 0.0).astype(BF16)
    e1r, e1i = epow(t + 1.0, arc, aic)
    w_ref[0:A_P, :] = (ctr * e1r - cti * e1i).astype(BF16)
    w_ref[A_P:2 * A_P, :] = (-(ctr * e1i + cti * e1r)).astype(BF16)

    lr, li = epow(pw_ref[...], ar, ai)
    err_ref[...] = jnp.concatenate([lr, lr], axis=1)
    eii_ref[...] = jnp.concatenate([-li, li], axis=1)


def _s5_params(a_re, a_im, log_dt, b_re, b_im, c_re, c_im):
    depth = a_re.shape[0]
    pw = (S5_T * 2.0 ** jnp.arange(S5_LEVELS, dtype=F32)).reshape(S5_LEVELS, 1)
    arr, air = a_re.reshape(depth, A_NG, 1, A_P), a_im.reshape(depth, A_NG, 1, A_P)
    arc, aic = a_re.reshape(depth, A_NG, A_P, 1), a_im.reshape(depth, A_NG, A_P, 1)
    ldt = log_dt.reshape(depth, A_NG, 1, 1)
    bt = lambda b: jnp.tile(jnp.swapaxes(b, 2, 3), (1, 1, S5_T, 1))
    ct = lambda c: jnp.tile(jnp.swapaxes(c, 2, 3), (1, 1, 1, S5_T))
    blk = lambda r, c: pl.BlockSpec((None, None, r, c), lambda l, g: (l, g, 0, 0))
    outs = [_sds((depth, A_NG, S5_ROW, S5_ROW), BF16),
            _sds((depth, A_NG, S5_ROW, A_P), BF16), _sds((depth, A_NG, S5_ROW, A_P), BF16),
            _sds((depth, A_NG, 2 * A_P, S5_ROW), BF16),
            _sds((depth, A_NG, S5_LEVELS, 2 * A_P), F32), _sds((depth, A_NG, S5_LEVELS, 2 * A_P), F32),
            _sds((depth, A_NG, 1, A_P), F32), _sds((depth, A_NG, 1, A_P), F32),
            _sds((depth, A_NG, A_GROUP, A_P), F32), _sds((depth, A_NG, A_GROUP, A_P), F32)]
    out_specs = [blk(S5_ROW, S5_ROW), blk(S5_ROW, A_P), blk(S5_ROW, A_P), blk(2 * A_P, S5_ROW),
                 blk(S5_LEVELS, 2 * A_P), blk(S5_LEVELS, 2 * A_P), blk(1, A_P), blk(1, A_P),
                 blk(A_GROUP, A_P), blk(A_GROUP, A_P)]
    return pl.pallas_call(
        _s5_param_body,
        out_shape=outs,
        grid=(depth, A_NG),
        in_specs=[pl.BlockSpec((S5_LEVELS, 1), lambda l, g: (0, 0)),
                  blk(1, 1), blk(1, A_P), blk(1, A_P), blk(A_P, 1), blk(A_P, 1),
                  blk(S5_ROW, A_P), blk(S5_ROW, A_P), blk(A_P, S5_ROW), blk(A_P, S5_ROW)],
        out_specs=out_specs,
        compiler_params=_cp("arbitrary", "arbitrary"),
        name="s5_params",
    )(pw, ldt, arr, air, arc, aic, bt(b_re), bt(b_im), ct(c_re), ct(c_im))


def _s5_prompt_body(x_ref, m_ref, s_ref, w_ref, err_ref, eii_ref, d_ref, y_ref, hf_ref, *, n, chunks, levels):
    x = x_ref[...]
    xb = x.astype(BF16)
    yi = _dot(xb, m_ref[...])
    b = _dot(xb, s_ref[...])
    cidx = jnp.bitwise_and(_iota((n * chunks, 1), 0), chunks - 1)
    for k in range(levels):
        d = 1 << k
        sh = jnp.where(cidx >= d, pltpu.roll(b, d, axis=0), 0.0)
        sw = pltpu.roll(sh, A_P, axis=1)
        b = b + err_ref[k:k + 1, :] * sh + eii_ref[k:k + 1, :] * sw
    hp = jnp.where(cidx >= 1, pltpu.roll(b, 1, axis=0), 0.0)
    y_ref[...] = yi + _dot(hp.astype(BF16), w_ref[...]) + d_ref[...] * x
    for i in range(n):
        hf_ref[i:i + 1, :] = b[(i + 1) * chunks - 1:(i + 1) * chunks, :]


def _s5_prompt(xt, m, s, w, err, eii, d_t, n, chunks):
    rows = n * chunks
    levels = max(1, (chunks - 1).bit_length())
    assert chunks & (chunks - 1) == 0 and levels <= S5_LEVELS
    blk = lambda r, c: pl.BlockSpec((None, r, c), lambda g: (g, 0, 0))
    return pl.pallas_call(
        functools.partial(_s5_prompt_body, n=n, chunks=chunks, levels=levels),
        out_shape=[_sds((A_NG, rows, S5_ROW), F32), _sds((A_NG, n, 2 * A_P), F32)],
        grid=(A_NG,),
        in_specs=[blk(rows, S5_ROW), blk(S5_ROW, S5_ROW), blk(S5_ROW, 2 * A_P), blk(2 * A_P, S5_ROW),
                  blk(S5_LEVELS, 2 * A_P), blk(S5_LEVELS, 2 * A_P), blk(1, S5_ROW)],
        out_specs=[blk(rows, S5_ROW), blk(n, 2 * A_P)],
        compiler_params=_cp("arbitrary"),
        name="s5_prompt",
    )(xt, m, s, w, err, eii, d_t)


def _s5_sample_body(u_ref, hr_ref, hi_ref, abr_ref, abi_ref, bre_ref, bim_ref, cre_ref, cim_ref, d_ref,
                    y_ref, sr_ref, si_ref):
    u = u_ref[...]
    ub = u.astype(BF16)
    hr, hi = hr_ref[...], hi_ref[...]
    abr, abi = abr_ref[...], abi_ref[...]
    sr = abr * hr - abi * hi + _dot(ub, bre_ref[...])
    si = abr * hi + abi * hr + _dot(ub, bim_ref[...])
    sr_ref[...] = sr
    si_ref[...] = si
    y_ref[...] = _dot(sr.astype(BF16), cre_ref[...]) - _dot(si.astype(BF16), cim_ref[...]) + d_ref[...] * u


def _s5_sample(u, hr, hi, abr, abi, bre, bim, cre, cim, d):
    b = u.shape[0]
    w = A_NG * A_P
    return pl.pallas_call(
        _s5_sample_body,
        out_shape=[_sds((b, A_CH), F32), _sds((b, w), F32), _sds((b, w), F32)],
        compiler_params=pltpu.CompilerParams(vmem_limit_bytes=VMEM_LIMIT),
        name="s5_sample",
    )(u, hr, hi, abr, abi, bre, bim, cre, cim, d)


def _ssd_prompt_body(xbc_ref, z_ref, dtc_ref, dtr_ref, cprev_ref, h0_ref, cw_ref, cb_ref,
                     dtbc_ref, dtbr_ref, alc_ref, alr_ref, dsk_ref, ng_ref,
                     y_ref, hf_ref, ext_scr, h_scr):
    c = pl.program_id(1)
    q = B_CHUNK

    @pl.when(c == 0)
    def _():
        ext_scr[0:8, :] = cprev_ref[...]
        h_scr[...] = h0_ref[...]

    x = xbc_ref[...]
    ext_scr[8:8 + q, :] = x
    cw = cw_ref[...]
    conv = (cb_ref[...] + cw[3:4] * x + cw[2:3] * ext_scr[7:7 + q, :]
            + cw[1:2] * ext_scr[6:6 + q, :] + cw[0:1] * ext_scr[5:5 + q, :])
    ext_scr[0:8, :] = x[q - 8:q, :]
    xc = _silu(conv)

    r_i, c_i = _iota((q, q), 0), _iota((q, q), 1)
    causal = r_i >= c_i
    dtc = _softplus(dtc_ref[...] + dtbc_ref[...])
    da_c = dtc * (-jnp.exp(alc_ref[...]))
    acc_c = _dot(causal.astype(F32), da_c, HI)
    dtr = _softplus(dtr_ref[...] + dtbr_ref[...])
    da_r = dtr * (-jnp.exp(alr_ref[...]))
    acc_r = _dot(da_r, (r_i <= c_i).astype(F32), HI)

    bm = [xc[:, B_CH + B_N * g:B_CH + B_N * (g + 1)].astype(BF16) for g in range(B_NG)]
    cm = [xc[:, B_CH + B_N * (B_NG + g):B_CH + B_N * (B_NG + g + 1)].astype(BF16) for g in range(B_NG)]
    cbm = [_dot_nt(cm[g], bm[g]) for g in range(B_NG)]
    lane = _iota((q, 128), 1)
    heads_per_group = B_HEADS // B_NG
    ys = []
    for j in range(B_HEADS // 2):
        xs_pair = xc[:, 128 * j:128 * (j + 1)]
        dt_pair = jnp.where(lane < B_HD, dtc[:, 2 * j:2 * j + 1], dtc[:, 2 * j + 1:2 * j + 2])
        xdt = xs_pair * dt_pair
        xdt_b = xdt.astype(BF16)
        hprev_b = h_scr[2 * j:2 * j + 2].reshape(2 * B_HD, B_N).astype(BF16)
        outs = []
        for e in range(2):
            h = 2 * j + e
            g = h // heads_per_group
            ac = acc_c[:, h:h + 1]
            ar = acc_r[h:h + 1, :]
            lm = jnp.exp(jnp.where(causal, ac - ar, -jnp.inf))
            yd = _dot((cbm[g] * lm).astype(BF16), xdt_b)
            yo = _dot_nt(cm[g], hprev_b) * jnp.exp(ac)
            outs.append(yd + yo)
            alast = acc_r[h:h + 1, q - 1:q]
            st = _dot_tn((xdt * jnp.exp(alast - ac)).astype(BF16), bm[g])
            h_scr[h] = jnp.exp(alast) * h_scr[h] + st[B_HD * e:B_HD * (e + 1), :]
        ys.append(jnp.where(lane < B_HD, outs[0], outs[1]) + dsk_ref[:, 128 * j:128 * (j + 1)] * xs_pair)
    y = jnp.concatenate(ys, axis=1)
    y_ref[...] = _rms(y * _silu(z_ref[...]), ng_ref[...]).astype(y_ref.dtype)

    @pl.when(c == pl.num_programs(1) - 1)
    def _():
        hf_ref[...] = h_scr[...]


def _ssd_prompt(xbc, z, dtf, dtft, cprev, h0, cw, cb, dtb_c, dtb_r, al_c, al_r, dsk, ng, n, length):
    nc = length // B_CHUNK
    q = B_CHUNK
    tok = lambda w: pl.BlockSpec((q, w), lambda i, c: (i * nc + c, 0))
    const = lambda r, w: pl.BlockSpec((r, w), lambda i, c: (0, 0))
    return pl.pallas_call(
        _ssd_prompt_body,
        out_shape=[_sds((n * length, B_CH), BF16), _sds((n, B_HEADS, B_HD, B_N), F32)],
        grid=(n, nc),
        in_specs=[tok(B_XBC), tok(B_CH), tok(PAD_W),
                  pl.BlockSpec((16, q), lambda i, c: (0, i * nc + c)),
                  pl.BlockSpec((None, 8, B_XBC), lambda i, c: (i, 0, 0)),
                  pl.BlockSpec((None, B_HEADS, B_HD, B_N), lambda i, c: (i, 0, 0, 0)),
                  const(B_CONV, B_XBC), const(1, B_XBC), const(1, PAD_W), const(16, 1),
                  const(1, PAD_W), const(16, 1), const(1, B_CH), const(1, B_CH)],
        out_specs=[tok(B_CH), pl.BlockSpec((None, B_HEADS, B_HD, B_N), lambda i, c: (i, 0, 0, 0))],
        scratch_shapes=[pltpu.VMEM((8 + q, B_XBC), F32), pltpu.VMEM((B_HEADS, B_HD, B_N), F32)],
        compiler_params=_cp("arbitrary", "arbitrary"),
        name="ssd_prompt",
    )(xbc, z, dtf, dtft, cprev, h0, cw, cb, dtb_c, dtb_r, al_c, al_r, dsk, ng)


def _ssd_sample_body(xbc_ref, p0_ref, p1_ref, p2_ref, z_ref, dtf_ref, h0_ref, cw_ref, cb_ref,
                     dtb_ref, al_ref, ex_ref, dsk_ref, ng_ref, y_ref, hn_ref, y_scr, *, tb):
    x = xbc_ref[...]
    cw = cw_ref[...]
    conv = cb_ref[...] + cw[0:1] * p0_ref[...] + cw[1:2] * p1_ref[...] + cw[2:3] * p2_ref[...] + cw[3:4] * x
    xc = _silu(conv)
    dt = _softplus(dtf_ref[...] + dtb_ref[...])
    e = jnp.exp(dt * (-jnp.exp(al_ref[...])))
    ex = ex_ref[...]
    dt_e = _dot(dt, ex, HI)
    e_e = _dot(e, ex, HI)
    xs = xc[:, :B_CH]
    xdt = xs * dt_e
    half = B_CH // B_NG
    row8, lane8 = _iota((8, B_CH), 0), _iota((8, B_CH), 1)
    r8 = _iota((8, B_N), 0)
    lane1 = _iota((1, B_CH), 1)
    for b in range(tb):
        xr = xdt[b:b + 1, :]
        lhs = jnp.where(row8 == 0, e_e[b:b + 1, :],
                        jnp.where(((row8 == 1) & (lane8 < half)) | ((row8 == 2) & (lane8 >= half)), xr, 0.0))
        rhs_e = jnp.where(r8 == 0, 1.0, 0.0)
        rhs_s = jnp.where(r8 == 1, xc[b:b + 1, B_CH:B_CH + B_N],
                          jnp.where(r8 == 2, xc[b:b + 1, B_CH + B_N:B_CH + 2 * B_N], 0.0))
        hn = _dot_tn(lhs, rhs_e, HI) * h0_ref[b] + _dot_tn(lhs, rhs_s, HI)
        hn_ref[b] = hn
        cmat = jnp.where(r8 == 0, xc[b:b + 1, B_CH + 2 * B_N:B_CH + 3 * B_N],
                         jnp.where(r8 == 1, xc[b:b + 1, B_CH + 3 * B_N:B_CH + 4 * B_N], 0.0))
        yt = _dot_nt(cmat.astype(BF16), hn.astype(BF16))
        y_scr[b:b + 1, :] = jnp.where(lane1 < half, yt[0:1, :], yt[1:2, :])
    y = y_scr[...] + dsk_ref[...] * xs
    y_ref[...] = _rms(y * _silu(z_ref[...]), ng_ref[...]).astype(y_ref.dtype)


def _ssd_sample(xbc, p0, p1, p2, z, dtf, h0, cw, cb, dtb, al, ex, dsk, ng):
    b = xbc.shape[0]
    tb = 8
    row = lambda w: pl.BlockSpec((tb, w), lambda i: (i, 0))
    const = lambda r, w: pl.BlockSpec((r, w), lambda i: (0, 0))
    st = pl.BlockSpec((tb, B_CH, B_N), lambda i: (i, 0, 0))
    return pl.pallas_call(
        functools.partial(_ssd_sample_body, tb=tb),
        out_shape=[_sds((b, B_CH), BF16), _sds((b, B_CH, B_N), F32)],
        grid=(b // tb,),
        in_specs=[row(B_XBC), row(B_XBC), row(B_XBC), row(B_XBC), row(B_CH), row(PAD_W), st,
                  const(B_CONV, B_XBC), const(1, B_XBC), const(1, PAD_W), const(1, PAD_W),
                  const(PAD_W, B_CH), const(1, B_CH), const(1, B_CH)],
        out_specs=[row(B_CH), st],
        scratch_shapes=[pltpu.VMEM((tb, B_CH), F32)],
        compiler_params=_cp("arbitrary"),
        name="ssd_sample",
    )(xbc, p0, p1, p2, z, dtf, h0, cw, cb, dtb, al, ex, dsk, ng)


def _fox_prep_body(fc_ref, fr_ref, bfr_ref, bfc_ref, lf_ref, cc_ref, cr_ref, carry_c, carry_r):
    j = pl.program_id(1)
    tl = fc_ref.shape[0]

    @pl.when(j == 0)
    def _():
        carry_c[...] = jnp.zeros_like(carry_c)
        carry_r[...] = jnp.zeros_like(carry_r)

    r_i, c_i = _iota((tl, tl), 0), _iota((tl, tl), 1)
    lfc = _log_sigmoid(fc_ref[...] + bfr_ref[...])
    lf_ref[...] = lfc
    cc = _dot((r_i >= c_i).astype(F32), lfc, HI) + carry_c[0:1, :]
    cc_ref[...] = cc
    carry_c[...] = jnp.broadcast_to(cc[tl - 1:tl, :], carry_c.shape)
    lfr = _log_sigmoid(fr_ref[...] + bfc_ref[...])
    cr = _dot(lfr, (r_i <= c_i).astype(F32), HI) + carry_r[:, 0:1]
    cr_ref[...] = cr
    carry_r[...] = jnp.broadcast_to(cr[:, tl - 1:tl], carry_r.shape)


def _fox_prep(dtf, dtft, bf_row, bf_col, n, length):
    tl = 256
    nt = length // tl
    t = n * length
    tok = pl.BlockSpec((tl, PAD_W), lambda i, j: (i * nt + j, 0))
    rowb = pl.BlockSpec((16, tl), lambda i, j: (0, i * nt + j))
    return pl.pallas_call(
        _fox_prep_body,
        out_shape=[_sds((t, PAD_W), F32), _sds((t, PAD_W), F32), _sds((16, t), F32)],
        grid=(n, nt),
        in_specs=[tok, rowb, pl.BlockSpec((1, PAD_W), lambda i, j: (0, 0)),
                  pl.BlockSpec((16, 1), lambda i, j: (0, 0))],
        out_specs=[tok, tok, rowb],
        scratch_shapes=[pltpu.VMEM((8, PAD_W), F32), pltpu.VMEM((16, 128), F32)],
        compiler_params=_cp("arbitrary", "arbitrary"),
        name="fox_prep",
    )(dtf, dtft, bf_row, bf_col)


def _fox_flash_body(q_ref, k_ref, v_ref, cc_ref, cr_ref, o_ref, q_scr, cq_scr, m_scr, l_scr, acc_scr):
    hp, qi, ki = pl.program_id(1), pl.program_id(2), pl.program_id(3)
    tq, tk = q_ref.shape[0], k_ref.shape[0]
    lane = _iota((tq, 128), 1)

    @pl.when(ki == 0)
    def _():
        q = q_ref[...]
        cc = cc_ref[...]
        for e in range(2):
            q_scr[e] = jnp.where((lane < C_HD) == (e == 0), q, jnp.zeros_like(q))
            cq_scr[e] = jnp.sum(jnp.where(lane == F_OFF + 2 * hp + e, cc, 0.0), axis=1, keepdims=True)
        m_scr[...] = jnp.full(m_scr.shape, -jnp.inf, F32)
        l_scr[...] = jnp.zeros_like(l_scr)
        acc_scr[...] = jnp.zeros_like(acc_scr)

    @pl.when(ki <= qi)
    def _():
        k = k_ref[...]
        v = v_ref[...]
        keep = (qi * tq + _iota((tq, tk), 0)) >= (ki * tk + _iota((tq, tk), 1))
        for e in range(2):
            ck = cr_ref[pl.ds(F_OFF + 2 * hp + e, 1), :]
            s = _dot_nt(q_scr[e], k) + (cq_scr[e] - ck)
            s = jnp.where(keep, s, -jnp.inf)
            m_old = m_scr[e]
            m_new = jnp.maximum(m_old, jnp.max(s, axis=1, keepdims=True))
            alpha = jnp.exp(m_old - m_new)
            p = jnp.exp(s - m_new)
            l_scr[e] = alpha * l_scr[e] + jnp.sum(p, axis=1, keepdims=True)
            acc_scr[e] = alpha * acc_scr[e] + _dot(p.astype(BF16), v)
            m_scr[e] = m_new

    @pl.when(ki == qi)
    def _():
        o = jnp.where(lane < C_HD, acc_scr[0] / l_scr[0], acc_scr[1] / l_scr[1])
        o_ref[...] = o.astype(o_ref.dtype)


def _fox_flash(q, k, v, cc, cr, n, length):
    tq = tk = 256
    nq = length // tq
    qmap = lambda i, h, a, b: (i * nq + a, h)
    kmap = lambda i, h, a, b: (i * nq + jnp.minimum(a, b), h)
    return pl.pallas_call(
        _fox_flash_body,
        out_shape=_sds((n * length, C_CH), BF16),
        grid=(n, C_HEADS // 2, nq, nq),
        in_specs=[pl.BlockSpec((tq, 128), qmap), pl.BlockSpec((tk, 128), kmap), pl.BlockSpec((tk, 128), kmap),
                  pl.BlockSpec((tq, PAD_W), lambda i, h, a, b: (i * nq + a, 0)),
                  pl.BlockSpec((16, tk), lambda i, h, a, b: (0, i * nq + jnp.minimum(a, b)))],
        out_specs=pl.BlockSpec((tq, 128), qmap),
        scratch_shapes=[pltpu.VMEM((2, tq, 128), BF16), pltpu.VMEM((2, tq, 1), F32),
                        pltpu.VMEM((2, tq, 1), F32), pltpu.VMEM((2, tq, 1), F32),
                        pltpu.VMEM((2, tq, 128), F32)],
        compiler_params=_cp("arbitrary", "arbitrary", "arbitrary", "arbitrary"),
        name="fox_flash",
    )(q, k, v, cc, cr)


def _fox_sample_body(pt_ref, q_ref, kn_ref, vn_ref, fn_ref, bf_ref, sel_ref, t2_ref, *refs, pages):
    k_refs = refs[0:pages]
    v_refs = refs[pages:2 * pages]
    f_refs = refs[2 * pages:3 * pages]
    o_ref, lf_ref = refs[3 * pages], refs[3 * pages + 1]
    row, lane = _iota((8, C_CH), 0), _iota((8, C_CH), 1)
    own = jnp.right_shift(lane, 6) == row
    qm = jnp.where(own, q_ref[...].astype(F32), 0.0)
    qb = qm.astype(BF16)
    lfn = _log_sigmoid(fn_ref[...] + bf_ref[...])
    lf_ref[...] = lfn
    carry = jnp.sum(jnp.where(_iota((8, PAD_W), 1) == F_OFF + _iota((8, PAD_W), 0), lfn, 0.0),
                    axis=1, keepdims=True)
    sel = sel_ref[...]
    t2 = t2_ref[...]
    s = [None] * pages
    for p in reversed(range(pages)):
        l8 = sel * f_refs[p][...]
        s[p] = _dot_nt(qb, k_refs[p][...].astype(BF16)) + _dot(l8, t2, HI) + carry
        carry = carry + jnp.sum(l8, axis=1, keepdims=True)
    s_new = jnp.sum(qm * kn_ref[...], axis=1, keepdims=True)
    m = s_new
    for p in range(pages):
        m = jnp.maximum(m, jnp.max(s[p], axis=1, keepdims=True))
    p_new = jnp.exp(s_new - m)
    l = p_new
    acc = p_new * vn_ref[...]
    for p in range(pages):
        pr = jnp.exp(s[p] - m)
        l = l + jnp.sum(pr, axis=1, keepdims=True)
        acc = acc + _dot(pr.astype(BF16), v_refs[p][...].astype(BF16))
    o_ref[...] = jnp.sum(jnp.where(own, acc / l, 0.0), axis=0, keepdims=True)


def _fox_sample(page_table, layer, q, k_new, v_new, f_new, bf_row, sel, t2, cache_k, cache_v, cache_logf):
    b = q.shape[0]
    pages = page_table.shape[1]
    tok = lambda w: pl.BlockSpec((None, 1, w), lambda i, pt: (i, 0, 0))
    const = lambda r, w: pl.BlockSpec((r, w), lambda i, pt: (0, 0))
    page = lambda p, r, w: pl.BlockSpec((None, None, r, w), lambda i, pt: (layer, pt[i, p], 0, 0))
    in_specs = ([tok(C_CH), tok(C_CH), tok(C_CH), tok(PAD_W), const(1, PAD_W),
                 const(8, PAGE * C_HEADS), const(PAGE * C_HEADS, PAGE)]
                + [page(p, PAGE, C_CH) for p in range(pages)]
                + [page(p, PAGE, C_CH) for p in range(pages)]
                + [page(p, 1, PAGE * C_HEADS) for p in range(pages)])
    grid_spec = pltpu.PrefetchScalarGridSpec(
        num_scalar_prefetch=1, grid=(b,), in_specs=in_specs,
        out_specs=[tok(C_CH), tok(PAD_W)])
    return pl.pallas_call(
        functools.partial(_fox_sample_body, pages=pages),
        out_shape=[_sds((b, 1, C_CH), F32), _sds((b, 1, PAD_W), F32)],
        grid_spec=grid_spec,
        compiler_params=_cp("arbitrary"),
        name="fox_sample",
    )(page_table, q.reshape(b, 1, C_CH), k_new.reshape(b, 1, C_CH), v_new.reshape(b, 1, C_CH),
      f_new.reshape(b, 1, PAD_W), bf_row, sel, t2,
      *([cache_k] * pages), *([cache_v] * pages), *([cache_logf] * pages))


def _outproj_body(x_ref, ya_ref, yb_ref, yc_ref, wg_ref, bg_ref, wo_ref, gp_ref, g1_ref, o_ref):
    y = _gelu_tanh(ya_ref[...])
    gate = jax.nn.sigmoid(_dot(y.astype(BF16), wg_ref[...]) + bg_ref[...])
    ya = (y * gate).astype(BF16)
    mix = (_dot(ya, wo_ref[0:A_CH, :]) + _dot(yb_ref[...], wo_ref[A_CH:A_CH + B_CH, :])
           + _dot(yc_ref[...], wo_ref[A_CH + B_CH:, :]))
    o_ref[...] = x_ref[...] + g1_ref[...] * _rms(mix, gp_ref[...])


def _out_proj(x, ya, yb, yc, w_glu, b_glu, w_out, g_post, mod, mod_rows, mod_base, tiles_per_group, tm):
    t = x.shape[0]
    row = lambda w: pl.BlockSpec((tm, w), lambda i: (i, 0))
    const = lambda r, w: pl.BlockSpec((r, w), lambda i: (0, 0))
    return pl.pallas_call(
        _outproj_body,
        out_shape=_sds((t, D_MODEL), F32),
        grid=(t // tm,),
        in_specs=[row(D_MODEL), row(A_CH), row(B_CH), row(C_CH), const(A_CH, A_CH), const(1, A_CH),
                  const(D_MODEL, D_MODEL), const(1, D_MODEL), _mod_spec(mod_rows, mod_base, tiles_per_group, 2)],
        out_specs=row(D_MODEL),
        compiler_params=_cp("arbitrary"),
        name="out_proj",
    )(x, ya, yb, yc, w_glu, b_glu, w_out, g_post, mod)


def _ffn_body(x_ref, gpre_ref, sh_ref, sc_ref, w1_ref, w2_ref, gpost_ref, g2_ref, o_ref, *, tf):
    x = x_ref[...]
    hb = (_rms(x, gpre_ref[...]) * (1.0 + sc_ref[...]) + sh_ref[...]).astype(BF16)
    acc = jnp.zeros(x.shape, F32)
    for j in range(D_FF // tf):
        a = jnp.maximum(_dot(hb, w1_ref[:, j * tf:(j + 1) * tf]), 0.0)
        acc = acc + _dot((a * a).astype(BF16), w2_ref[j * tf:(j + 1) * tf, :])
    o_ref[...] = x + g2_ref[...] * _rms(acc, gpost_ref[...])


def _ffn(x, g_pre, w1, w2, g_post, mod, mod_rows, mod_base, tiles_per_group, tm):
    t = x.shape[0]
    row = pl.BlockSpec((tm, D_MODEL), lambda i: (i, 0))
    const = lambda r, w: pl.BlockSpec((r, w), lambda i: (0, 0))
    weight = lambda r, w: pl.BlockSpec((r, w), lambda i: (0, 0), pipeline_mode=pl.Buffered(1))
    return pl.pallas_call(
        functools.partial(_ffn_body, tf=1024),
        out_shape=_sds((t, D_MODEL), F32),
        grid=(t // tm,),
        in_specs=[row, const(1, D_MODEL), _mod_spec(mod_rows, mod_base, tiles_per_group, 3),
                  _mod_spec(mod_rows, mod_base, tiles_per_group, 4), weight(D_MODEL, D_FF), weight(D_FF, D_MODEL),
                  const(1, D_MODEL), _mod_spec(mod_rows, mod_base, tiles_per_group, 5)],
        out_specs=row,
        compiler_params=_cp("arbitrary"),
        name="ffn",
    )(x, g_pre, mod, mod, w1, w2, g_post, mod)


def _pad_lanes(v, offset, width=PAD_W):
    return jnp.zeros((1, width), F32).at[0, offset:offset + v.shape[0]].set(v)


def _pad_rows(v, offset, rows=16):
    return jnp.zeros((rows, 1), F32).at[offset:offset + v.shape[0], 0].set(v)


def kernel(x_prompt, x_sample, c_prompt, c_sample, cache_k, cache_v, cache_logf, page_table, state_s5_re, state_s5_im, state_conv, state_ssm, w_ada, b_ada, g_pre_mix, g_post_mix, g_pre_ffn, g_post_ffn, w_in, w_out, s5_a_re, s5_a_im, s5_log_dt, s5_b_re, s5_b_im, s5_c_re, s5_c_im, s5_d, s5_w_glu, s5_b_glu, ssd_conv_w, ssd_conv_b, ssd_dt_bias, ssd_a_log, ssd_d, ssd_norm_g, fox_b_f, w_ff1, w_ff2):
    depth = w_ada.shape[0]
    n, length, _ = x_prompt.shape
    nb = x_sample.shape[0]
    t = n * length
    n_pool = cache_k.shape[1]
    chunks = length // S5_T
    tm_p = 512
    assert x_sample.shape[1] == 1 and length % tm_p == 0 and nb % 8 == 0

    rows = -(-(n + nb) // 8) * 8
    cond = jnp.concatenate([c_prompt, c_sample, jnp.zeros((rows - n - nb, D_MODEL), F32)], axis=0)
    mod = _ada_mod(cond, w_ada, b_ada)
    mod_p = mod[:, :n].reshape(depth * n, 1, 6 * D_MODEL)
    mod_s = mod[:, n:n + nb]

    s5m, s5sre, s5sim, s5w, s5err, s5eii, s5abr, s5abi, s5bbr, s5bbi = _s5_params(
        s5_a_re, s5_a_im, s5_log_dt, s5_b_re, s5_b_im, s5_c_re, s5_c_im)
    s5s = jnp.concatenate([s5sre, s5sim], axis=-1)
    eye_g = jnp.eye(A_NG, dtype=F32)

    flat = jnp.arange(PAGE * C_HEADS)
    sel = (flat[None, :] % C_HEADS == jnp.arange(8)[:, None]).astype(F32)
    t2 = (flat[:, None] // C_HEADS > jnp.arange(PAGE)[None, :]).astype(F32)
    expand = (jnp.arange(PAD_W)[:, None] == jnp.arange(B_CH)[None, :] // B_HD).astype(F32)

    ck = cache_k.reshape(depth, n_pool, PAGE, C_CH)
    cv = cache_v.reshape(depth, n_pool, PAGE, C_CH)
    clf = cache_logf.reshape(depth, n_pool, 1, PAGE * C_HEADS)

    xp = x_prompt.reshape(t, D_MODEL)
    xs = x_sample.reshape(nb, D_MODEL)
    zeros_conv = jnp.zeros((n, 8, B_XBC), F32)
    zeros_ssm = jnp.zeros((n, B_HEADS, B_HD, B_N), F32)
    p_out = [[] for _ in range(7)]
    s_out = [[] for _ in range(7)]

    for l in range(depth):
        wi = w_in[l]
        o_dt = A_CH + B_CH + B_XBC
        o_f = o_dt + B_HEADS + 3 * C_CH
        w_all = jnp.concatenate(
            [wi[:, :o_dt], wi[:, o_dt + B_HEADS:o_f], wi[:, o_dt:o_dt + B_HEADS], wi[:, o_f:],
             jnp.zeros((D_MODEL, PAD_W - B_HEADS - C_HEADS), F32)], axis=1).astype(BF16)
        w_o = w_out[l].astype(BF16)
        w_g = s5_w_glu[l].astype(BF16)
        w1 = w_ff1[l].astype(BF16)
        w2 = w_ff2[l].astype(BF16)
        row = lambda v: v.reshape(1, -1)
        g_pm, g_qm, g_pf, g_qf = row(g_pre_mix[l]), row(g_post_mix[l]), row(g_pre_ffn[l]), row(g_post_ffn[l])
        b_g = row(s5_b_glu[l])
        cw, cb = ssd_conv_w[l], row(ssd_conv_b[l])
        dtb_c, dtb_r = _pad_lanes(ssd_dt_bias[l], 0), _pad_rows(ssd_dt_bias[l], 0)
        al_c, al_r = _pad_lanes(ssd_a_log[l], 0), _pad_rows(ssd_a_log[l], 0)
        dsk = row(jnp.repeat(ssd_d[l], B_HD))
        ng = row(ssd_norm_g[l])
        bf_row, bf_col = _pad_lanes(fox_b_f[l], F_OFF), _pad_rows(fox_b_f[l], F_OFF)
        d_row = row(s5_d[l])
        d_t = jnp.tile(s5_d[l].reshape(A_NG, 1, A_GROUP), (1, 1, S5_T))

        u, z, xbc, q, kb, vb, k, v, dtf, dtft = _in_proj(
            xp, mod_p, 1, l * n, length // tm_p, g_pm, w_all, tm_p)
        xt = u.reshape(n, chunks, S5_T, A_NG, A_GROUP).transpose(3, 0, 1, 2, 4).reshape(A_NG, n * chunks, S5_ROW)
        yt, hfin = _s5_prompt(xt, s5m[l], s5s[l], s5w[l], s5err[l], s5eii[l], d_t, n, chunks)
        ya = yt.reshape(A_NG, n, chunks, S5_T, A_GROUP).transpose(1, 2, 3, 0, 4).reshape(t, A_CH)
        yb, ssm_p = _ssd_prompt(xbc, z, dtf, dtft, zeros_conv, zeros_ssm, cw, cb, dtb_c, dtb_r,
                                al_c, al_r, dsk, ng, n, length)
        lf, cc, cr = _fox_prep(dtf, dtft, bf_row, bf_col, n, length)
        yc = _fox_flash(q, kb, vb, cc, cr, n, length)
        xp = _out_proj(xp, ya, yb, yc, w_g, b_g, w_o, g_qm, mod_p, 1, l * n, length // tm_p, tm_p)
        xp = _ffn(xp, g_pf, w1, w2, g_qf, mod_p, 1, l * n, length // tm_p, tm_p)
        p_out[0].append(k.reshape(n, length, C_HEADS, C_HD))
        p_out[1].append(v.reshape(n, length, C_HEADS, C_HD))
        p_out[2].append(lf[:, F_OFF:F_OFF + C_HEADS].reshape(n, length, C_HEADS))
        hfin = hfin.transpose(1, 0, 2)
        p_out[3].append(hfin[..., :A_P])
        p_out[4].append(hfin[..., A_P:])
        p_out[5].append(xbc.reshape(n, length, B_XBC)[:, length - (B_CONV - 1):])
        p_out[6].append(ssm_p)

        u, z, xbc, q, kb, vb, k, v, dtf, dtft = _in_proj(xs, mod_s, nb, l, 1, g_pm, w_all, nb)
        bre = jnp.einsum("gcp,gh->gchp", s5bbr[l], eye_g).reshape(A_CH, A_NG * A_P).astype(BF16)
        bim = jnp.einsum("gcp,gh->gchp", s5bbi[l], eye_g).reshape(A_CH, A_NG * A_P).astype(BF16)
        cre = jnp.einsum("gcp,gh->gphc", s5_c_re[l], eye_g).reshape(A_NG * A_P, A_CH).astype(BF16)
        cim = jnp.einsum("gcp,gh->gphc", s5_c_im[l], eye_g).reshape(A_NG * A_P, A_CH).astype(BF16)
        ya, s5r, s5i = _s5_sample(u, state_s5_re[l].reshape(nb, -1), state_s5_im[l].reshape(nb, -1),
                                  s5abr[l].reshape(1, -1), s5abi[l].reshape(1, -1), bre, bim, cre, cim, d_row)
        cprev = state_conv[l]
        yb, ssm_s = _ssd_sample(xbc, cprev[:, 0], cprev[:, 1], cprev[:, 2], z, dtf,
                                state_ssm[l].reshape(nb, B_CH, B_N), cw, cb, dtb_c, al_c, expand, dsk, ng)
        yc, lfn = _fox_sample(page_table, l, q, k, v, dtf, bf_row, sel, t2, ck, cv, clf)
        xs = _out_proj(xs, ya, yb, yc.reshape(nb, C_CH).astype(BF16), w_g, b_g, w_o, g_qm, mod_s, nb, l, 1, nb)
        xs = _ffn(xs, g_pf, w1, w2, g_qf, mod_s, nb, l, 1, nb)
        s_out[0].append(k.reshape(nb, 1, C_HEADS, C_HD))
        s_out[1].append(v.reshape(nb, 1, C_HEADS, C_HD))
        s_out[2].append(lfn[:, :, F_OFF:F_OFF + C_HEADS])
        s_out[3].append(s5r.reshape(nb, A_NG, A_P))
        s_out[4].append(s5i.reshape(nb, A_NG, A_P))
        s_out[5].append(jnp.concatenate([cprev[:, 1:], xbc[:, None, :]], axis=1))
        s_out[6].append(ssm_s.reshape(nb, B_HEADS, B_HD, B_N))

    pk, pv, plf, ps5r, ps5i, pconv, pssm = [jnp.stack(a) for a in p_out]
    sk, sv, slf, ss5r, ss5i, sconv, sssm = [jnp.stack(a) for a in s_out]
    return (xp.reshape(n, length, D_MODEL), xs.reshape(nb, 1, D_MODEL), pk, pv, plf, ps5r, ps5i, pconv, pssm,
            sk, sv, slf, ss5r, ss5i, sconv, sssm)
```

```python
import functools
import math

import jax
import jax.numpy as jnp
from jax import lax
from jax.experimental import pallas as pl
from jax.experimental.pallas import tpu as pltpu

F32 = jnp.float32
BF16 = jnp.bfloat16
HI = lax.Precision.HIGHEST

D_MODEL = 1024
A_CH = 256
A_GROUP = 16
A_NG = 16
A_P = 64
S5_T = 16
S5_ROW = S5_T * A_GROUP
S5_LEVELS = 8
B_HD = 64
B_CH = 384
B_HEADS = 6
B_NG = 2
B_N = 128
B_CONV = 4
B_XBC = 896
B_CHUNK = 128
C_HD = 64
C_CH = 384
C_HEADS = 6
PAGE = 128
D_FF = 4096
EPS = 1e-6
PAD_W = 128
F_OFF = 6
N_ALL = A_CH + B_CH + B_XBC + 3 * C_CH + PAD_W
VMEM_LIMIT = 56 * 1024 * 1024


def _cp(*sem):
    return pltpu.CompilerParams(dimension_semantics=sem, vmem_limit_bytes=VMEM_LIMIT)


def _sds(shape, dtype):
    return jax.ShapeDtypeStruct(shape, dtype)


def _dot(a, b, precision=None):
    return jnp.dot(a, b, preferred_element_type=F32, precision=precision)


def _dot_nt(a, b, precision=None):
    return lax.dot_general(a, b, (((1,), (1,)), ((), ())), preferred_element_type=F32, precision=precision)


def _dot_tn(a, b, precision=None):
    return lax.dot_general(a, b, (((0,), (0,)), ((), ())), preferred_element_type=F32, precision=precision)


def _silu(x):
    return x * jax.nn.sigmoid(x)


def _softplus(x):
    return jnp.maximum(x, 0.0) + jnp.log1p(jnp.exp(-jnp.abs(x)))


def _log_sigmoid(x):
    return jnp.minimum(x, 0.0) - jnp.log1p(jnp.exp(-jnp.abs(x)))


def _gelu_tanh(x):
    return 0.5 * x * (1.0 + jnp.tanh(math.sqrt(2.0 / math.pi) * (x + 0.044715 * (x * x * x))))


def _rms(x, g):
    return x * lax.rsqrt(jnp.mean(x * x, axis=-1, keepdims=True) + EPS) * g


def _iota(shape, dim):
    return lax.broadcasted_iota(jnp.int32, shape, dim)


def _ada_body(c_ref, w_ref, b_ref, o_ref):
    s = _silu(c_ref[...]).astype(BF16)
    o_ref[...] = _dot(s, w_ref[...].astype(BF16)) + b_ref[...]


def _ada_mod(cond, w_ada, b_ada):
    depth = w_ada.shape[0]
    rows = cond.shape[0]
    tn = 1536
    return pl.pallas_call(
        _ada_body,
        out_shape=_sds((depth, rows, 6 * D_MODEL), F32),
        grid=(depth, 6 * D_MODEL // tn),
        in_specs=[pl.BlockSpec((rows, D_MODEL), lambda l, j: (0, 0)),
                  pl.BlockSpec((None, D_MODEL, tn), lambda l, j: (l, 0, j)),
                  pl.BlockSpec((None, 1, tn), lambda l, j: (l, 0, j))],
        out_specs=pl.BlockSpec((None, rows, tn), lambda l, j: (l, 0, j)),
        compiler_params=_cp("arbitrary", "arbitrary"),
        name="ada_mod",
    )(cond, w_ada, b_ada.reshape(depth, 1, 6 * D_MODEL))


def _mod_spec(rows, base, tiles_per_group, piece):
    return pl.BlockSpec((None, rows, D_MODEL), lambda i: (base + i // tiles_per_group, 0, piece))


def _inproj_body(*refs, n_in, rows_out):
    x_ref, g_ref, sh_ref, sc_ref, w_ref = refs[:5]
    u_ref, z_ref, xbc_ref, q_ref, ktb_ref, vb_ref, kt_ref, vt_ref, dtf_ref, dtft_ref = refs[n_in:n_in + 10]
    h = _rms(x_ref[...], g_ref[...]) * (1.0 + sc_ref[...]) + sh_ref[...]
    hb = h.astype(BF16)

    def mm(a, b):
        return _dot(hb, w_ref[:, a:b])

    o = 0
    u_ref[...] = mm(o, o + A_CH)
    o += A_CH
    z_ref[...] = mm(o, o + B_CH)
    o += B_CH
    xbc_ref[...] = mm(o, o + B_XBC)
    o += B_XBC
    q_ref[...] = (mm(o, o + C_CH) * (C_HD ** -0.5)).astype(BF16)
    o += C_CH
    k = mm(o, o + C_CH)
    kt = k.T
    kt_ref[...] = kt
    ktb_ref[...] = kt.astype(BF16).reshape(ktb_ref.shape)
    o += C_CH
    v = mm(o, o + C_CH)
    vt_ref[...] = v.T
    vb_ref[...] = v.astype(BF16)
    o += C_CH
    dtf = mm(o, o + PAD_W)
    dtf_ref[...] = dtf
    dtft_ref[...] = dtf.T[:16, :]
    if rows_out:
        k_ref, v_ref = refs[n_in + 10:n_in + 12]
        k_ref[...] = k
        v_ref[...] = v


def _in_proj(x, mod, mod_rows, mod_base, tiles_per_group, g_pre, w_all, tm, layer, depth, kv_all, rows_out):
    t = x.shape[0]
    groups = t // (tm * tiles_per_group)
    glen = tm * tiles_per_group
    row = lambda w: pl.BlockSpec((tm, w), lambda i: (i, 0))
    kv_spec = pl.BlockSpec((None, None, C_CH, tm),
                           lambda i: (layer, i // tiles_per_group, 0, i % tiles_per_group))
    outs = [_sds((t, A_CH), F32), _sds((t, B_CH), F32), _sds((t, B_XBC), F32), _sds((t, C_CH), BF16),
            _sds((groups, C_HEADS // 2, tiles_per_group, 128, tm), BF16), _sds((t, C_CH), BF16),
            _sds((depth, groups, C_CH, glen), F32), _sds((depth, groups, C_CH, glen), F32),
            _sds((t, PAD_W), F32), _sds((16, t), F32)]
    out_specs = [row(A_CH), row(B_CH), row(B_XBC), row(C_CH),
                 pl.BlockSpec((None, C_HEADS // 2, None, 128, tm),
                              lambda i: (i // tiles_per_group, 0, i % tiles_per_group, 0, 0)),
                 row(C_CH), kv_spec, kv_spec, row(PAD_W), pl.BlockSpec((16, tm), lambda i: (0, i))]
    if rows_out:
        outs += [_sds((t, C_CH), F32), _sds((t, C_CH), F32)]
        out_specs += [row(C_CH), row(C_CH)]
    in_specs = [row(D_MODEL),
                pl.BlockSpec((1, D_MODEL), lambda i: (0, 0)),
                _mod_spec(mod_rows, mod_base, tiles_per_group, 0),
                _mod_spec(mod_rows, mod_base, tiles_per_group, 1),
                pl.BlockSpec((D_MODEL, N_ALL), lambda i: (0, 0))]
    args = [x, g_pre, mod, mod, w_all]
    aliases = {}
    if kv_all is not None:
        in_specs += [pl.BlockSpec(memory_space=pl.ANY), pl.BlockSpec(memory_space=pl.ANY)]
        args += list(kv_all)
        aliases = {5: 6, 6: 7}
    return pl.pallas_call(
        functools.partial(_inproj_body, n_in=len(args), rows_out=rows_out),
        out_shape=outs,
        grid=(t // tm,),
        in_specs=in_specs,
        out_specs=out_specs,
        input_output_aliases=aliases,
        compiler_params=_cp("arbitrary"),
        name="in_proj",
    )(*args)


def _s5_param_body(pw_ref, ldt_ref, arr_ref, air_ref, arc_ref, aic_ref, btr_ref, bti_ref, ctr_ref, cti_ref,
                   m_ref, sre_ref, sim_ref, w_ref, err_ref, eii_ref, abr_ref, abi_ref, bbr_ref, bbi_ref):
    dt = jnp.exp(ldt_ref[...])
    ar, ai = arr_ref[...], air_ref[...]
    mag = jnp.exp(ar * dt)
    abr, abi = mag * jnp.cos(ai * dt), mag * jnp.sin(ai * dt)
    den = ar * ar + ai * ai
    nr, ni = abr - 1.0, abi
    fr, fi = (nr * ar + ni * ai) / den, (ni * ar - nr * ai) / den
    btr, bti = btr_ref[...], bti_ref[...]
    bbr, bbi = fr * btr - fi * bti, fr * bti + fi * btr
    abr_ref[...] = abr
    abi_ref[...] = abi
    bbr_ref[...] = bbr[0:A_GROUP]
    bbi_ref[...] = bbi[0:A_GROUP]

    def epow(k, a_r, a_i):
        m = jnp.exp(k * a_r * dt)
        th = k * a_i * dt
        return m * jnp.cos(th), m * jnp.sin(th)

    s = jnp.right_shift(_iota((S5_ROW, 1), 0), 4).astype(F32)
    enr, eni = epow(-s, ar, ai)
    a_re, a_im = bbr * enr - bbi * eni, bbr * eni + bbi * enr
    elr, eli = epow((S5_T - 1.0) - s, ar, ai)
    sre_ref[...] = (bbr * elr - bbi * eli).astype(BF16)
    sim_ref[...] = (bbr * eli + bbi * elr).astype(BF16)

    arc, aic = arc_ref[...], aic_ref[...]
    t = jnp.right_shift(_iota((1, S5_ROW), 1), 4).astype(F32)
    etr, eti = epow(t, arc, aic)
    ctr, cti = ctr_ref[...], cti_ref[...]
    bm_re, bm_im = ctr * etr - cti * eti, ctr * eti + cti * etr
    m = _dot(a_re, bm_re, HI) - _dot(a_im, bm_im, HI)
    m_ref[...] = jnp.where(s <= t, m, 0.0).astype(BF16)
    e1r, e1i = epow(t + 1.0, arc, aic)
    w_ref[0:A_P, :] = (ctr * e1r - cti * e1i).astype(BF16)
    w_ref[A_P:2 * A_P, :] = (-(ctr * e1i + cti * e1r)).astype(BF16)

    lr, li = epow(pw_ref[...], ar, ai)
    err_ref[...] = jnp.concatenate([lr, lr], axis=1)
    eii_ref[...] = jnp.concatenate([-li, li], axis=1)


def _s5_params(a_re, a_im, log_dt, b_re, b_im, c_re, c_im):
    depth = a_re.shape[0]
    pw = (S5_T * 2.0 ** jnp.arange(S5_LEVELS, dtype=F32)).reshape(S5_LEVELS, 1)
    arr, air = a_re.reshape(depth, A_NG, 1, A_P), a_im.reshape(depth, A_NG, 1, A_P)
    arc, aic = a_re.reshape(depth, A_NG, A_P, 1), a_im.reshape(depth, A_NG, A_P, 1)
    ldt = log_dt.reshape(depth, A_NG, 1, 1)
    bt = lambda b: jnp.tile(jnp.swapaxes(b, 2, 3), (1, 1, S5_T, 1))
    ct = lambda c: jnp.tile(jnp.swapaxes(c, 2, 3), (1, 1, 1, S5_T))
    blk = lambda r, c: pl.BlockSpec((None, None, r, c), lambda l, g: (l, g, 0, 0))
    outs = [_sds((depth, A_NG, S5_ROW, S5_ROW), BF16),
            _sds((depth, A_NG, S5_ROW, A_P), BF16), _sds((depth, A_NG, S5_ROW, A_P), BF16),
            _sds((depth, A_NG, 2 * A_P, S5_ROW), BF16),
            _sds((depth, A_NG, S5_LEVELS, 2 * A_P), F32), _sds((depth, A_NG, S5_LEVELS, 2 * A_P), F32),
            _sds((depth, A_NG, 1, A_P), F32), _sds((depth, A_NG, 1, A_P), F32),
            _sds((depth, A_NG, A_GROUP, A_P), F32), _sds((depth, A_NG, A_GROUP, A_P), F32)]
    out_specs = [blk(S5_ROW, S5_ROW), blk(S5_ROW, A_P), blk(S5_ROW, A_P), blk(2 * A_P, S5_ROW),
                 blk(S5_LEVELS, 2 * A_P), blk(S5_LEVELS, 2 * A_P), blk(1, A_P), blk(1, A_P),
                 blk(A_GROUP, A_P), blk(A_GROUP, A_P)]
    return pl.pallas_call(
        _s5_param_body,
        out_shape=outs,
        grid=(depth, A_NG),
        in_specs=[pl.BlockSpec((S5_LEVELS, 1), lambda l, g: (0, 0)),
                  blk(1, 1), blk(1, A_P), blk(1, A_P), blk(A_P, 1), blk(A_P, 1),
                  blk(S5_ROW, A_P), blk(S5_ROW, A_P), blk(A_P, S5_ROW), blk(A_P, S5_ROW)],
        out_specs=out_specs,
        compiler_params=_cp("arbitrary", "arbitrary"),
        name="s5_params",
    )(pw, ldt, arr, air, arc, aic, bt(b_re), bt(b_im), ct(c_re), ct(c_im))


def _s5_prompt_body(x_ref, m_ref, s_ref, w_ref, err_ref, eii_ref, d_ref, y_ref, hf_ref, *, n, chunks, levels):
    x = x_ref[...]
    xb = x.astype(BF16)
    yi = _dot(xb, m_ref[...])
    b = _dot(xb, s_ref[...])
    cidx = jnp.bitwise_and(_iota((n * chunks, 1), 0), chunks - 1)
    for k in range(levels):
        d = 1 << k
        sh = jnp.where(cidx >= d, pltpu.roll(b, d, axis=0), 0.0)
        sw = pltpu.roll(sh, A_P, axis=1)
        b = b + err_ref[k:k + 1, :] * sh + eii_ref[k:k + 1, :] * sw
    hp = jnp.where(cidx >= 1, pltpu.roll(b, 1, axis=0), 0.0)
    y_ref[...] = yi + _dot(hp.astype(BF16), w_ref[...]) + d_ref[...] * x
    for i in range(n):
        hf_ref[i:i + 1, :] = b[(i + 1) * chunks - 1:(i + 1) * chunks, :]


def _s5_prompt(xt, m, s, w, err, eii, d_t, n, chunks):
    rows = n * chunks
    levels = max(1, (chunks - 1).bit_length())
    assert chunks & (chunks - 1) == 0 and levels <= S5_LEVELS
    blk = lambda r, c: pl.BlockSpec((None, r, c), lambda g: (g, 0, 0))
    return pl.pallas_call(
        functools.partial(_s5_prompt_body, n=n, chunks=chunks, levels=levels),
        out_shape=[_sds((A_NG, rows, S5_ROW), F32), _sds((A_NG, n, 2 * A_P), F32)],
        grid=(A_NG,),
        in_specs=[blk(rows, S5_ROW), blk(S5_ROW, S5_ROW), blk(S5_ROW, 2 * A_P), blk(2 * A_P, S5_ROW),
                  blk(S5_LEVELS, 2 * A_P), blk(S5_LEVELS, 2 * A_P), blk(1, S5_ROW)],
        out_specs=[blk(rows, S5_ROW), blk(n, 2 * A_P)],
        compiler_params=_cp("arbitrary"),
        name="s5_prompt",
    )(xt, m, s, w, err, eii, d_t)


def _s5_sample_body(u_ref, hr_ref, hi_ref, abr_ref, abi_ref, bre_ref, bim_ref, cre_ref, cim_ref, d_ref,
                    y_ref, sr_ref, si_ref):
    u = u_ref[...]
    ub = u.astype(BF16)
    hr, hi = hr_ref[...], hi_ref[...]
    abr, abi = abr_ref[...], abi_ref[...]
    sr = abr * hr - abi * hi + _dot(ub, bre_ref[...])
    si = abr * hi + abi * hr + _dot(ub, bim_ref[...])
    sr_ref[...] = sr
    si_ref[...] = si
    y_ref[...] = _dot(sr.astype(BF16), cre_ref[...]) - _dot(si.astype(BF16), cim_ref[...]) + d_ref[...] * u


def _s5_sample(u, hr, hi, abr, abi, bre, bim, cre, cim, d):
    b = u.shape[0]
    w = A_NG * A_P
    return pl.pallas_call(
        _s5_sample_body,
        out_shape=[_sds((b, A_CH), F32), _sds((b, w), F32), _sds((b, w), F32)],
        compiler_params=pltpu.CompilerParams(vmem_limit_bytes=VMEM_LIMIT),
        name="s5_sample",
    )(u, hr, hi, abr, abi, bre, bim, cre, cim, d)


def _ssd_prompt_body(xbc_ref, z_ref, dtc_ref, dtr_ref, cprev_ref, h0_ref, cw_ref, cb_ref,
                     dtbc_ref, dtbr_ref, alc_ref, alr_ref, dsk_ref, ng_ref,
                     y_ref, hf_ref, ext_scr, h_scr):
    c = pl.program_id(1)
    q = B_CHUNK

    @pl.when(c == 0)
    def _():
        ext_scr[0:8, :] = cprev_ref[...]
        h_scr[...] = h0_ref[...]

    x = xbc_ref[...]
    ext_scr[8:8 + q, :] = x
    cw = cw_ref[...]
    conv = (cb_ref[...] + cw[3:4] * x + cw[2:3] * ext_scr[7:7 + q, :]
            + cw[1:2] * ext_scr[6:6 + q, :] + cw[0:1] * ext_scr[5:5 + q, :])
    ext_scr[0:8, :] = x[q - 8:q, :]
    xc = _silu(conv)

    r_i, c_i = _iota((q, q), 0), _iota((q, q), 1)
    causal = r_i >= c_i
    dtc = _softplus(dtc_ref[...] + dtbc_ref[...])
    da_c = dtc * (-jnp.exp(alc_ref[...]))
    acc_c = _dot(causal.astype(F32), da_c, HI)
    dtr = _softplus(dtr_ref[...] + dtbr_ref[...])
    da_r = dtr * (-jnp.exp(alr_ref[...]))
    acc_r = _dot(da_r, (r_i <= c_i).astype(F32), HI)

    bm = [xc[:, B_CH + B_N * g:B_CH + B_N * (g + 1)].astype(BF16) for g in range(B_NG)]
    cm = [xc[:, B_CH + B_N * (B_NG + g):B_CH + B_N * (B_NG + g + 1)].astype(BF16) for g in range(B_NG)]
    cbm = [_dot_nt(cm[g], bm[g]) for g in range(B_NG)]
    lane = _iota((q, 128), 1)
    heads_per_group = B_HEADS // B_NG
    ys = []
    for j in range(B_HEADS // 2):
        xs_pair = xc[:, 128 * j:128 * (j + 1)]
        dt_pair = jnp.where(lane < B_HD, dtc[:, 2 * j:2 * j + 1], dtc[:, 2 * j + 1:2 * j + 2])
        xdt = xs_pair * dt_pair
        xdt_b = xdt.astype(BF16)
        hprev_b = h_scr[2 * j:2 * j + 2].reshape(2 * B_HD, B_N).astype(BF16)
        outs = []
        for e in range(2):
            h = 2 * j + e
            g = h // heads_per_group
            ac = acc_c[:, h:h + 1]
            ar = acc_r[h:h + 1, :]
            lm = jnp.exp(jnp.where(causal, ac - ar, -jnp.inf))
            yd = _dot((cbm[g] * lm).astype(BF16), xdt_b)
            yo = _dot_nt(cm[g], hprev_b) * jnp.exp(ac)
            outs.append(yd + yo)
            alast = acc_r[h:h + 1, q - 1:q]
            st = _dot_tn((xdt * jnp.exp(alast - ac)).astype(BF16), bm[g])
            h_scr[h] = jnp.exp(alast) * h_scr[h] + st[B_HD * e:B_HD * (e + 1), :]
        ys.append(jnp.where(lane < B_HD, outs[0], outs[1]) + dsk_ref[:, 128 * j:128 * (j + 1)] * xs_pair)
    y = jnp.concatenate(ys, axis=1)
    y_ref[...] = _rms(y * _silu(z_ref[...]), ng_ref[...]).astype(y_ref.dtype)

    @pl.when(c == pl.num_programs(1) - 1)
    def _():
        hf_ref[...] = h_scr[...]


def _ssd_prompt(xbc, z, dtf, dtft, cprev, h0, cw, cb, dtb_c, dtb_r, al_c, al_r, dsk, ng, n, length):
    nc = length // B_CHUNK
    q = B_CHUNK
    tok = lambda w: pl.BlockSpec((q, w), lambda i, c: (i * nc + c, 0))
    const = lambda r, w: pl.BlockSpec((r, w), lambda i, c: (0, 0))
    return pl.pallas_call(
        _ssd_prompt_body,
        out_shape=[_sds((n * length, B_CH), BF16), _sds((n, B_HEADS, B_HD, B_N), F32)],
        grid=(n, nc),
        in_specs=[tok(B_XBC), tok(B_CH), tok(PAD_W),
                  pl.BlockSpec((16, q), lambda i, c: (0, i * nc + c)),
                  pl.BlockSpec((None, 8, B_XBC), lambda i, c: (i, 0, 0)),
                  pl.BlockSpec((None, B_HEADS, B_HD, B_N), lambda i, c: (i, 0, 0, 0)),
                  const(B_CONV, B_XBC), const(1, B_XBC), const(1, PAD_W), const(16, 1),
                  const(1, PAD_W), const(16, 1), const(1, B_CH), const(1, B_CH)],
        out_specs=[tok(B_CH), pl.BlockSpec((None, B_HEADS, B_HD, B_N), lambda i, c: (i, 0, 0, 0))],
        scratch_shapes=[pltpu.VMEM((8 + q, B_XBC), F32), pltpu.VMEM((B_HEADS, B_HD, B_N), F32)],
        compiler_params=_cp("arbitrary", "arbitrary"),
        name="ssd_prompt",
    )(xbc, z, dtf, dtft, cprev, h0, cw, cb, dtb_c, dtb_r, al_c, al_r, dsk, ng)


def _ssd_sample_body(xbc_ref, p0_ref, p1_ref, p2_ref, z_ref, dtf_ref, h0_ref, cw_ref, cb_ref,
                     dtb_ref, al_ref, ex_ref, dsk_ref, ng_ref, *rest, tb):
    y_ref, hn_ref, y_scr = rest[-3:]
    x = xbc_ref[...]
    cw = cw_ref[...]
    conv = cb_ref[...] + cw[0:1] * p0_ref[...] + cw[1:2] * p1_ref[...] + cw[2:3] * p2_ref[...] + cw[3:4] * x
    xc = _silu(conv)
    dt = _softplus(dtf_ref[...] + dtb_ref[...])
    e = jnp.exp(dt * (-jnp.exp(al_ref[...])))
    ex = ex_ref[...]
    dt_e = _dot(dt, ex, HI)
    e_e = _dot(e, ex, HI)
    xs = xc[:, :B_CH]
    xdt = xs * dt_e
    half = B_CH // B_NG
    row8, lane8 = _iota((8, B_CH), 0), _iota((8, B_CH), 1)
    r8 = _iota((8, B_N), 0)
    lane1 = _iota((1, B_CH), 1)
    for b in range(tb):
        xr = xdt[b:b + 1, :]
        lhs = jnp.where(row8 == 0, e_e[b:b + 1, :],
                        jnp.where(((row8 == 1) & (lane8 < half)) | ((row8 == 2) & (lane8 >= half)), xr, 0.0))
        rhs_e = jnp.where(r8 == 0, 1.0, 0.0)
        rhs_s = jnp.where(r8 == 1, xc[b:b + 1, B_CH:B_CH + B_N],
                          jnp.where(r8 == 2, xc[b:b + 1, B_CH + B_N:B_CH + 2 * B_N], 0.0))
        hn = _dot_tn(lhs, rhs_e, HI) * h0_ref[b] + _dot_tn(lhs, rhs_s, HI)
        hn_ref[b] = hn
        cmat = jnp.where(r8 == 0, xc[b:b + 1, B_CH + 2 * B_N:B_CH + 3 * B_N],
                         jnp.where(r8 == 1, xc[b:b + 1, B_CH + 3 * B_N:B_CH + 4 * B_N], 0.0))
        yt = _dot_nt(cmat.astype(BF16), hn.astype(BF16))
        y_scr[b:b + 1, :] = jnp.where(lane1 < half, yt[0:1, :], yt[1:2, :])
    y = y_scr[...] + dsk_ref[...] * xs
    y_ref[...] = _rms(y * _silu(z_ref[...]), ng_ref[...]).astype(y_ref.dtype)


def _ssd_sample(xbc, conv_t, z, dtf, ssm_in, ssm_out, layer, cw, cb, dtb, al, ex, dsk, ng):
    depth, b = ssm_in.shape[0], xbc.shape[0]
    tb = 8
    row = lambda w: pl.BlockSpec((tb, w), lambda i: (i, 0))
    const = lambda r, w: pl.BlockSpec((r, w), lambda i: (0, 0))
    prev = lambda k: pl.BlockSpec((None, None, tb, B_XBC), lambda i: (layer, k, i, 0))
    st = pl.BlockSpec((None, tb, B_CH, B_N), lambda i: (layer, i, 0, 0))
    in_specs = [row(B_XBC), prev(0), prev(1), prev(2), row(B_CH), row(PAD_W), st,
                const(B_CONV, B_XBC), const(1, B_XBC), const(1, PAD_W), const(1, PAD_W),
                const(PAD_W, B_CH), const(1, B_CH), const(1, B_CH)]
    args = [xbc, conv_t, conv_t, conv_t, z, dtf, ssm_in, cw, cb, dtb, al, ex, dsk, ng]
    aliases = {}
    if ssm_out is not None:
        in_specs.append(pl.BlockSpec(memory_space=pl.ANY))
        args.append(ssm_out)
        aliases = {len(args) - 1: 1}
    return pl.pallas_call(
        functools.partial(_ssd_sample_body, tb=tb),
        out_shape=[_sds((b, B_CH), BF16), _sds((depth, b, B_CH, B_N), F32)],
        grid=(b // tb,),
        in_specs=in_specs,
        out_specs=[row(B_CH), st],
        scratch_shapes=[pltpu.VMEM((tb, B_CH), F32)],
        input_output_aliases=aliases,
        compiler_params=_cp("arbitrary"),
        name="ssd_sample",
    )(*args)


def _fox_prep_body(fc_ref, fr_ref, bfr_ref, bfc_ref, lfr_ref, cc_ref, cr_ref, carry_c, carry_r):
    j = pl.program_id(1)
    tl = fc_ref.shape[0]

    @pl.when(j == 0)
    def _():
        carry_c[...] = jnp.zeros_like(carry_c)
        carry_r[...] = jnp.zeros_like(carry_r)

    r_i, c_i = _iota((tl, tl), 0), _iota((tl, tl), 1)
    lfc = _log_sigmoid(fc_ref[...] + bfr_ref[...])
    cc = _dot((r_i >= c_i).astype(F32), lfc, HI) + carry_c[0:1, :]
    cc_ref[...] = cc
    carry_c[...] = jnp.broadcast_to(cc[tl - 1:tl, :], carry_c.shape)
    lfr = _log_sigmoid(fr_ref[...] + bfc_ref[...])
    lfr_ref[...] = lfr
    cr = _dot(lfr, (r_i <= c_i).astype(F32), HI) + carry_r[:, 0:1]
    cr_ref[...] = cr
    carry_r[...] = jnp.broadcast_to(cr[:, tl - 1:tl], carry_r.shape)


def _fox_prep(dtf, dtft, bf_row, bf_col, n, length, tl):
    nt = length // tl
    t = n * length
    tok = pl.BlockSpec((tl, PAD_W), lambda i, j: (i * nt + j, 0))
    rowt = pl.BlockSpec((None, 16, tl), lambda i, j: (i * nt + j, 0, 0))
    return pl.pallas_call(
        _fox_prep_body,
        out_shape=[_sds((n * nt, 16, tl), F32), _sds((t, PAD_W), F32), _sds((n * nt, 16, tl), F32)],
        grid=(n, nt),
        in_specs=[tok, pl.BlockSpec((16, tl), lambda i, j: (0, i * nt + j)),
                  pl.BlockSpec((1, PAD_W), lambda i, j: (0, 0)), pl.BlockSpec((16, 1), lambda i, j: (0, 0))],
        out_specs=[rowt, tok, rowt],
        scratch_shapes=[pltpu.VMEM((8, PAD_W), F32), pltpu.VMEM((16, 128), F32)],
        compiler_params=_cp("arbitrary", "arbitrary"),
        name="fox_prep",
    )(dtf, dtft, bf_row, bf_col)


def _fox_flash_body(q_ref, k_ref, v_ref, cc_ref, cr_ref, o_ref, q_scr, cq_scr, m_scr, l_scr, acc_scr):
    hp, qi = pl.program_id(1), pl.program_id(2)
    tq, tk = q_ref.shape[0], k_ref.shape[2]
    lane = _iota((tq, 128), 1)
    q = q_ref[...]
    cc = cc_ref[...]
    for e in range(2):
        q_scr[e] = jnp.where((lane < C_HD) == (e == 0), q, jnp.zeros_like(q))
        cq_scr[e] = jnp.sum(jnp.where(lane == F_OFF + 2 * hp + e, cc, 0.0), axis=1, keepdims=True)
    m_scr[...] = jnp.full(m_scr.shape, -jnp.inf, F32)
    l_scr[...] = jnp.zeros_like(l_scr)
    acc_scr[...] = jnp.zeros_like(acc_scr)

    def block(ki, diagonal):
        kt = k_ref[ki]
        v = v_ref[pl.ds(pl.multiple_of(ki * tk, tk), tk), :]
        for e in range(2):
            ck = cr_ref[ki, pl.ds(F_OFF + 2 * hp + e, 1), :]
            s = _dot(q_scr[e], kt) + (cq_scr[e] - ck)
            if diagonal:
                s = jnp.where(_iota((tq, tk), 0) >= _iota((tq, tk), 1), s, -jnp.inf)
            m_old = m_scr[e]
            m_new = jnp.maximum(m_old, jnp.max(s, axis=1, keepdims=True))
            alpha = jnp.exp(m_old - m_new)
            p = jnp.exp(s - m_new)
            l_scr[e] = alpha * l_scr[e] + jnp.sum(p, axis=1, keepdims=True)
            acc_scr[e] = alpha * acc_scr[e] + _dot(p.astype(BF16), v)
            m_scr[e] = m_new

    def below_diagonal(ki, carry):
        block(ki, False)
        return carry

    lax.fori_loop(0, qi, below_diagonal, 0)
    block(qi, True)
    o = jnp.where(lane < C_HD, acc_scr[0] / l_scr[0], acc_scr[1] / l_scr[1])
    o_ref[...] = o.astype(o_ref.dtype)


def _fox_flash(q, kt, v, cc, cr, n, length, tq):
    nq = length // tq
    qmap = lambda i, h, a: (i * nq + a, h)
    return pl.pallas_call(
        _fox_flash_body,
        out_shape=_sds((n * length, C_CH), BF16),
        grid=(n, C_HEADS // 2, nq),
        in_specs=[pl.BlockSpec((tq, 128), qmap),
                  pl.BlockSpec((None, None, nq, 128, tq), lambda i, h, a: (i, h, 0, 0, 0)),
                  pl.BlockSpec((length, 128), lambda i, h, a: (i, h)),
                  pl.BlockSpec((tq, PAD_W), lambda i, h, a: (i * nq + a, 0)),
                  pl.BlockSpec((nq, 16, tq), lambda i, h, a: (i, 0, 0))],
        out_specs=pl.BlockSpec((tq, 128), qmap),
        scratch_shapes=[pltpu.VMEM((2, tq, 128), BF16), pltpu.VMEM((2, tq, 1), F32),
                        pltpu.VMEM((2, tq, 1), F32), pltpu.VMEM((2, tq, 1), F32),
                        pltpu.VMEM((2, tq, 128), F32)],
        compiler_params=_cp("arbitrary", "arbitrary", "arbitrary"),
        name="fox_flash",
    )(q, kt, v, cc, cr)


def _fox_sample_body(pt_ref, q_ref, kn_ref, vn_ref, fn_ref, bf_ref, *refs, pages):
    k_refs = refs[0:pages]
    v_refs = refs[pages:2 * pages]
    f_refs = refs[2 * pages:3 * pages]
    o_ref, lf_ref, lf_scr = refs[3 * pages:3 * pages + 3]
    i = pl.program_id(0)
    row, lane = _iota((8, C_CH), 0), _iota((8, C_CH), 1)
    own = jnp.right_shift(lane, 6) == row
    qm = jnp.where(own, q_ref[...].astype(F32), 0.0)
    qb = qm.astype(BF16)
    lfn = _log_sigmoid(fn_ref[...] + bf_ref[...])
    lf_ref[...] = lfn
    carry = jnp.sum(jnp.where(_iota((8, PAD_W), 1) == F_OFF + _iota((8, PAD_W), 0), lfn, 0.0),
                    axis=1, keepdims=True)
    for p in range(pages):
        r = jnp.bitwise_and(pt_ref[i, p], 7)
        for h in range(C_HEADS):
            lf_scr[8 * p + h:8 * p + h + 1, :] = f_refs[p][h, pl.ds(r, 1), :]
        lf_scr[8 * p + C_HEADS:8 * p + 8, :] = jnp.zeros((8 - C_HEADS, PAGE), F32)
    lf_all = lf_scr[...]
    later = (_iota((PAGE, PAGE), 0) > _iota((PAGE, PAGE), 1)).astype(F32)
    suffix = _dot(lf_all, later, HI)
    total = jnp.sum(lf_all, axis=1, keepdims=True)
    s = [None] * pages
    for p in reversed(range(pages)):
        kt = k_refs[p][...].reshape(C_CH, PAGE).astype(BF16)
        s[p] = _dot(qb, kt) + suffix[8 * p:8 * p + 8, :] + carry
        carry = carry + total[8 * p:8 * p + 8, :]
    s_new = jnp.sum(qm * kn_ref[...], axis=1, keepdims=True)
    m = s_new
    for p in range(pages):
        m = jnp.maximum(m, jnp.max(s[p], axis=1, keepdims=True))
    p_new = jnp.exp(s_new - m)
    l = p_new
    acc = p_new * vn_ref[...]
    for p in range(pages):
        pr = jnp.exp(s[p] - m)
        l = l + jnp.sum(pr, axis=1, keepdims=True)
        acc = acc + _dot_nt(pr.astype(BF16), v_refs[p][...].reshape(C_CH, PAGE).astype(BF16))
    o_ref[...] = jnp.sum(jnp.where(own, acc / l, 0.0), axis=0, keepdims=True)


def _fox_sample(page_table, layer, q, k_new, v_new, f_new, bf_row, cache_kt, cache_vt, cache_lft):
    b = q.shape[0]
    pages = page_table.shape[1]
    tok = lambda w: pl.BlockSpec((None, 1, w), lambda i, pt: (i, 0, 0))
    page = lambda p: pl.BlockSpec((None, None, C_HEADS, C_HD, PAGE), lambda i, pt: (layer, pt[i, p], 0, 0, 0))
    lfpage = lambda p: pl.BlockSpec((None, C_HEADS, 8, PAGE), lambda i, pt: (layer, 0, pt[i, p] // 8, 0))
    in_specs = ([tok(C_CH), tok(C_CH), tok(C_CH), tok(PAD_W), pl.BlockSpec((1, PAD_W), lambda i, pt: (0, 0))]
                + [page(p) for p in range(pages)] + [page(p) for p in range(pages)]
                + [lfpage(p) for p in range(pages)])
    grid_spec = pltpu.PrefetchScalarGridSpec(
        num_scalar_prefetch=1, grid=(b,), in_specs=in_specs,
        out_specs=[tok(C_CH), tok(PAD_W)],
        scratch_shapes=[pltpu.VMEM((8 * pages, PAGE), F32)])
    return pl.pallas_call(
        functools.partial(_fox_sample_body, pages=pages),
        out_shape=[_sds((b, 1, C_CH), F32), _sds((b, 1, PAD_W), F32)],
        grid_spec=grid_spec,
        compiler_params=_cp("arbitrary"),
        name="fox_sample",
    )(page_table, q.reshape(b, 1, C_CH), k_new.reshape(b, 1, C_CH), v_new.reshape(b, 1, C_CH),
      f_new.reshape(b, 1, PAD_W), bf_row,
      *([cache_kt] * pages), *([cache_vt] * pages), *([cache_lft] * pages))


def _outproj_body(x_ref, ya_ref, yb_ref, yc_ref, wg_ref, bg_ref, wo_ref, gp_ref, g1_ref, o_ref):
    y = _gelu_tanh(ya_ref[...])
    gate = jax.nn.sigmoid(_dot(y.astype(BF16), wg_ref[...]) + bg_ref[...])
    ya = (y * gate).astype(BF16)
    mix = (_dot(ya, wo_ref[0:A_CH, :]) + _dot(yb_ref[...], wo_ref[A_CH:A_CH + B_CH, :])
           + _dot(yc_ref[...], wo_ref[A_CH + B_CH:, :]))
    o_ref[...] = x_ref[...] + g1_ref[...] * _rms(mix, gp_ref[...])


def _out_proj(x, ya, yb, yc, w_glu, b_glu, w_out, g_post, mod, mod_rows, mod_base, tiles_per_group, tm):
    t = x.shape[0]
    row = lambda w: pl.BlockSpec((tm, w), lambda i: (i, 0))
    const = lambda r, w: pl.BlockSpec((r, w), lambda i: (0, 0))
    return pl.pallas_call(
        _outproj_body,
        out_shape=_sds((t, D_MODEL), F32),
        grid=(t // tm,),
        in_specs=[row(D_MODEL), row(A_CH), row(B_CH), row(C_CH), const(A_CH, A_CH), const(1, A_CH),
                  const(D_MODEL, D_MODEL), const(1, D_MODEL), _mod_spec(mod_rows, mod_base, tiles_per_group, 2)],
        out_specs=row(D_MODEL),
        compiler_params=_cp("arbitrary"),
        name="out_proj",
    )(x, ya, yb, yc, w_glu, b_glu, w_out, g_post, mod)


def _ffn_body(x_ref, gpre_ref, sh_ref, sc_ref, w1_ref, w2_ref, gpost_ref, g2_ref, o_ref, *, tf):
    x = x_ref[...]
    hb = (_rms(x, gpre_ref[...]) * (1.0 + sc_ref[...]) + sh_ref[...]).astype(BF16)
    acc = jnp.zeros(x.shape, F32)
    for j in range(D_FF // tf):
        a = jnp.maximum(_dot(hb, w1_ref[:, j * tf:(j + 1) * tf]), 0.0)
        acc = acc + _dot((a * a).astype(BF16), w2_ref[j * tf:(j + 1) * tf, :])
    o_ref[...] = x + g2_ref[...] * _rms(acc, gpost_ref[...])


def _ffn(x, g_pre, w1, w2, g_post, mod, mod_rows, mod_base, tiles_per_group, tm):
    t = x.shape[0]
    row = pl.BlockSpec((tm, D_MODEL), lambda i: (i, 0))
    const = lambda r, w: pl.BlockSpec((r, w), lambda i: (0, 0))
    weight = lambda r, w: pl.BlockSpec((r, w), lambda i: (0, 0), pipeline_mode=pl.Buffered(1))
    return pl.pallas_call(
        functools.partial(_ffn_body, tf=1024),
        out_shape=_sds((t, D_MODEL), F32),
        grid=(t // tm,),
        in_specs=[row, const(1, D_MODEL), _mod_spec(mod_rows, mod_base, tiles_per_group, 3),
                  _mod_spec(mod_rows, mod_base, tiles_per_group, 4), weight(D_MODEL, D_FF), weight(D_FF, D_MODEL),
                  const(1, D_MODEL), _mod_spec(mod_rows, mod_base, tiles_per_group, 5)],
        out_specs=row,
        compiler_params=_cp("arbitrary"),
        name="ffn",
    )(x, g_pre, mod, mod, w1, w2, g_post, mod)


def _pad_lanes(v, offset, width=PAD_W):
    return jnp.zeros((1, width), F32).at[0, offset:offset + v.shape[0]].set(v)


def _pad_rows(v, offset, rows=16):
    return jnp.zeros((rows, 1), F32).at[offset:offset + v.shape[0], 0].set(v)


def kernel(x_prompt, x_sample, c_prompt, c_sample, cache_k, cache_v, cache_logf, page_table, state_s5_re, state_s5_im, state_conv, state_ssm, w_ada, b_ada, g_pre_mix, g_post_mix, g_pre_ffn, g_post_ffn, w_in, w_out, s5_a_re, s5_a_im, s5_log_dt, s5_b_re, s5_b_im, s5_c_re, s5_c_im, s5_d, s5_w_glu, s5_b_glu, ssd_conv_w, ssd_conv_b, ssd_dt_bias, ssd_a_log, ssd_d, ssd_norm_g, fox_b_f, w_ff1, w_ff2):
    depth = w_ada.shape[0]
    n, length, _ = x_prompt.shape
    nb = x_sample.shape[0]
    t = n * length
    n_pool = cache_k.shape[1]
    chunks = length // S5_T
    tm_p = 512
    assert x_sample.shape[1] == 1 and length % tm_p == 0 and nb % 8 == 0

    rows = -(-(n + nb) // 8) * 8
    cond = jnp.concatenate([c_prompt, c_sample, jnp.zeros((rows - n - nb, D_MODEL), F32)], axis=0)
    mod = _ada_mod(cond, w_ada, b_ada)
    mod_p = mod[:, :n].reshape(depth * n, 1, 6 * D_MODEL)
    mod_s = mod[:, n:n + nb]

    s5m, s5sre, s5sim, s5w, s5err, s5eii, s5abr, s5abi, s5bbr, s5bbi = _s5_params(
        s5_a_re, s5_a_im, s5_log_dt, s5_b_re, s5_b_im, s5_c_re, s5_c_im)
    s5s = jnp.concatenate([s5sre, s5sim], axis=-1)
    eye_g = jnp.eye(A_NG, dtype=F32)
    expand = (jnp.arange(PAD_W)[:, None] == jnp.arange(B_CH)[None, :] // B_HD).astype(F32)

    ck = cache_k.transpose(0, 1, 3, 4, 2)
    cv = cache_v.transpose(0, 1, 3, 4, 2)
    clf = cache_logf.transpose(0, 3, 1, 2)
    conv_t = state_conv.transpose(0, 2, 1, 3)
    ssm_in = state_ssm.reshape(depth, nb, B_CH, B_N)

    xp = x_prompt.reshape(t, D_MODEL)
    xs = x_sample.reshape(nb, D_MODEL)
    zeros_conv = jnp.zeros((n, 8, B_XBC), F32)
    zeros_ssm = jnp.zeros((n, B_HEADS, B_HD, B_N), F32)
    p_out = [[] for _ in range(7)]
    s_out = [[] for _ in range(7)]
    kv_all = None
    ssm_all = None

    for l in range(depth):
        wi = w_in[l]
        o_dt = A_CH + B_CH + B_XBC
        o_f = o_dt + B_HEADS + 3 * C_CH
        w_all = jnp.concatenate(
            [wi[:, :o_dt], wi[:, o_dt + B_HEADS:o_f], wi[:, o_dt:o_dt + B_HEADS], wi[:, o_f:],
             jnp.zeros((D_MODEL, PAD_W - B_HEADS - C_HEADS), F32)], axis=1).astype(BF16)
        w_o = w_out[l].astype(BF16)
        w_g = s5_w_glu[l].astype(BF16)
        w1 = w_ff1[l].astype(BF16)
        w2 = w_ff2[l].astype(BF16)
        row = lambda v: v.reshape(1, -1)
        g_pm, g_qm, g_pf, g_qf = row(g_pre_mix[l]), row(g_post_mix[l]), row(g_pre_ffn[l]), row(g_post_ffn[l])
        b_g = row(s5_b_glu[l])
        cw, cb = ssd_conv_w[l], row(ssd_conv_b[l])
        dtb_c, dtb_r = _pad_lanes(ssd_dt_bias[l], 0), _pad_rows(ssd_dt_bias[l], 0)
        al_c, al_r = _pad_lanes(ssd_a_log[l], 0), _pad_rows(ssd_a_log[l], 0)
        dsk = row(jnp.repeat(ssd_d[l], B_HD))
        ng = row(ssd_norm_g[l])
        bf_row, bf_col = _pad_lanes(fox_b_f[l], F_OFF), _pad_rows(fox_b_f[l], F_OFF)
        d_row = row(s5_d[l])
        d_t = jnp.tile(s5_d[l].reshape(A_NG, 1, A_GROUP), (1, 1, S5_T))

        tiles = length // tm_p
        u, z, xbc, q, ktb, vb, kt_all, vt_all, dtf, dtft = _in_proj(
            xp, mod_p, 1, l * n, tiles, g_pm, w_all, tm_p, l, depth, kv_all, False)
        kv_all = (kt_all, vt_all)
        xt = u.reshape(n, chunks, S5_T, A_NG, A_GROUP).transpose(3, 0, 1, 2, 4).reshape(A_NG, n * chunks, S5_ROW)
        yt, hfin = _s5_prompt(xt, s5m[l], s5s[l], s5w[l], s5err[l], s5eii[l], d_t, n, chunks)
        ya = yt.reshape(A_NG, n, chunks, S5_T, A_GROUP).transpose(1, 2, 3, 0, 4).reshape(t, A_CH)
        yb, ssm_p = _ssd_prompt(xbc, z, dtf, dtft, zeros_conv, zeros_ssm, cw, cb, dtb_c, dtb_r,
                                al_c, al_r, dsk, ng, n, length)
        lfr, cc, cr = _fox_prep(dtf, dtft, bf_row, bf_col, n, length, tm_p)
        yc = _fox_flash(q, ktb, vb, cc, cr, n, length, tm_p)
        xp = _out_proj(xp, ya, yb, yc, w_g, b_g, w_o, g_qm, mod_p, 1, l * n, tiles, tm_p)
        xp = _ffn(xp, g_pf, w1, w2, g_qf, mod_p, 1, l * n, tiles, tm_p)
        p_out[2].append(lfr[:, F_OFF:F_OFF + C_HEADS].reshape(n, tiles, C_HEADS, tm_p)
                        .transpose(2, 0, 1, 3).reshape(C_HEADS, n, length))
        hfin = hfin.transpose(1, 0, 2)
        p_out[3].append(hfin[..., :A_P])
        p_out[4].append(hfin[..., A_P:])
        p_out[5].append(xbc.reshape(n, length, B_XBC)[:, length - (B_CONV - 1):])
        p_out[6].append(ssm_p)

        u, z, xbc, q, _, _, kt_s, vt_s, dtf, dtft, k, v = _in_proj(
            xs, mod_s, nb, l, 1, g_pm, w_all, nb, 0, 1, None, True)
        bre = jnp.einsum("gcp,gh->gchp", s5bbr[l], eye_g).reshape(A_CH, A_NG * A_P).astype(BF16)
        bim = jnp.einsum("gcp,gh->gchp", s5bbi[l], eye_g).reshape(A_CH, A_NG * A_P).astype(BF16)
        cre = jnp.einsum("gcp,gh->gphc", s5_c_re[l], eye_g).reshape(A_NG * A_P, A_CH).astype(BF16)
        cim = jnp.einsum("gcp,gh->gphc", s5_c_im[l], eye_g).reshape(A_NG * A_P, A_CH).astype(BF16)
        ya, s5r, s5i = _s5_sample(u, state_s5_re[l].reshape(nb, -1), state_s5_im[l].reshape(nb, -1),
                                  s5abr[l].reshape(1, -1), s5abi[l].reshape(1, -1), bre, bim, cre, cim, d_row)
        yb, ssm_all = _ssd_sample(xbc, conv_t, z, dtf, ssm_in, ssm_all, l, cw, cb, dtb_c, al_c, expand, dsk, ng)
        yc, lfn = _fox_sample(page_table, l, q, k, v, dtf, bf_row, ck, cv, clf)
        xs = _out_proj(xs, ya, yb, yc.reshape(nb, C_CH).astype(BF16), w_g, b_g, w_o, g_qm, mod_s, nb, l, 1, nb)
        xs = _ffn(xs, g_pf, w1, w2, g_qf, mod_s, nb, l, 1, nb)
        s_out[0].append(kt_s.reshape(C_HEADS, C_HD, nb))
        s_out[1].append(vt_s.reshape(C_HEADS, C_HD, nb))
        s_out[2].append(lfn[:, :, F_OFF:F_OFF + C_HEADS])
        s_out[3].append(s5r.reshape(nb, A_NG, A_P))
        s_out[4].append(s5i.reshape(nb, A_NG, A_P))
        s_out[5].append(jnp.stack([conv_t[l, 1], conv_t[l, 2], xbc], axis=0))

    plf, ps5r, ps5i, pconv, pssm = [jnp.stack(a) for a in p_out[2:]]
    sk, sv, slf, ss5r, ss5i, sconv = [jnp.stack(a) for a in s_out[:6]]
    kt_all, vt_all = kv_all
    pk = kt_all.reshape(depth, n, C_HEADS, C_HD, length).transpose(0, 1, 4, 2, 3)
    pv = vt_all.reshape(depth, n, C_HEADS, C_HD, length).transpose(0, 1, 4, 2, 3)
    plf = plf.transpose(0, 2, 3, 1)
    sk = sk.transpose(0, 3, 1, 2)[:, :, None]
    sv = sv.transpose(0, 3, 1, 2)[:, :, None]
    sconv = sconv.transpose(0, 2, 1, 3)
    sssm = ssm_all.reshape(depth, nb, B_HEADS, B_HD, B_N)
    return (xp.reshape(n, length, D_MODEL), xs.reshape(nb, 1, D_MODEL), pk, pv, plf, ps5r, ps5i, pconv, pssm,
            sk, sv, slf, ss5r, ss5i, sconv, sssm)
```

```python
import functools
import math

import jax
import jax.numpy as jnp
from jax import lax
from jax.experimental import pallas as pl
from jax.experimental.pallas import tpu as pltpu

F32 = jnp.float32
BF16 = jnp.bfloat16
HI = lax.Precision.HIGHEST

D_MODEL = 1024
A_CH = 256
A_GROUP = 16
A_NG = 16
A_P = 64
S5_T = 16
S5_ROW = S5_T * A_GROUP
S5_LEVELS = 8
B_HD = 64
B_CH = 384
B_HEADS = 6
B_NG = 2
B_N = 128
B_CONV = 4
B_XBC = 896
B_CHUNK = 128
C_HD = 64
C_CH = 384
C_HEADS = 6
PAGE = 128
D_FF = 4096
EPS = 1e-6
LOG2E = math.log2(math.e)
PAD_W = 128
F_OFF = 6
N_ALL = A_CH + B_CH + B_XBC + 3 * C_CH + PAD_W
VMEM_LIMIT = 56 * 1024 * 1024


def _cp(*sem):
    return pltpu.CompilerParams(dimension_semantics=sem, vmem_limit_bytes=VMEM_LIMIT)


def _sds(shape, dtype):
    return jax.ShapeDtypeStruct(shape, dtype)


def _dot(a, b, precision=None):
    return jnp.dot(a, b, preferred_element_type=F32, precision=precision)


def _dot_nt(a, b, precision=None):
    return lax.dot_general(a, b, (((1,), (1,)), ((), ())), preferred_element_type=F32, precision=precision)


def _dot_tn(a, b, precision=None):
    return lax.dot_general(a, b, (((0,), (0,)), ((), ())), preferred_element_type=F32, precision=precision)


def _silu(x):
    return x * jax.nn.sigmoid(x)


def _softplus(x):
    return jnp.maximum(x, 0.0) + jnp.log1p(jnp.exp(-jnp.abs(x)))


def _log_sigmoid(x):
    return jnp.minimum(x, 0.0) - jnp.log1p(jnp.exp(-jnp.abs(x)))


def _gelu_tanh(x):
    return 0.5 * x * (1.0 + jnp.tanh(math.sqrt(2.0 / math.pi) * (x + 0.044715 * (x * x * x))))


def _rms(x, g):
    return x * lax.rsqrt(jnp.mean(x * x, axis=-1, keepdims=True) + EPS) * g


def _iota(shape, dim):
    return lax.broadcasted_iota(jnp.int32, shape, dim)


def _ada_body(c_ref, w_ref, b_ref, o_ref):
    s = _silu(c_ref[...]).astype(BF16)
    o_ref[...] = _dot(s, w_ref[...].astype(BF16)) + b_ref[...]


def _ada_mod(cond, w_ada, b_ada):
    depth = w_ada.shape[0]
    rows = cond.shape[0]
    tn = 1536
    return pl.pallas_call(
        _ada_body,
        out_shape=_sds((depth, rows, 6 * D_MODEL), F32),
        grid=(depth, 6 * D_MODEL // tn),
        in_specs=[pl.BlockSpec((rows, D_MODEL), lambda l, j: (0, 0)),
                  pl.BlockSpec((None, D_MODEL, tn), lambda l, j: (l, 0, j)),
                  pl.BlockSpec((None, 1, tn), lambda l, j: (l, 0, j))],
        out_specs=pl.BlockSpec((None, rows, tn), lambda l, j: (l, 0, j)),
        compiler_params=_cp("arbitrary", "arbitrary"),
        name="ada_mod",
    )(cond, w_ada, b_ada.reshape(depth, 1, 6 * D_MODEL))


def _mod_spec(rows, base, tiles_per_group, piece):
    return pl.BlockSpec((None, rows, D_MODEL), lambda i: (base + i // tiles_per_group, 0, piece))


def _inproj_body(*refs, n_in, rows_out):
    x_ref, g_ref, sh_ref, sc_ref, w_ref = refs[:5]
    (u_ref, z_ref, xbc_ref, q_ref, qtb_ref, kb_ref, vtb_ref, kt_ref, vt_ref,
     dtf_ref, dtft_ref) = refs[n_in:n_in + 11]
    h = _rms(x_ref[...], g_ref[...]) * (1.0 + sc_ref[...]) + sh_ref[...]
    hb = h.astype(BF16)

    def mm(a, b):
        return _dot(hb, w_ref[:, a:b])

    o = 0
    u_ref[...] = mm(o, o + A_CH)
    o += A_CH
    z_ref[...] = mm(o, o + B_CH)
    o += B_CH
    xbc_ref[...] = mm(o, o + B_XBC)
    o += B_XBC
    q = mm(o, o + C_CH) * (C_HD ** -0.5)
    q_ref[...] = q.astype(BF16)
    qtb_ref[...] = (q * LOG2E).T.astype(BF16).reshape(qtb_ref.shape)
    o += C_CH
    k = mm(o, o + C_CH)
    kt_ref[...] = k.T
    kb_ref[...] = k.astype(BF16)
    o += C_CH
    v = mm(o, o + C_CH)
    vt = v.T
    vt_ref[...] = vt
    vtb_ref[...] = vt.astype(BF16).reshape(vtb_ref.shape)
    o += C_CH
    dtf = mm(o, o + PAD_W)
    dtf_ref[...] = dtf
    dtft_ref[...] = dtf.T[:16, :]
    if rows_out:
        k_ref, v_ref = refs[n_in + 11:n_in + 13]
        k_ref[...] = k
        v_ref[...] = v


def _in_proj(x, mod, mod_rows, mod_base, tiles_per_group, g_pre, w_all, tm, layer, depth, kv_all, rows_out):
    t = x.shape[0]
    groups = t // (tm * tiles_per_group)
    glen = tm * tiles_per_group
    row = lambda w: pl.BlockSpec((tm, w), lambda i: (i, 0))
    kv_spec = pl.BlockSpec((None, None, C_CH, tm),
                           lambda i: (layer, i // tiles_per_group, 0, i % tiles_per_group))
    pair_t = _sds((groups, C_HEADS // 2, tiles_per_group, 128, tm), BF16)
    pair_spec = pl.BlockSpec((None, C_HEADS // 2, None, 128, tm),
                             lambda i: (i // tiles_per_group, 0, i % tiles_per_group, 0, 0))
    outs = [_sds((t, A_CH), F32), _sds((t, B_CH), F32), _sds((t, B_XBC), F32), _sds((t, C_CH), BF16),
            pair_t, _sds((t, C_CH), BF16), pair_t,
            _sds((depth, groups, C_CH, glen), F32), _sds((depth, groups, C_CH, glen), F32),
            _sds((t, PAD_W), F32), _sds((16, t), F32)]
    out_specs = [row(A_CH), row(B_CH), row(B_XBC), row(C_CH), pair_spec, row(C_CH), pair_spec,
                 kv_spec, kv_spec, row(PAD_W), pl.BlockSpec((16, tm), lambda i: (0, i))]
    if rows_out:
        outs += [_sds((t, C_CH), F32), _sds((t, C_CH), F32)]
        out_specs += [row(C_CH), row(C_CH)]
    in_specs = [row(D_MODEL),
                pl.BlockSpec((1, D_MODEL), lambda i: (0, 0)),
                _mod_spec(mod_rows, mod_base, tiles_per_group, 0),
                _mod_spec(mod_rows, mod_base, tiles_per_group, 1),
                pl.BlockSpec((D_MODEL, N_ALL), lambda i: (0, 0))]
    args = [x, g_pre, mod, mod, w_all]
    aliases = {}
    if kv_all is not None:
        in_specs += [pl.BlockSpec(memory_space=pl.ANY), pl.BlockSpec(memory_space=pl.ANY)]
        args += list(kv_all)
        aliases = {5: 7, 6: 8}
    return pl.pallas_call(
        functools.partial(_inproj_body, n_in=len(args), rows_out=rows_out),
        out_shape=outs,
        grid=(t // tm,),
        in_specs=in_specs,
        out_specs=out_specs,
        input_output_aliases=aliases,
        compiler_params=_cp("arbitrary"),
        name="in_proj",
    )(*args)


def _s5_param_body(pw_ref, ldt_ref, arr_ref, air_ref, arc_ref, aic_ref, btr_ref, bti_ref, ctr_ref, cti_ref,
                   m_ref, sre_ref, sim_ref, w_ref, err_ref, eii_ref, abr_ref, abi_ref, bbr_ref, bbi_ref):
    dt = jnp.exp(ldt_ref[...])
    ar, ai = arr_ref[...], air_ref[...]
    mag = jnp.exp(ar * dt)
    abr, abi = mag * jnp.cos(ai * dt), mag * jnp.sin(ai * dt)
    den = ar * ar + ai * ai
    nr, ni = abr - 1.0, abi
    fr, fi = (nr * ar + ni * ai) / den, (ni * ar - nr * ai) / den
    btr, bti = btr_ref[...], bti_ref[...]
    bbr, bbi = fr * btr - fi * bti, fr * bti + fi * btr
    abr_ref[...] = abr
    abi_ref[...] = abi
    bbr_ref[...] = bbr[0:A_GROUP]
    bbi_ref[...] = bbi[0:A_GROUP]

    def epow(k, a_r, a_i):
        m = jnp.exp(k * a_r * dt)
        th = k * a_i * dt
        return m * jnp.cos(th), m * jnp.sin(th)

    s = jnp.right_shift(_iota((S5_ROW, 1), 0), 4).astype(F32)
    enr, eni = epow(-s, ar, ai)
    a_re, a_im = bbr * enr - bbi * eni, bbr * eni + bbi * enr
    elr, eli = epow((S5_T - 1.0) - s, ar, ai)
    sre_ref[...] = (bbr * elr - bbi * eli).astype(BF16)
    sim_ref[...] = (bbr * eli + bbi * elr).astype(BF16)

    arc, aic = arc_ref[...], aic_ref[...]
    t = jnp.right_shift(_iota((1, S5_ROW), 1), 4).astype(F32)
    etr, eti = epow(t, arc, aic)
    ctr, cti = ctr_ref[...], cti_ref[...]
    bm_re, bm_im = ctr * etr - cti * eti, ctr * eti + cti * etr
    m = _dot(a_re, bm_re, HI) - _dot(a_im, bm_im, HI)
    m_ref[...] = jnp.where(s <= t, m, 0.0).astype(BF16)
    e1r, e1i = epow(t + 1.0, arc, aic)
    w_ref[0:A_P, :] = (ctr * e1r - cti * e1i).astype(BF16)
    w_ref[A_P:2 * A_P, :] = (-(ctr * e1i + cti * e1r)).astype(BF16)

    lr, li = epow(pw_ref[...], ar, ai)
    err_ref[...] = jnp.concatenate([lr, lr], axis=1)
    eii_ref[...] = jnp.concatenate([-li, li], axis=1)


def _s5_params(a_re, a_im, log_dt, b_re, b_im, c_re, c_im):
    depth = a_re.shape[0]
    pw = (S5_T * 2.0 ** jnp.arange(S5_LEVELS, dtype=F32)).reshape(S5_LEVELS, 1)
    arr, air = a_re.reshape(depth, A_NG, 1, A_P), a_im.reshape(depth, A_NG, 1, A_P)
    arc, aic = a_re.reshape(depth, A_NG, A_P, 1), a_im.reshape(depth, A_NG, A_P, 1)
    ldt = log_dt.reshape(depth, A_NG, 1, 1)
    bt = lambda b: jnp.tile(jnp.swapaxes(b, 2, 3), (1, 1, S5_T, 1))
    ct = lambda c: jnp.tile(jnp.swapaxes(c, 2, 3), (1, 1, 1, S5_T))
    blk = lambda r, c: pl.BlockSpec((None, None, r, c), lambda l, g: (l, g, 0, 0))
    outs = [_sds((depth, A_NG, S5_ROW, S5_ROW), BF16),
            _sds((depth, A_NG, S5_ROW, A_P), BF16), _sds((depth, A_NG, S5_ROW, A_P), BF16),
            _sds((depth, A_NG, 2 * A_P, S5_ROW), BF16),
            _sds((depth, A_NG, S5_LEVELS, 2 * A_P), F32), _sds((depth, A_NG, S5_LEVELS, 2 * A_P), F32),
            _sds((depth, A_NG, 1, A_P), F32), _sds((depth, A_NG, 1, A_P), F32),
            _sds((depth, A_NG, A_GROUP, A_P), F32), _sds((depth, A_NG, A_GROUP, A_P), F32)]
    out_specs = [blk(S5_ROW, S5_ROW), blk(S5_ROW, A_P), blk(S5_ROW, A_P), blk(2 * A_P, S5_ROW),
                 blk(S5_LEVELS, 2 * A_P), blk(S5_LEVELS, 2 * A_P), blk(1, A_P), blk(1, A_P),
                 blk(A_GROUP, A_P), blk(A_GROUP, A_P)]
    return pl.pallas_call(
        _s5_param_body,
        out_shape=outs,
        grid=(depth, A_NG),
        in_specs=[pl.BlockSpec((S5_LEVELS, 1), lambda l, g: (0, 0)),
                  blk(1, 1), blk(1, A_P), blk(1, A_P), blk(A_P, 1), blk(A_P, 1),
                  blk(S5_ROW, A_P), blk(S5_ROW, A_P), blk(A_P, S5_ROW), blk(A_P, S5_ROW)],
        out_specs=out_specs,
        compiler_params=_cp("arbitrary", "arbitrary"),
        name="s5_params",
    )(pw, ldt, arr, air, arc, aic, bt(b_re), bt(b_im), ct(c_re), ct(c_im))


def _s5_prompt_body(x_ref, m_ref, s_ref, w_ref, err_ref, eii_ref, d_ref, y_ref, hf_ref, *, n, chunks, levels):
    x = x_ref[...]
    xb = x.astype(BF16)
    yi = _dot(xb, m_ref[...])
    b = _dot(xb, s_ref[...])
    cidx = jnp.bitwise_and(_iota((n * chunks, 1), 0), chunks - 1)
    for k in range(levels):
        d = 1 << k
        sh = jnp.where(cidx >= d, pltpu.roll(b, d, axis=0), 0.0)
        sw = pltpu.roll(sh, A_P, axis=1)
        b = b + err_ref[k:k + 1, :] * sh + eii_ref[k:k + 1, :] * sw
    hp = jnp.where(cidx >= 1, pltpu.roll(b, 1, axis=0), 0.0)
    y_ref[...] = yi + _dot(hp.astype(BF16), w_ref[...]) + d_ref[...] * x
    for i in range(n):
        hf_ref[i:i + 1, :] = b[(i + 1) * chunks - 1:(i + 1) * chunks, :]


def _s5_prompt(xt, m, s, w, err, eii, d_t, n, chunks):
    rows = n * chunks
    levels = max(1, (chunks - 1).bit_length())
    assert chunks & (chunks - 1) == 0 and levels <= S5_LEVELS
    blk = lambda r, c: pl.BlockSpec((None, r, c), lambda g: (g, 0, 0))
    return pl.pallas_call(
        functools.partial(_s5_prompt_body, n=n, chunks=chunks, levels=levels),
        out_shape=[_sds((A_NG, rows, S5_ROW), F32), _sds((A_NG, n, 2 * A_P), F32)],
        grid=(A_NG,),
        in_specs=[blk(rows, S5_ROW), blk(S5_ROW, S5_ROW), blk(S5_ROW, 2 * A_P), blk(2 * A_P, S5_ROW),
                  blk(S5_LEVELS, 2 * A_P), blk(S5_LEVELS, 2 * A_P), blk(1, S5_ROW)],
        out_specs=[blk(rows, S5_ROW), blk(n, 2 * A_P)],
        compiler_params=_cp("arbitrary"),
        name="s5_prompt",
    )(xt, m, s, w, err, eii, d_t)


def _s5_sample_body(u_ref, hr_ref, hi_ref, abr_ref, abi_ref, bre_ref, bim_ref, cre_ref, cim_ref, d_ref,
                    y_ref, sr_ref, si_ref):
    u = u_ref[...]
    ub = u.astype(BF16)
    hr, hi = hr_ref[...], hi_ref[...]
    abr, abi = abr_ref[...], abi_ref[...]
    sr = abr * hr - abi * hi + _dot(ub, bre_ref[...])
    si = abr * hi + abi * hr + _dot(ub, bim_ref[...])
    sr_ref[...] = sr
    si_ref[...] = si
    y_ref[...] = _dot(sr.astype(BF16), cre_ref[...]) - _dot(si.astype(BF16), cim_ref[...]) + d_ref[...] * u


def _s5_sample(u, hr, hi, abr, abi, bre, bim, cre, cim, d):
    b = u.shape[0]
    w = A_NG * A_P
    return pl.pallas_call(
        _s5_sample_body,
        out_shape=[_sds((b, A_CH), F32), _sds((b, w), F32), _sds((b, w), F32)],
        compiler_params=pltpu.CompilerParams(vmem_limit_bytes=VMEM_LIMIT),
        name="s5_sample",
    )(u, hr, hi, abr, abi, bre, bim, cre, cim, d)


def _ssd_prompt_body(xbc_ref, z_ref, dtc_ref, dtr_ref, cprev_ref, h0_ref, cw_ref, cb_ref,
                     dtbc_ref, dtbr_ref, alc_ref, alr_ref, dsk_ref, ng_ref,
                     y_ref, hf_ref, ext_scr, h_scr):
    c = pl.program_id(1)
    q = B_CHUNK

    @pl.when(c == 0)
    def _():
        ext_scr[0:8, :] = cprev_ref[...]
        h_scr[...] = h0_ref[...]

    x = xbc_ref[...]
    ext_scr[8:8 + q, :] = x
    cw = cw_ref[...]
    conv = (cb_ref[...] + cw[3:4] * x + cw[2:3] * ext_scr[7:7 + q, :]
            + cw[1:2] * ext_scr[6:6 + q, :] + cw[0:1] * ext_scr[5:5 + q, :])
    ext_scr[0:8, :] = x[q - 8:q, :]
    xc = _silu(conv)

    r_i, c_i = _iota((q, q), 0), _iota((q, q), 1)
    causal = r_i >= c_i
    dtc = _softplus(dtc_ref[...] + dtbc_ref[...])
    da_c = dtc * (-jnp.exp(alc_ref[...]))
    acc_c = _dot(causal.astype(F32), da_c, HI)
    dtr = _softplus(dtr_ref[...] + dtbr_ref[...])
    da_r = dtr * (-jnp.exp(alr_ref[...]))
    acc_r = _dot(da_r, (r_i <= c_i).astype(F32), HI)

    bm = [xc[:, B_CH + B_N * g:B_CH + B_N * (g + 1)].astype(BF16) for g in range(B_NG)]
    cm = [xc[:, B_CH + B_N * (B_NG + g):B_CH + B_N * (B_NG + g + 1)].astype(BF16) for g in range(B_NG)]
    cbm = [_dot_nt(cm[g], bm[g]) for g in range(B_NG)]
    lane = _iota((q, 128), 1)
    heads_per_group = B_HEADS // B_NG
    ys = []
    for j in range(B_HEADS // 2):
        xs_pair = xc[:, 128 * j:128 * (j + 1)]
        dt_pair = jnp.where(lane < B_HD, dtc[:, 2 * j:2 * j + 1], dtc[:, 2 * j + 1:2 * j + 2])
        xdt = xs_pair * dt_pair
        xdt_b = xdt.astype(BF16)
        hprev_b = h_scr[2 * j:2 * j + 2].reshape(2 * B_HD, B_N).astype(BF16)
        outs = []
        for e in range(2):
            h = 2 * j + e
            g = h // heads_per_group
            ac = acc_c[:, h:h + 1]
            ar = acc_r[h:h + 1, :]
            lm = jnp.exp(jnp.where(causal, ac - ar, -jnp.inf))
            yd = _dot((cbm[g] * lm).astype(BF16), xdt_b)
            yo = _dot_nt(cm[g], hprev_b) * jnp.exp(ac)
            outs.append(yd + yo)
            alast = acc_r[h:h + 1, q - 1:q]
            st = _dot_tn((xdt * jnp.exp(alast - ac)).astype(BF16), bm[g])
            h_scr[h] = jnp.exp(alast) * h_scr[h] + st[B_HD * e:B_HD * (e + 1), :]
        ys.append(jnp.where(lane < B_HD, outs[0], outs[1]) + dsk_ref[:, 128 * j:128 * (j + 1)] * xs_pair)
    y = jnp.concatenate(ys, axis=1)
    y_ref[...] = _rms(y * _silu(z_ref[...]), ng_ref[...]).astype(y_ref.dtype)

    @pl.when(c == pl.num_programs(1) - 1)
    def _():
        hf_ref[...] = h_scr[...]


def _ssd_prompt(xbc, z, dtf, dtft, cprev, h0, cw, cb, dtb_c, dtb_r, al_c, al_r, dsk, ng, n, length):
    nc = length // B_CHUNK
    q = B_CHUNK
    tok = lambda w: pl.BlockSpec((q, w), lambda i, c: (i * nc + c, 0))
    const = lambda r, w: pl.BlockSpec((r, w), lambda i, c: (0, 0))
    return pl.pallas_call(
        _ssd_prompt_body,
        out_shape=[_sds((n * length, B_CH), BF16), _sds((n, B_HEADS, B_HD, B_N), F32)],
        grid=(n, nc),
        in_specs=[tok(B_XBC), tok(B_CH), tok(PAD_W),
                  pl.BlockSpec((16, q), lambda i, c: (0, i * nc + c)),
                  pl.BlockSpec((None, 8, B_XBC), lambda i, c: (i, 0, 0)),
                  pl.BlockSpec((None, B_HEADS, B_HD, B_N), lambda i, c: (i, 0, 0, 0)),
                  const(B_CONV, B_XBC), const(1, B_XBC), const(1, PAD_W), const(16, 1),
                  const(1, PAD_W), const(16, 1), const(1, B_CH), const(1, B_CH)],
        out_specs=[tok(B_CH), pl.BlockSpec((None, B_HEADS, B_HD, B_N), lambda i, c: (i, 0, 0, 0))],
        scratch_shapes=[pltpu.VMEM((8 + q, B_XBC), F32), pltpu.VMEM((B_HEADS, B_HD, B_N), F32)],
        compiler_params=_cp("arbitrary", "arbitrary"),
        name="ssd_prompt",
    )(xbc, z, dtf, dtft, cprev, h0, cw, cb, dtb_c, dtb_r, al_c, al_r, dsk, ng)


def _ssd_sample_body(xbc_ref, p0_ref, p1_ref, p2_ref, z_ref, dtf_ref, h0_ref, cw_ref, cb_ref,
                     dtb_ref, al_ref, ex_ref, dsk_ref, ng_ref, *rest, tb):
    y_ref, hn_ref, y_scr = rest[-3:]
    x = xbc_ref[...]
    cw = cw_ref[...]
    conv = cb_ref[...] + cw[0:1] * p0_ref[...] + cw[1:2] * p1_ref[...] + cw[2:3] * p2_ref[...] + cw[3:4] * x
    xc = _silu(conv)
    dt = _softplus(dtf_ref[...] + dtb_ref[...])
    e = jnp.exp(dt * (-jnp.exp(al_ref[...])))
    ex = ex_ref[...]
    dt_e = _dot(dt, ex, HI)
    e_e = _dot(e, ex, HI)
    xs = xc[:, :B_CH]
    xdt = xs * dt_e
    half = B_CH // B_NG
    row8, lane8 = _iota((8, B_CH), 0), _iota((8, B_CH), 1)
    r8 = _iota((8, B_N), 0)
    lane1 = _iota((1, B_CH), 1)
    for b in range(tb):
        xr = xdt[b:b + 1, :]
        lhs = jnp.where(row8 == 0, e_e[b:b + 1, :],
                        jnp.where(((row8 == 1) & (lane8 < half)) | ((row8 == 2) & (lane8 >= half)), xr, 0.0))
        rhs_e = jnp.where(r8 == 0, 1.0, 0.0)
        rhs_s = jnp.where(r8 == 1, xc[b:b + 1, B_CH:B_CH + B_N],
                          jnp.where(r8 == 2, xc[b:b + 1, B_CH + B_N:B_CH + 2 * B_N], 0.0))
        hn = _dot_tn(lhs, rhs_e, HI) * h0_ref[b] + _dot_tn(lhs, rhs_s, HI)
        hn_ref[b] = hn
        cmat = jnp.where(r8 == 0, xc[b:b + 1, B_CH + 2 * B_N:B_CH + 3 * B_N],
                         jnp.where(r8 == 1, xc[b:b + 1, B_CH + 3 * B_N:B_CH + 4 * B_N], 0.0))
        yt = _dot_nt(cmat.astype(BF16), hn.astype(BF16))
        y_scr[b:b + 1, :] = jnp.where(lane1 < half, yt[0:1, :], yt[1:2, :])
    y = y_scr[...] + dsk_ref[...] * xs
    y_ref[...] = _rms(y * _silu(z_ref[...]), ng_ref[...]).astype(y_ref.dtype)


def _ssd_sample(xbc, conv_t, z, dtf, ssm_in, ssm_out, layer, cw, cb, dtb, al, ex, dsk, ng):
    depth, b = ssm_in.shape[0], xbc.shape[0]
    tb = 8
    row = lambda w: pl.BlockSpec((tb, w), lambda i: (i, 0))
    const = lambda r, w: pl.BlockSpec((r, w), lambda i: (0, 0))
    prev = lambda k: pl.BlockSpec((None, None, tb, B_XBC), lambda i: (layer, k, i, 0))
    st = pl.BlockSpec((None, tb, B_CH, B_N), lambda i: (layer, i, 0, 0))
    in_specs = [row(B_XBC), prev(0), prev(1), prev(2), row(B_CH), row(PAD_W), st,
                const(B_CONV, B_XBC), const(1, B_XBC), const(1, PAD_W), const(1, PAD_W),
                const(PAD_W, B_CH), const(1, B_CH), const(1, B_CH)]
    args = [xbc, conv_t, conv_t, conv_t, z, dtf, ssm_in, cw, cb, dtb, al, ex, dsk, ng]
    aliases = {}
    if ssm_out is not None:
        in_specs.append(pl.BlockSpec(memory_space=pl.ANY))
        args.append(ssm_out)
        aliases = {len(args) - 1: 1}
    return pl.pallas_call(
        functools.partial(_ssd_sample_body, tb=tb),
        out_shape=[_sds((b, B_CH), BF16), _sds((depth, b, B_CH, B_N), F32)],
        grid=(b // tb,),
        in_specs=in_specs,
        out_specs=[row(B_CH), st],
        scratch_shapes=[pltpu.VMEM((tb, B_CH), F32)],
        input_output_aliases=aliases,
        compiler_params=_cp("arbitrary"),
        name="ssd_sample",
    )(*args)


def _split3(x):
    hi = x.astype(BF16)
    r1 = x - hi.astype(F32)
    mid = r1.astype(BF16)
    lo = (r1 - mid.astype(F32)).astype(BF16)
    return hi, mid, lo


def _fox_prep_body(fc_ref, fr_ref, bfr_ref, bfc_ref, place_ref, ones_ref, lfr_ref, kx_ref, cr_ref,
                   carry_c, carry_r):
    j = pl.program_id(1)
    tl = fc_ref.shape[0]

    @pl.when(j == 0)
    def _():
        carry_c[...] = jnp.zeros_like(carry_c)
        carry_r[...] = jnp.zeros_like(carry_r)

    r_i, c_i = _iota((tl, tl), 0), _iota((tl, tl), 1)
    lfc = _log_sigmoid(fc_ref[...] + bfr_ref[...])
    cc = _dot((r_i >= c_i).astype(F32), lfc, HI) + carry_c[0:1, :]
    carry_c[...] = jnp.broadcast_to(cc[tl - 1:tl, :], carry_c.shape)
    parts = _split3(-LOG2E * cc)
    kx = ones_ref[...] + sum(_dot(parts[i], place_ref[i]) for i in range(3))
    kx_ref[...] = kx.astype(BF16)
    lfr = _log_sigmoid(fr_ref[...] + bfc_ref[...])
    lfr_ref[...] = lfr
    cr = _dot(lfr, (r_i <= c_i).astype(F32), HI) + carry_r[:, 0:1]
    cr_ref[...] = cr
    carry_r[...] = jnp.broadcast_to(cr[:, tl - 1:tl], carry_r.shape)


def _bias_slot(pair, head, term):
    return 16 * pair + 3 * head + term


def _fox_prep(dtf, dtft, bf_row, bf_col, n, length, tl):
    nt = length // tl
    t = n * length
    place = jnp.zeros((3, PAD_W, 128), F32)
    ones = jnp.zeros((1, 128), F32)
    for pair in range(C_HEADS // 2):
        for term in range(3):
            ones = ones.at[0, _bias_slot(pair, 2, term)].set(1.0)
            for head in range(2):
                place = place.at[term, F_OFF + 2 * pair + head, _bias_slot(pair, head, term)].set(1.0)
    tok = pl.BlockSpec((tl, PAD_W), lambda i, j: (i * nt + j, 0))
    rowt = pl.BlockSpec((None, 16, tl), lambda i, j: (i * nt + j, 0, 0))
    return pl.pallas_call(
        _fox_prep_body,
        out_shape=[_sds((n * nt, 16, tl), F32), _sds((t, 128), BF16), _sds((n * nt, 16, tl), F32)],
        grid=(n, nt),
        in_specs=[tok, pl.BlockSpec((16, tl), lambda i, j: (0, i * nt + j)),
                  pl.BlockSpec((1, PAD_W), lambda i, j: (0, 0)), pl.BlockSpec((16, 1), lambda i, j: (0, 0)),
                  pl.BlockSpec((3, PAD_W, 128), lambda i, j: (0, 0, 0)),
                  pl.BlockSpec((1, 128), lambda i, j: (0, 0))],
        out_specs=[rowt, pl.BlockSpec((tl, 128), lambda i, j: (i * nt + j, 0)), rowt],
        scratch_shapes=[pltpu.VMEM((8, PAD_W), F32), pltpu.VMEM((16, 128), F32)],
        compiler_params=_cp("arbitrary", "arbitrary"),
        name="fox_prep",
    )(dtf, dtft, bf_row, bf_col, place.astype(BF16), ones)


def _fox_flash_body(qt_ref, k_ref, kx_ref, vt_ref, cr_ref, o_ref, qa_scr, m_scr, l_scr, acc_scr):
    hp, qi = pl.program_id(1), pl.program_id(2)
    tq = qt_ref.shape[1]
    tk = tq
    row = _iota((128, tq), 0)
    qt = qt_ref[...]
    base = 16 * hp
    for e in range(2):
        cq = _split3(LOG2E * cr_ref[pl.ds(F_OFF + 2 * hp + e, 1), :])
        qx = jnp.where((row >= base + 3 * e) & (row < base + 3 * e + 3), 1.0, 0.0).astype(BF16)
        for term in range(3):
            qx = jnp.where(row == base + 6 + term, cq[term], qx)
        qa_scr[e, 0:128, :] = jnp.where((row < C_HD) == (e == 0), qt, jnp.zeros_like(qt))
        qa_scr[e, 128:256, :] = qx
    m_scr[...] = jnp.full(m_scr.shape, -jnp.inf, F32)
    l_scr[...] = jnp.zeros_like(l_scr)
    acc_scr[...] = jnp.zeros_like(acc_scr)

    def block(ki, diagonal):
        rows = pl.ds(pl.multiple_of(ki * tk, tk), tk)
        ka = jnp.concatenate([k_ref[rows, :], kx_ref[rows, :]], axis=1)
        vt = vt_ref[ki]
        for e in range(2):
            s = _dot(ka, qa_scr[e])
            if diagonal:
                s = jnp.where(_iota((tk, tq), 0) <= _iota((tk, tq), 1), s, -jnp.inf)
            m_old = m_scr[e]
            m_new = jnp.maximum(m_old, jnp.max(s, axis=0, keepdims=True))
            alpha = jnp.exp2(m_old - m_new)
            p = jnp.exp2(s - m_new)
            l_scr[e] = alpha * l_scr[e] + jnp.sum(p, axis=0, keepdims=True)
            acc_scr[e] = alpha * acc_scr[e] + _dot(vt, p.astype(BF16))
            m_scr[e] = m_new

    def below_diagonal(ki, carry):
        block(ki, False)
        return carry

    lax.fori_loop(0, qi, below_diagonal, 0)
    block(qi, True)
    ot = jnp.where(row < C_HD, acc_scr[0] / l_scr[0], acc_scr[1] / l_scr[1])
    o_ref[...] = ot.T.astype(o_ref.dtype)


def _fox_flash(qt, k, kx, vt, cr, n, length, tq):
    nq = length // tq
    return pl.pallas_call(
        _fox_flash_body,
        out_shape=_sds((n * length, C_CH), BF16),
        grid=(n, C_HEADS // 2, nq),
        in_specs=[pl.BlockSpec((None, None, None, 128, tq), lambda i, h, a: (i, h, a, 0, 0)),
                  pl.BlockSpec((length, 128), lambda i, h, a: (i, h)),
                  pl.BlockSpec((length, 128), lambda i, h, a: (i, 0)),
                  pl.BlockSpec((None, None, nq, 128, tq), lambda i, h, a: (i, h, 0, 0, 0)),
                  pl.BlockSpec((None, 16, tq), lambda i, h, a: (i * nq + a, 0, 0))],
        out_specs=pl.BlockSpec((tq, 128), lambda i, h, a: (i * nq + a, h)),
        scratch_shapes=[pltpu.VMEM((2, 256, tq), BF16), pltpu.VMEM((2, 1, tq), F32),
                        pltpu.VMEM((2, 1, tq), F32), pltpu.VMEM((2, 128, tq), F32)],
        compiler_params=_cp("arbitrary", "arbitrary", "arbitrary"),
        name="fox_flash",
    )(qt, k, kx, vt, cr)


def _fox_sample_body(pt_ref, q_ref, kn_ref, vn_ref, fn_ref, bf_ref, *refs, pages):
    k_refs = refs[0:pages]
    v_refs = refs[pages:2 * pages]
    f_refs = refs[2 * pages:3 * pages]
    o_ref, lf_ref, lf_scr = refs[3 * pages:3 * pages + 3]
    i = pl.program_id(0)
    row, lane = _iota((8, C_CH), 0), _iota((8, C_CH), 1)
    own = jnp.right_shift(lane, 6) == row
    qm = jnp.where(own, q_ref[...].astype(F32), 0.0)
    qb = qm.astype(BF16)
    lfn = _log_sigmoid(fn_ref[...] + bf_ref[...])
    lf_ref[...] = lfn
    carry = jnp.sum(jnp.where(_iota((8, PAD_W), 1) == F_OFF + _iota((8, PAD_W), 0), lfn, 0.0),
                    axis=1, keepdims=True)
    for p in range(pages):
        r = jnp.bitwise_and(pt_ref[i, p], 7)
        for h in range(C_HEADS):
            lf_scr[8 * p + h:8 * p + h + 1, :] = f_refs[p][h, pl.ds(r, 1), :]
        lf_scr[8 * p + C_HEADS:8 * p + 8, :] = jnp.zeros((8 - C_HEADS, PAGE), F32)
    lf_all = lf_scr[...]
    later = (_iota((PAGE, PAGE), 0) > _iota((PAGE, PAGE), 1)).astype(F32)
    suffix = _dot(lf_all, later, HI)
    total = jnp.sum(lf_all, axis=1, keepdims=True)
    s = [None] * pages
    for p in reversed(range(pages)):
        kt = k_refs[p][...].reshape(C_CH, PAGE).astype(BF16)
        s[p] = _dot(qb, kt) + suffix[8 * p:8 * p + 8, :] + carry
        carry = carry + total[8 * p:8 * p + 8, :]
    s_new = jnp.sum(qm * kn_ref[...], axis=1, keepdims=True)
    m = s_new
    for p in range(pages):
        m = jnp.maximum(m, jnp.max(s[p], axis=1, keepdims=True))
    p_new = jnp.exp(s_new - m)
    l = p_new
    acc = p_new * vn_ref[...]
    for p in range(pages):
        pr = jnp.exp(s[p] - m)
        l = l + jnp.sum(pr, axis=1, keepdims=True)
        acc = acc + _dot_nt(pr.astype(BF16), v_refs[p][...].reshape(C_CH, PAGE).astype(BF16))
    o_ref[...] = jnp.sum(jnp.where(own, acc / l, 0.0), axis=0, keepdims=True)


def _fox_sample(page_table, layer, q, k_new, v_new, f_new, bf_row, cache_kt, cache_vt, cache_lft):
    b = q.shape[0]
    pages = page_table.shape[1]
    tok = lambda w: pl.BlockSpec((None, 1, w), lambda i, pt: (i, 0, 0))
    page = lambda p: pl.BlockSpec((None, None, C_HEADS, C_HD, PAGE), lambda i, pt: (layer, pt[i, p], 0, 0, 0))
    lfpage = lambda p: pl.BlockSpec((None, C_HEADS, 8, PAGE), lambda i, pt: (layer, 0, pt[i, p] // 8, 0))
    in_specs = ([tok(C_CH), tok(C_CH), tok(C_CH), tok(PAD_W), pl.BlockSpec((1, PAD_W), lambda i, pt: (0, 0))]
                + [page(p) for p in range(pages)] + [page(p) for p in range(pages)]
                + [lfpage(p) for p in range(pages)])
    grid_spec = pltpu.PrefetchScalarGridSpec(
        num_scalar_prefetch=1, grid=(b,), in_specs=in_specs,
        out_specs=[tok(C_CH), tok(PAD_W)],
        scratch_shapes=[pltpu.VMEM((8 * pages, PAGE), F32)])
    return pl.pallas_call(
        functools.partial(_fox_sample_body, pages=pages),
        out_shape=[_sds((b, 1, C_CH), F32), _sds((b, 1, PAD_W), F32)],
        grid_spec=grid_spec,
        compiler_params=_cp("arbitrary"),
        name="fox_sample",
    )(page_table, q.reshape(b, 1, C_CH), k_new.reshape(b, 1, C_CH), v_new.reshape(b, 1, C_CH),
      f_new.reshape(b, 1, PAD_W), bf_row,
      *([cache_kt] * pages), *([cache_vt] * pages), *([cache_lft] * pages))


def _outproj_body(x_ref, ya_ref, yb_ref, yc_ref, wg_ref, bg_ref, wo_ref, gp_ref, g1_ref, o_ref):
    y = _gelu_tanh(ya_ref[...])
    gate = jax.nn.sigmoid(_dot(y.astype(BF16), wg_ref[...]) + bg_ref[...])
    ya = (y * gate).astype(BF16)
    mix = (_dot(ya, wo_ref[0:A_CH, :]) + _dot(yb_ref[...], wo_ref[A_CH:A_CH + B_CH, :])
           + _dot(yc_ref[...], wo_ref[A_CH + B_CH:, :]))
    o_ref[...] = x_ref[...] + g1_ref[...] * _rms(mix, gp_ref[...])


def _out_proj(x, ya, yb, yc, w_glu, b_glu, w_out, g_post, mod, mod_rows, mod_base, tiles_per_group, tm):
    t = x.shape[0]
    row = lambda w: pl.BlockSpec((tm, w), lambda i: (i, 0))
    const = lambda r, w: pl.BlockSpec((r, w), lambda i: (0, 0))
    return pl.pallas_call(
        _outproj_body,
        out_shape=_sds((t, D_MODEL), F32),
        grid=(t // tm,),
        in_specs=[row(D_MODEL), row(A_CH), row(B_CH), row(C_CH), const(A_CH, A_CH), const(1, A_CH),
                  const(D_MODEL, D_MODEL), const(1, D_MODEL), _mod_spec(mod_rows, mod_base, tiles_per_group, 2)],
        out_specs=row(D_MODEL),
        compiler_params=_cp("arbitrary"),
        name="out_proj",
    )(x, ya, yb, yc, w_glu, b_glu, w_out, g_post, mod)


def _ffn_body(x_ref, gpre_ref, sh_ref, sc_ref, w1_ref, w2_ref, gpost_ref, g2_ref, o_ref, *, tf):
    x = x_ref[...]
    hb = (_rms(x, gpre_ref[...]) * (1.0 + sc_ref[...]) + sh_ref[...]).astype(BF16)
    acc = jnp.zeros(x.shape, F32)
    for j in range(D_FF // tf):
        a = jnp.maximum(_dot(hb, w1_ref[:, j * tf:(j + 1) * tf]), 0.0)
        acc = acc + _dot((a * a).astype(BF16), w2_ref[j * tf:(j + 1) * tf, :])
    o_ref[...] = x + g2_ref[...] * _rms(acc, gpost_ref[...])


def _ffn(x, g_pre, w1, w2, g_post, mod, mod_rows, mod_base, tiles_per_group, tm):
    t = x.shape[0]
    row = pl.BlockSpec((tm, D_MODEL), lambda i: (i, 0))
    const = lambda r, w: pl.BlockSpec((r, w), lambda i: (0, 0))
    weight = lambda r, w: pl.BlockSpec((r, w), lambda i: (0, 0), pipeline_mode=pl.Buffered(1))
    return pl.pallas_call(
        functools.partial(_ffn_body, tf=1024),
        out_shape=_sds((t, D_MODEL), F32),
        grid=(t // tm,),
        in_specs=[row, const(1, D_MODEL), _mod_spec(mod_rows, mod_base, tiles_per_group, 3),
                  _mod_spec(mod_rows, mod_base, tiles_per_group, 4), weight(D_MODEL, D_FF), weight(D_FF, D_MODEL),
                  const(1, D_MODEL), _mod_spec(mod_rows, mod_base, tiles_per_group, 5)],
        out_specs=row,
        compiler_params=_cp("arbitrary"),
        name="ffn",
    )(x, g_pre, mod, mod, w1, w2, g_post, mod)


def _pad_lanes(v, offset, width=PAD_W):
    return jnp.zeros((1, width), F32).at[0, offset:offset + v.shape[0]].set(v)


def _pad_rows(v, offset, rows=16):
    return jnp.zeros((rows, 1), F32).at[offset:offset + v.shape[0], 0].set(v)


def kernel(x_prompt, x_sample, c_prompt, c_sample, cache_k, cache_v, cache_logf, page_table, state_s5_re, state_s5_im, state_conv, state_ssm, w_ada, b_ada, g_pre_mix, g_post_mix, g_pre_ffn, g_post_ffn, w_in, w_out, s5_a_re, s5_a_im, s5_log_dt, s5_b_re, s5_b_im, s5_c_re, s5_c_im, s5_d, s5_w_glu, s5_b_glu, ssd_conv_w, ssd_conv_b, ssd_dt_bias, ssd_a_log, ssd_d, ssd_norm_g, fox_b_f, w_ff1, w_ff2):
    depth = w_ada.shape[0]
    n, length, _ = x_prompt.shape
    nb = x_sample.shape[0]
    t = n * length
    n_pool = cache_k.shape[1]
    chunks = length // S5_T
    tm_p = 512
    assert x_sample.shape[1] == 1 and length % tm_p == 0 and nb % 8 == 0

    rows = -(-(n + nb) // 8) * 8
    cond = jnp.concatenate([c_prompt, c_sample, jnp.zeros((rows - n - nb, D_MODEL), F32)], axis=0)
    mod = _ada_mod(cond, w_ada, b_ada)
    mod_p = mod[:, :n].reshape(depth * n, 1, 6 * D_MODEL)
    mod_s = mod[:, n:n + nb]

    s5m, s5sre, s5sim, s5w, s5err, s5eii, s5abr, s5abi, s5bbr, s5bbi = _s5_params(
        s5_a_re, s5_a_im, s5_log_dt, s5_b_re, s5_b_im, s5_c_re, s5_c_im)
    s5s = jnp.concatenate([s5sre, s5sim], axis=-1)
    eye_g = jnp.eye(A_NG, dtype=F32)
    expand = (jnp.arange(PAD_W)[:, None] == jnp.arange(B_CH)[None, :] // B_HD).astype(F32)

    ck = cache_k.transpose(0, 1, 3, 4, 2)
    cv = cache_v.transpose(0, 1, 3, 4, 2)
    clf = cache_logf.transpose(0, 3, 1, 2)
    conv_t = state_conv.transpose(0, 2, 1, 3)
    ssm_in = state_ssm.reshape(depth, nb, B_CH, B_N)

    xp = x_prompt.reshape(t, D_MODEL)
    xs = x_sample.reshape(nb, D_MODEL)
    zeros_conv = jnp.zeros((n, 8, B_XBC), F32)
    zeros_ssm = jnp.zeros((n, B_HEADS, B_HD, B_N), F32)
    p_out = [[] for _ in range(7)]
    s_out = [[] for _ in range(7)]
    kv_all = None
    ssm_all = None

    for l in range(depth):
        wi = w_in[l]
        o_dt = A_CH + B_CH + B_XBC
        o_f = o_dt + B_HEADS + 3 * C_CH
        w_all = jnp.concatenate(
            [wi[:, :o_dt], wi[:, o_dt + B_HEADS:o_f], wi[:, o_dt:o_dt + B_HEADS], wi[:, o_f:],
             jnp.zeros((D_MODEL, PAD_W - B_HEADS - C_HEADS), F32)], axis=1).astype(BF16)
        w_o = w_out[l].astype(BF16)
        w_g = s5_w_glu[l].astype(BF16)
        w1 = w_ff1[l].astype(BF16)
        w2 = w_ff2[l].astype(BF16)
        row = lambda v: v.reshape(1, -1)
        g_pm, g_qm, g_pf, g_qf = row(g_pre_mix[l]), row(g_post_mix[l]), row(g_pre_ffn[l]), row(g_post_ffn[l])
        b_g = row(s5_b_glu[l])
        cw, cb = ssd_conv_w[l], row(ssd_conv_b[l])
        dtb_c, dtb_r = _pad_lanes(ssd_dt_bias[l], 0), _pad_rows(ssd_dt_bias[l], 0)
        al_c, al_r = _pad_lanes(ssd_a_log[l], 0), _pad_rows(ssd_a_log[l], 0)
        dsk = row(jnp.repeat(ssd_d[l], B_HD))
        ng = row(ssd_norm_g[l])
        bf_row, bf_col = _pad_lanes(fox_b_f[l], F_OFF), _pad_rows(fox_b_f[l], F_OFF)
        d_row = row(s5_d[l])
        d_t = jnp.tile(s5_d[l].reshape(A_NG, 1, A_GROUP), (1, 1, S5_T))

        tiles = length // tm_p
        u, z, xbc, _, qtb, kb, vtb, kt_all, vt_all, dtf, dtft = _in_proj(
            xp, mod_p, 1, l * n, tiles, g_pm, w_all, tm_p, l, depth, kv_all, False)
        kv_all = (kt_all, vt_all)
        xt = u.reshape(n, chunks, S5_T, A_NG, A_GROUP).transpose(3, 0, 1, 2, 4).reshape(A_NG, n * chunks, S5_ROW)
        yt, hfin = _s5_prompt(xt, s5m[l], s5s[l], s5w[l], s5err[l], s5eii[l], d_t, n, chunks)
        ya = yt.reshape(A_NG, n, chunks, S5_T, A_GROUP).transpose(1, 2, 3, 0, 4).reshape(t, A_CH)
        yb, ssm_p = _ssd_prompt(xbc, z, dtf, dtft, zeros_conv, zeros_ssm, cw, cb, dtb_c, dtb_r,
                                al_c, al_r, dsk, ng, n, length)
        lfr, kx, cr = _fox_prep(dtf, dtft, bf_row, bf_col, n, length, tm_p)
        yc = _fox_flash(qtb, kb, kx, vtb, cr, n, length, tm_p)
        xp = _out_proj(xp, ya, yb, yc, w_g, b_g, w_o, g_qm, mod_p, 1, l * n, tiles, tm_p)
        xp = _ffn(xp, g_pf, w1, w2, g_qf, mod_p, 1, l * n, tiles, tm_p)
        p_out[2].append(lfr[:, F_OFF:F_OFF + C_HEADS].reshape(n, tiles, C_HEADS, tm_p)
                        .transpose(2, 0, 1, 3).reshape(C_HEADS, n, length))
        hfin = hfin.transpose(1, 0, 2)
        p_out[3].append(hfin[..., :A_P])
        p_out[4].append(hfin[..., A_P:])
        p_out[5].append(xbc.reshape(n, length, B_XBC)[:, length - (B_CONV - 1):])
        p_out[6].append(ssm_p)

        u, z, xbc, q, _, _, _, kt_s, vt_s, dtf, dtft, k, v = _in_proj(
            xs, mod_s, nb, l, 1, g_pm, w_all, nb, 0, 1, None, True)
        bre = jnp.einsum("gcp,gh->gchp", s5bbr[l], eye_g).reshape(A_CH, A_NG * A_P).astype(BF16)
        bim = jnp.einsum("gcp,gh->gchp", s5bbi[l], eye_g).reshape(A_CH, A_NG * A_P).astype(BF16)
        cre = jnp.einsum("gcp,gh->gphc", s5_c_re[l], eye_g).reshape(A_NG * A_P, A_CH).astype(BF16)
        cim = jnp.einsum("gcp,gh->gphc", s5_c_im[l], eye_g).reshape(A_NG * A_P, A_CH).astype(BF16)
        ya, s5r, s5i = _s5_sample(u, state_s5_re[l].reshape(nb, -1), state_s5_im[l].reshape(nb, -1),
                                  s5abr[l].reshape(1, -1), s5abi[l].reshape(1, -1), bre, bim, cre, cim, d_row)
        yb, ssm_all = _ssd_sample(xbc, conv_t, z, dtf, ssm_in, ssm_all, l, cw, cb, dtb_c, al_c, expand, dsk, ng)
        yc, lfn = _fox_sample(page_table, l, q, k, v, dtf, bf_row, ck, cv, clf)
        xs = _out_proj(xs, ya, yb, yc.reshape(nb, C_CH).astype(BF16), w_g, b_g, w_o, g_qm, mod_s, nb, l, 1, nb)
        xs = _ffn(xs, g_pf, w1, w2, g_qf, mod_s, nb, l, 1, nb)
        s_out[0].append(kt_s.reshape(C_HEADS, C_HD, nb))
        s_out[1].append(vt_s.reshape(C_HEADS, C_HD, nb))
        s_out[2].append(lfn[:, :, F_OFF:F_OFF + C_HEADS])
        s_out[3].append(s5r.reshape(nb, A_NG, A_P))
        s_out[4].append(s5i.reshape(nb, A_NG, A_P))
        s_out[5].append(jnp.stack([conv_t[l, 1], conv_t[l, 2], xbc], axis=0))

    plf, ps5r, ps5i, pconv, pssm = [jnp.stack(a) for a in p_out[2:]]
    sk, sv, slf, ss5r, ss5i, sconv = [jnp.stack(a) for a in s_out[:6]]
    kt_all, vt_all = kv_all
    pk = kt_all.reshape(depth, n, C_HEADS, C_HD, length).transpose(0, 1, 4, 2, 3)
    pv = vt_all.reshape(depth, n, C_HEADS, C_HD, length).transpose(0, 1, 4, 2, 3)
    plf = plf.transpose(0, 2, 3, 1)
    sk = sk.transpose(0, 3, 1, 2)[:, :, None]
    sv = sv.transpose(0, 3, 1, 2)[:, :, None]
    sconv = sconv.transpose(0, 2, 1, 3)
    sssm = ssm_all.reshape(depth, nb, B_HEADS, B_HD, B_N)
    return (xp.reshape(n, length, D_MODEL), xs.reshape(nb, 1, D_MODEL), pk, pv, plf, ps5r, ps5i, pconv, pssm,
            sk, sv, slf, ss5r, ss5i, sconv, sssm)
```

```python
import functools
import math

import jax
import jax.numpy as jnp
from jax import lax
from jax.experimental import pallas as pl
from jax.experimental.pallas import tpu as pltpu

F32 = jnp.float32
BF16 = jnp.bfloat16
HI = lax.Precision.HIGHEST

D_MODEL = 1024
A_CH = 256
A_GROUP = 16
A_NG = 16
A_P = 64
S5_T = 16
S5_ROW = S5_T * A_GROUP
S5_LEVELS = 8
S5_ROWS_E = 2 * S5_T + S5_LEVELS
B_HD = 64
B_CH = 384
B_HEADS = 6
B_NG = 2
B_N = 128
B_CONV = 4
B_XBC = 896
B_CHUNK = 128
C_HD = 64
C_CH = 384
C_HEADS = 6
PAGE = 128
D_FF = 4096
EPS = 1e-6
LOG2E = math.log2(math.e)
PAD_W = 128
F_OFF = 6
N_ALL = A_CH + B_CH + B_XBC + 3 * C_CH + PAD_W
VMEM_LIMIT = 56 * 1024 * 1024


def _cp(*sem):
    return pltpu.CompilerParams(dimension_semantics=sem, vmem_limit_bytes=VMEM_LIMIT)


def _sds(shape, dtype):
    return jax.ShapeDtypeStruct(shape, dtype)


def _dot(a, b, precision=None):
    return jnp.dot(a, b, preferred_element_type=F32, precision=precision)


def _dot_nt(a, b, precision=None):
    return lax.dot_general(a, b, (((1,), (1,)), ((), ())), preferred_element_type=F32, precision=precision)


def _dot_tn(a, b, precision=None):
    return lax.dot_general(a, b, (((0,), (0,)), ((), ())), preferred_element_type=F32, precision=precision)


def _silu(x):
    return x * jax.nn.sigmoid(x)


def _softplus(x):
    return jnp.maximum(x, 0.0) + jnp.log1p(jnp.exp(-jnp.abs(x)))


def _log_sigmoid(x):
    return jnp.minimum(x, 0.0) - jnp.log1p(jnp.exp(-jnp.abs(x)))


def _gelu_tanh(x):
    return 0.5 * x * (1.0 + jnp.tanh(math.sqrt(2.0 / math.pi) * (x + 0.044715 * (x * x * x))))


def _rms(x, g):
    return x * lax.rsqrt(jnp.mean(x * x, axis=-1, keepdims=True) + EPS) * g


def _iota(shape, dim):
    return lax.broadcasted_iota(jnp.int32, shape, dim)


def _ada_body(c_ref, w_ref, b_ref, o_ref):
    s = _silu(c_ref[...]).astype(BF16)
    o_ref[...] = _dot(s, w_ref[...].astype(BF16)) + b_ref[...]


def _ada_mod(cond, w_ada, b_ada):
    depth = w_ada.shape[0]
    rows = cond.shape[0]
    tn = 1536
    return pl.pallas_call(
        _ada_body,
        out_shape=_sds((depth, rows, 6 * D_MODEL), F32),
        grid=(depth, 6 * D_MODEL // tn),
        in_specs=[pl.BlockSpec((rows, D_MODEL), lambda l, j: (0, 0)),
                  pl.BlockSpec((None, D_MODEL, tn), lambda l, j: (l, 0, j)),
                  pl.BlockSpec((None, 1, tn), lambda l, j: (l, 0, j))],
        out_specs=pl.BlockSpec((None, rows, tn), lambda l, j: (l, 0, j)),
        compiler_params=_cp("arbitrary", "arbitrary"),
        name="ada_mod",
    )(cond, w_ada, b_ada.reshape(depth, 1, 6 * D_MODEL))


def _mod_spec(rows, base, tiles_per_group, piece):
    return pl.BlockSpec((None, rows, D_MODEL), lambda i: (base + i // tiles_per_group, 0, piece))


def _inproj_body(*refs, n_in, rows_out):
    x_ref, g_ref, sh_ref, sc_ref, w_ref = refs[:5]
    (ua_ref, ub_ref, z_ref, xbc_ref, q_ref, qtb_ref, kb_ref, vtb_ref, kt_ref, vt_ref,
     dtf_ref, dtft_ref) = refs[n_in:n_in + 12]
    h = _rms(x_ref[...], g_ref[...]) * (1.0 + sc_ref[...]) + sh_ref[...]
    hb = h.astype(BF16)

    def mm(a, b):
        return _dot(hb, w_ref[:, a:b])

    o = 0
    u = mm(o, o + A_CH)
    ua_ref[...] = u[:, :A_CH // 2]
    ub_ref[...] = u[:, A_CH // 2:]
    o += A_CH
    z_ref[...] = mm(o, o + B_CH)
    o += B_CH
    xbc_ref[...] = mm(o, o + B_XBC)
    o += B_XBC
    q = mm(o, o + C_CH) * (C_HD ** -0.5)
    q_ref[...] = q.astype(BF16)
    qtb_ref[...] = (q * LOG2E).T.astype(BF16).reshape(qtb_ref.shape)
    o += C_CH
    k = mm(o, o + C_CH)
    kt_ref[...] = k.T
    kb_ref[...] = k.astype(BF16)
    o += C_CH
    v = mm(o, o + C_CH)
    vt = v.T
    vt_ref[...] = vt
    vtb_ref[...] = vt.astype(BF16).reshape(vtb_ref.shape)
    o += C_CH
    dtf = mm(o, o + PAD_W)
    dtf_ref[...] = dtf
    dtft_ref[...] = dtf.T[:16, :]
    if rows_out:
        k_ref, v_ref = refs[n_in + 12:n_in + 14]
        k_ref[...] = k
        v_ref[...] = v


def _in_proj(x, mod, mod_rows, mod_base, tiles_per_group, g_pre, w_all, tm, layer, depth, kv_all, rows_out):
    t = x.shape[0]
    groups = t // (tm * tiles_per_group)
    glen = tm * tiles_per_group
    row = lambda w: pl.BlockSpec((tm, w), lambda i: (i, 0))
    kv_spec = pl.BlockSpec((None, None, C_CH, tm),
                           lambda i: (layer, i // tiles_per_group, 0, i % tiles_per_group))
    pair_t = _sds((groups, C_HEADS // 2, tiles_per_group, 128, tm), BF16)
    pair_spec = pl.BlockSpec((None, C_HEADS // 2, None, 128, tm),
                             lambda i: (i // tiles_per_group, 0, i % tiles_per_group, 0, 0))
    outs = [_sds((t, A_CH // 2), F32), _sds((t, A_CH // 2), F32), _sds((t, B_CH), F32), _sds((t, B_XBC), F32),
            _sds((t, C_CH), BF16), pair_t, _sds((t, C_CH), BF16), pair_t,
            _sds((depth, groups, C_CH, glen), F32), _sds((depth, groups, C_CH, glen), F32),
            _sds((t, PAD_W), F32), _sds((16, t), F32)]
    out_specs = [row(A_CH // 2), row(A_CH // 2), row(B_CH), row(B_XBC), row(C_CH), pair_spec, row(C_CH),
                 pair_spec, kv_spec, kv_spec, row(PAD_W), pl.BlockSpec((16, tm), lambda i: (0, i))]
    if rows_out:
        outs += [_sds((t, C_CH), F32), _sds((t, C_CH), F32)]
        out_specs += [row(C_CH), row(C_CH)]
    in_specs = [row(D_MODEL),
                pl.BlockSpec((1, D_MODEL), lambda i: (0, 0)),
                _mod_spec(mod_rows, mod_base, tiles_per_group, 0),
                _mod_spec(mod_rows, mod_base, tiles_per_group, 1),
                pl.BlockSpec((D_MODEL, N_ALL), lambda i: (0, 0))]
    args = [x, g_pre, mod, mod, w_all]
    aliases = {}
    if kv_all is not None:
        in_specs += [pl.BlockSpec(memory_space=pl.ANY), pl.BlockSpec(memory_space=pl.ANY)]
        args += list(kv_all)
        aliases = {5: 8, 6: 9}
    return pl.pallas_call(
        functools.partial(_inproj_body, n_in=len(args), rows_out=rows_out),
        out_shape=outs,
        grid=(t // tm,),
        in_specs=in_specs,
        out_specs=out_specs,
        input_output_aliases=aliases,
        compiler_params=_cp("arbitrary"),
        name="in_proj",
    )(*args)


def _s5_param_body(ex_ref, ldt_ref, arr_ref, air_ref, arc_ref, aic_ref, btr_ref, bti_ref, ctr_ref, cti_ref,
                   krow_ref, etr_ref, eti_ref, abr_ref, abi_ref, bbr_ref, bbi_ref):
    dt = jnp.exp(ldt_ref[...])
    ar, ai = arr_ref[...], air_ref[...]
    mag = jnp.exp(ar * dt)
    abr, abi = mag * jnp.cos(ai * dt), mag * jnp.sin(ai * dt)
    den = ar * ar + ai * ai
    nr, ni = abr - 1.0, abi
    fr, fi = (nr * ar + ni * ai) / den, (ni * ar - nr * ai) / den
    btr, bti = btr_ref[...], bti_ref[...]
    bbr, bbi = fr * btr - fi * bti, fr * bti + fi * btr
    abr_ref[...] = abr
    abi_ref[...] = abi
    bbr_ref[...] = bbr
    bbi_ref[...] = bbi

    def epow(k, a_r, a_i):
        m = jnp.exp(k * a_r * dt)
        th = k * a_i * dt
        return m * jnp.cos(th), m * jnp.sin(th)

    er, ei = epow(ex_ref[...], ar, ai)
    etr_ref[...] = er
    eti_ref[...] = ei
    arc, aic = arc_ref[...], aic_ref[...]
    lag = jnp.right_shift(_iota((1, S5_ROW), 1), 4).astype(F32)
    elr, eli = epow(lag, arc, aic)
    ctr, cti = ctr_ref[...], cti_ref[...]
    bm_re, bm_im = ctr * elr - cti * eli, ctr * eli + cti * elr
    krow_ref[...] = _dot(bbr, bm_re, HI) - _dot(bbi, bm_im, HI)


def _s5_params(a_re, a_im, log_dt, b_re, b_im, c_re, c_im):
    depth = a_re.shape[0]
    steps = jnp.arange(S5_T, dtype=F32)
    ex = jnp.concatenate([S5_T - 1.0 - steps, steps + 1.0,
                          S5_T * 2.0 ** jnp.arange(S5_LEVELS, dtype=F32)]).reshape(S5_ROWS_E, 1)
    arr, air = a_re.reshape(depth, A_NG, 1, A_P), a_im.reshape(depth, A_NG, 1, A_P)
    arc, aic = a_re.reshape(depth, A_NG, A_P, 1), a_im.reshape(depth, A_NG, A_P, 1)
    ldt = log_dt.reshape(depth, A_NG, 1, 1)
    bt = lambda b: jnp.swapaxes(b, 2, 3)
    ct = lambda c: jnp.tile(jnp.swapaxes(c, 2, 3), (1, 1, 1, S5_T))
    blk = lambda r, c: pl.BlockSpec((None, None, r, c), lambda l, g: (l, g, 0, 0))
    outs = [_sds((depth, A_NG, A_GROUP, S5_ROW), F32),
            _sds((depth, A_NG, S5_ROWS_E, A_P), F32), _sds((depth, A_NG, S5_ROWS_E, A_P), F32),
            _sds((depth, A_NG, 1, A_P), F32), _sds((depth, A_NG, 1, A_P), F32),
            _sds((depth, A_NG, A_GROUP, A_P), F32), _sds((depth, A_NG, A_GROUP, A_P), F32)]
    out_specs = [blk(A_GROUP, S5_ROW), blk(S5_ROWS_E, A_P), blk(S5_ROWS_E, A_P), blk(1, A_P), blk(1, A_P),
                 blk(A_GROUP, A_P), blk(A_GROUP, A_P)]
    return pl.pallas_call(
        _s5_param_body,
        out_shape=outs,
        grid=(depth, A_NG),
        in_specs=[pl.BlockSpec((S5_ROWS_E, 1), lambda l, g: (0, 0)),
                  blk(1, 1), blk(1, A_P), blk(1, A_P), blk(A_P, 1), blk(A_P, 1),
                  blk(A_GROUP, A_P), blk(A_GROUP, A_P), blk(A_P, S5_ROW), blk(A_P, S5_ROW)],
        out_specs=out_specs,
        compiler_params=_cp("arbitrary", "arbitrary"),
        name="s5_params",
    )(ex, ldt, arr, air, arc, aic, bt(b_re), bt(b_im), ct(c_re), ct(c_im))


def _s5_prompt_body(ua_ref, ub_ref, kst_ref, bre_ref, bim_ref, cre_ref, cim_ref, etr_ref, eti_ref, d_ref,
                    ya_ref, yb_ref, hfr_ref, hfi_ref, ucat_scr, hr_scr, hi_scr, *, chunks, levels):
    def slab(ref, s):
        return ref[pl.ds(s, chunks, stride=S5_T), :]

    def local_state(s, carry):
        x = jnp.concatenate([slab(ua_ref, s), slab(ub_ref, s)], axis=1).astype(BF16)
        bur, bui = _dot(x, bre_ref[...]), _dot(x, bim_ref[...])
        er, ei = etr_ref[pl.ds(s, 1), :], eti_ref[pl.ds(s, 1), :]
        hr_scr[...] += er * bur - ei * bui
        hi_scr[...] += er * bui + ei * bur
        return carry

    hr_scr[...] = jnp.zeros_like(hr_scr)
    hi_scr[...] = jnp.zeros_like(hi_scr)
    lax.fori_loop(0, S5_T, local_state, 0)
    for s in range(S5_T):
        ucat_scr[:, A_CH * s:A_CH * s + A_CH // 2] = slab(ua_ref, s).astype(BF16)
        ucat_scr[:, A_CH * s + A_CH // 2:A_CH * (s + 1)] = slab(ub_ref, s).astype(BF16)
    hr, hi = hr_scr[...], hi_scr[...]
    cidx = _iota((chunks, 1), 0)
    for k in range(levels):
        d = 1 << k
        keep = cidx >= d
        sr = jnp.where(keep, pltpu.roll(hr, d, axis=0), 0.0)
        si = jnp.where(keep, pltpu.roll(hi, d, axis=0), 0.0)
        lr, li = etr_ref[2 * S5_T + k:2 * S5_T + k + 1, :], eti_ref[2 * S5_T + k:2 * S5_T + k + 1, :]
        hr, hi = hr + lr * sr - li * si, hi + lr * si + li * sr
    hfr_ref[...] = hr[chunks - 1:chunks, :]
    hfi_ref[...] = hi[chunks - 1:chunks, :]
    keep = cidx >= 1
    hr_scr[...] = jnp.where(keep, pltpu.roll(hr, 1, axis=0), 0.0)
    hi_scr[...] = jnp.where(keep, pltpu.roll(hi, 1, axis=0), 0.0)
    half = A_CH // 2

    def outputs(t, span):
        er, ei = etr_ref[pl.ds(S5_T + t, 1), :], eti_ref[pl.ds(S5_T + t, 1), :]
        pr, pi = hr_scr[...], hi_scr[...]
        gr = (er * pr - ei * pi).astype(BF16)
        gi = (er * pi + ei * pr).astype(BF16)
        first = pl.multiple_of((S5_T - 1 - t) * A_CH, A_CH)
        y = (_dot(ucat_scr[:, :A_CH * span], kst_ref[pl.ds(first, A_CH * span), :])
             + _dot(gr, cre_ref[...]) - _dot(gi, cim_ref[...]))
        dd = d_ref[...]
        ya_ref[pl.ds(t, chunks, stride=S5_T), :] = y[:, :half] + dd[:, :half] * slab(ua_ref, t)
        yb_ref[pl.ds(t, chunks, stride=S5_T), :] = y[:, half:] + dd[:, half:] * slab(ub_ref, t)

    def first_half(t, carry):
        outputs(t, S5_T // 2)
        return carry

    def second_half(t, carry):
        outputs(t, S5_T)
        return carry

    lax.fori_loop(0, S5_T // 2, first_half, 0)
    lax.fori_loop(S5_T // 2, S5_T, second_half, 0)


def _s5_prompt(ua, ub, kstack, bre, bim, cre, cim, etr, eti, d, n, length):
    chunks = length // S5_T
    levels = max(1, (chunks - 1).bit_length())
    assert chunks & (chunks - 1) == 0 and levels <= S5_LEVELS
    half = A_CH // 2
    w = A_NG * A_P
    tok = pl.BlockSpec((length, half), lambda i: (i, 0))
    const = lambda r, c: pl.BlockSpec((r, c), lambda i: (0, 0))
    fin = pl.BlockSpec((None, 1, w), lambda i: (i, 0, 0))
    return pl.pallas_call(
        functools.partial(_s5_prompt_body, chunks=chunks, levels=levels),
        out_shape=[_sds((n * length, half), F32), _sds((n * length, half), F32),
                   _sds((n, 1, w), F32), _sds((n, 1, w), F32)],
        grid=(n,),
        in_specs=[tok, tok, const(2 * S5_T * A_CH, A_CH), const(A_CH, w), const(A_CH, w), const(w, A_CH),
                  const(w, A_CH), const(S5_ROWS_E, w), const(S5_ROWS_E, w), const(1, A_CH)],
        out_specs=[tok, tok, fin, fin],
        scratch_shapes=[pltpu.VMEM((chunks, S5_T * A_CH), BF16), pltpu.VMEM((chunks, w), F32),
                        pltpu.VMEM((chunks, w), F32)],
        compiler_params=_cp("arbitrary"),
        name="s5_prompt",
    )(ua, ub, kstack, bre, bim, cre, cim, etr, eti, d)


def _s5_sample_body(ua_ref, ub_ref, hr_ref, hi_ref, abr_ref, abi_ref, bre_ref, bim_ref, cre_ref, cim_ref,
                    d_ref, y_ref, sr_ref, si_ref):
    u = jnp.concatenate([ua_ref[...], ub_ref[...]], axis=1)
    ub = u.astype(BF16)
    hr, hi = hr_ref[...], hi_ref[...]
    abr, abi = abr_ref[...], abi_ref[...]
    sr = abr * hr - abi * hi + _dot(ub, bre_ref[...])
    si = abr * hi + abi * hr + _dot(ub, bim_ref[...])
    sr_ref[...] = sr
    si_ref[...] = si
    y_ref[...] = _dot(sr.astype(BF16), cre_ref[...]) - _dot(si.astype(BF16), cim_ref[...]) + d_ref[...] * u


def _s5_sample(ua, ub, hr, hi, abr, abi, bre, bim, cre, cim, d):
    b = ua.shape[0]
    w = A_NG * A_P
    return pl.pallas_call(
        _s5_sample_body,
        out_shape=[_sds((b, A_CH), F32), _sds((b, w), F32), _sds((b, w), F32)],
        compiler_params=pltpu.CompilerParams(vmem_limit_bytes=VMEM_LIMIT),
        name="s5_sample",
    )(ua, ub, hr, hi, abr, abi, bre, bim, cre, cim, d)


def _ssd_prompt_body(xbc_ref, z_ref, dtc_ref, dtr_ref, cprev_ref, h0_ref, cw_ref, cb_ref,
                     dtbc_ref, dtbr_ref, alc_ref, alr_ref, dsk_ref, ng_ref,
                     y_ref, hf_ref, ext_scr, h_scr):
    c = pl.program_id(1)
    q = B_CHUNK

    @pl.when(c == 0)
    def _():
        ext_scr[0:8, :] = cprev_ref[...]
        h_scr[...] = h0_ref[...]

    x = xbc_ref[...]
    ext_scr[8:8 + q, :] = x
    cw = cw_ref[...]
    conv = (cb_ref[...] + cw[3:4] * x + cw[2:3] * ext_scr[7:7 + q, :]
            + cw[1:2] * ext_scr[6:6 + q, :] + cw[0:1] * ext_scr[5:5 + q, :])
    ext_scr[0:8, :] = x[q - 8:q, :]
    xc = _silu(conv)

    r_i, c_i = _iota((q, q), 0), _iota((q, q), 1)
    causal = r_i >= c_i
    dtc = _softplus(dtc_ref[...] + dtbc_ref[...])
    da_c = dtc * (-jnp.exp(alc_ref[...]))
    acc_c = _dot(causal.astype(F32), da_c, HI)
    dtr = _softplus(dtr_ref[...] + dtbr_ref[...])
    da_r = dtr * (-jnp.exp(alr_ref[...]))
    acc_r = _dot(da_r, (r_i <= c_i).astype(F32), HI)

    bm = [xc[:, B_CH + B_N * g:B_CH + B_N * (g + 1)].astype(BF16) for g in range(B_NG)]
    cm = [xc[:, B_CH + B_N * (B_NG + g):B_CH + B_N * (B_NG + g + 1)].astype(BF16) for g in range(B_NG)]
    cbm = [_dot_nt(cm[g], bm[g]) for g in range(B_NG)]
    lane = _iota((q, 128), 1)
    heads_per_group = B_HEADS // B_NG
    ys = []
    for j in range(B_HEADS // 2):
        xs_pair = xc[:, 128 * j:128 * (j + 1)]
        dt_pair = jnp.where(lane < B_HD, dtc[:, 2 * j:2 * j + 1], dtc[:, 2 * j + 1:2 * j + 2])
        xdt = xs_pair * dt_pair
        xdt_b = xdt.astype(BF16)
        hprev_b = h_scr[2 * j:2 * j + 2].reshape(2 * B_HD, B_N).astype(BF16)
        outs = []
        for e in range(2):
            h = 2 * j + e
            g = h // heads_per_group
            ac = acc_c[:, h:h + 1]
            ar = acc_r[h:h + 1, :]
            lm = jnp.exp(jnp.where(causal, ac - ar, -jnp.inf))
            yd = _dot((cbm[g] * lm).astype(BF16), xdt_b)
            yo = _dot_nt(cm[g], hprev_b) * jnp.exp(ac)
            outs.append(yd + yo)
            alast = acc_r[h:h + 1, q - 1:q]
            st = _dot_tn((xdt * jnp.exp(alast - ac)).astype(BF16), bm[g])
            h_scr[h] = jnp.exp(alast) * h_scr[h] + st[B_HD * e:B_HD * (e + 1), :]
        ys.append(jnp.where(lane < B_HD, outs[0], outs[1]) + dsk_ref[:, 128 * j:128 * (j + 1)] * xs_pair)
    y = jnp.concatenate(ys, axis=1)
    y_ref[...] = _rms(y * _silu(z_ref[...]), ng_ref[...]).astype(y_ref.dtype)

    @pl.when(c == pl.num_programs(1) - 1)
    def _():
        hf_ref[...] = h_scr[...]


def _ssd_prompt(xbc, z, dtf, dtft, cprev, h0, cw, cb, dtb_c, dtb_r, al_c, al_r, dsk, ng, n, length):
    nc = length // B_CHUNK
    q = B_CHUNK
    tok = lambda w: pl.BlockSpec((q, w), lambda i, c: (i * nc + c, 0))
    const = lambda r, w: pl.BlockSpec((r, w), lambda i, c: (0, 0))
    return pl.pallas_call(
        _ssd_prompt_body,
        out_shape=[_sds((n * length, B_CH), BF16), _sds((n, B_HEADS, B_HD, B_N), F32)],
        grid=(n, nc),
        in_specs=[tok(B_XBC), tok(B_CH), tok(PAD_W),
                  pl.BlockSpec((16, q), lambda i, c: (0, i * nc + c)),
                  pl.BlockSpec((None, 8, B_XBC), lambda i, c: (i, 0, 0)),
                  pl.BlockSpec((None, B_HEADS, B_HD, B_N), lambda i, c: (i, 0, 0, 0)),
                  const(B_CONV, B_XBC), const(1, B_XBC), const(1, PAD_W), const(16, 1),
                  const(1, PAD_W), const(16, 1), const(1, B_CH), const(1, B_CH)],
        out_specs=[tok(B_CH), pl.BlockSpec((None, B_HEADS, B_HD, B_N), lambda i, c: (i, 0, 0, 0))],
        scratch_shapes=[pltpu.VMEM((8 + q, B_XBC), F32), pltpu.VMEM((B_HEADS, B_HD, B_N), F32)],
        compiler_params=_cp("arbitrary", "arbitrary"),
        name="ssd_prompt",
    )(xbc, z, dtf, dtft, cprev, h0, cw, cb, dtb_c, dtb_r, al_c, al_r, dsk, ng)


def _ssd_sample_body(xbc_ref, p0_ref, p1_ref, p2_ref, z_ref, dtf_ref, h0_ref, cw_ref, cb_ref,
                     dtb_ref, al_ref, ex_ref, dsk_ref, ng_ref, *rest, tb):
    y_ref, hn_ref, y_scr = rest[-3:]
    x = xbc_ref[...]
    cw = cw_ref[...]
    conv = cb_ref[...] + cw[0:1] * p0_ref[...] + cw[1:2] * p1_ref[...] + cw[2:3] * p2_ref[...] + cw[3:4] * x
    xc = _silu(conv)
    dt = _softplus(dtf_ref[...] + dtb_ref[...])
    e = jnp.exp(dt * (-jnp.exp(al_ref[...])))
    ex = ex_ref[...]
    dt_e = _dot(dt, ex, HI)
    e_e = _dot(e, ex, HI)
    xs = xc[:, :B_CH]
    xdt = xs * dt_e
    half = B_CH // B_NG
    row8, lane8 = _iota((8, B_CH), 0), _iota((8, B_CH), 1)
    r8 = _iota((8, B_N), 0)
    lane1 = _iota((1, B_CH), 1)
    for b in range(tb):
        xr = xdt[b:b + 1, :]
        lhs = jnp.where(row8 == 0, e_e[b:b + 1, :],
                        jnp.where(((row8 == 1) & (lane8 < half)) | ((row8 == 2) & (lane8 >= half)), xr, 0.0))
        rhs_e = jnp.where(r8 == 0, 1.0, 0.0)
        rhs_s = jnp.where(r8 == 1, xc[b:b + 1, B_CH:B_CH + B_N],
                          jnp.where(r8 == 2, xc[b:b + 1, B_CH + B_N:B_CH + 2 * B_N], 0.0))
        hn = _dot_tn(lhs, rhs_e, HI) * h0_ref[b] + _dot_tn(lhs, rhs_s, HI)
        hn_ref[b] = hn
        cmat = jnp.where(r8 == 0, xc[b:b + 1, B_CH + 2 * B_N:B_CH + 3 * B_N],
                         jnp.where(r8 == 1, xc[b:b + 1, B_CH + 3 * B_N:B_CH + 4 * B_N], 0.0))
        yt = _dot_nt(cmat.astype(BF16), hn.astype(BF16))
        y_scr[b:b + 1, :] = jnp.where(lane1 < half, yt[0:1, :], yt[1:2, :])
    y = y_scr[...] + dsk_ref[...] * xs
    y_ref[...] = _rms(y * _silu(z_ref[...]), ng_ref[...]).astype(y_ref.dtype)


def _ssd_sample(xbc, conv_t, z, dtf, ssm_in, ssm_out, layer, cw, cb, dtb, al, ex, dsk, ng):
    depth, b = ssm_in.shape[0], xbc.shape[0]
    tb = 8
    row = lambda w: pl.BlockSpec((tb, w), lambda i: (i, 0))
    const = lambda r, w: pl.BlockSpec((r, w), lambda i: (0, 0))
    prev = lambda k: pl.BlockSpec((None, None, tb, B_XBC), lambda i: (layer, k, i, 0))
    st = pl.BlockSpec((None, tb, B_CH, B_N), lambda i: (layer, i, 0, 0))
    in_specs = [row(B_XBC), prev(0), prev(1), prev(2), row(B_CH), row(PAD_W), st,
                const(B_CONV, B_XBC), const(1, B_XBC), const(1, PAD_W), const(1, PAD_W),
                const(PAD_W, B_CH), const(1, B_CH), const(1, B_CH)]
    args = [xbc, conv_t, conv_t, conv_t, z, dtf, ssm_in, cw, cb, dtb, al, ex, dsk, ng]
    aliases = {}
    if ssm_out is not None:
        in_specs.append(pl.BlockSpec(memory_space=pl.ANY))
        args.append(ssm_out)
        aliases = {len(args) - 1: 1}
    return pl.pallas_call(
        functools.partial(_ssd_sample_body, tb=tb),
        out_shape=[_sds((b, B_CH), BF16), _sds((depth, b, B_CH, B_N), F32)],
        grid=(b // tb,),
        in_specs=in_specs,
        out_specs=[row(B_CH), st],
        scratch_shapes=[pltpu.VMEM((tb, B_CH), F32)],
        input_output_aliases=aliases,
        compiler_params=_cp("arbitrary"),
        name="ssd_sample",
    )(*args)


def _split3(x):
    hi = x.astype(BF16)
    r1 = x - hi.astype(F32)
    mid = r1.astype(BF16)
    lo = (r1 - mid.astype(F32)).astype(BF16)
    return hi, mid, lo


def _fox_prep_body(fc_ref, fr_ref, bfr_ref, bfc_ref, place_ref, ones_ref, lfr_ref, kx_ref, cr_ref,
                   carry_c, carry_r):
    j = pl.program_id(1)
    tl = fc_ref.shape[0]

    @pl.when(j == 0)
    def _():
        carry_c[...] = jnp.zeros_like(carry_c)
        carry_r[...] = jnp.zeros_like(carry_r)

    r_i, c_i = _iota((tl, tl), 0), _iota((tl, tl), 1)
    lfc = _log_sigmoid(fc_ref[...] + bfr_ref[...])
    cc = _dot((r_i >= c_i).astype(F32), lfc, HI) + carry_c[0:1, :]
    carry_c[...] = jnp.broadcast_to(cc[tl - 1:tl, :], carry_c.shape)
    parts = _split3(-LOG2E * cc)
    kx = ones_ref[...] + sum(_dot(parts[i], place_ref[i]) for i in range(3))
    kx_ref[...] = kx.astype(BF16)
    lfr = _log_sigmoid(fr_ref[...] + bfc_ref[...])
    lfr_ref[...] = lfr
    cr = _dot(lfr, (r_i <= c_i).astype(F32), HI) + carry_r[:, 0:1]
    cr_ref[...] = cr
    carry_r[...] = jnp.broadcast_to(cr[:, tl - 1:tl], carry_r.shape)


def _bias_slot(pair, head, term):
    return 16 * pair + 3 * head + term


def _fox_prep(dtf, dtft, bf_row, bf_col, n, length, tl):
    nt = length // tl
    t = n * length
    place = jnp.zeros((3, PAD_W, 128), F32)
    ones = jnp.zeros((1, 128), F32)
    for pair in range(C_HEADS // 2):
        for term in range(3):
            ones = ones.at[0, _bias_slot(pair, 2, term)].set(1.0)
            for head in range(2):
                place = place.at[term, F_OFF + 2 * pair + head, _bias_slot(pair, head, term)].set(1.0)
    tok = pl.BlockSpec((tl, PAD_W), lambda i, j: (i * nt + j, 0))
    rowt = pl.BlockSpec((None, 16, tl), lambda i, j: (i * nt + j, 0, 0))
    return pl.pallas_call(
        _fox_prep_body,
        out_shape=[_sds((n * nt, 16, tl), F32), _sds((t, 128), BF16), _sds((n * nt, 16, tl), F32)],
        grid=(n, nt),
        in_specs=[tok, pl.BlockSpec((16, tl), lambda i, j: (0, i * nt + j)),
                  pl.BlockSpec((1, PAD_W), lambda i, j: (0, 0)), pl.BlockSpec((16, 1), lambda i, j: (0, 0)),
                  pl.BlockSpec((3, PAD_W, 128), lambda i, j: (0, 0, 0)),
                  pl.BlockSpec((1, 128), lambda i, j: (0, 0))],
        out_specs=[rowt, pl.BlockSpec((tl, 128), lambda i, j: (i * nt + j, 0)), rowt],
        scratch_shapes=[pltpu.VMEM((8, PAD_W), F32), pltpu.VMEM((16, 128), F32)],
        compiler_params=_cp("arbitrary", "arbitrary"),
        name="fox_prep",
    )(dtf, dtft, bf_row, bf_col, place.astype(BF16), ones)


def _fox_flash_body(qt_ref, k_ref, kx_ref, vt_ref, cr_ref, o_ref, qa_scr, m_scr, l_scr, acc_scr):
    hp, qi = pl.program_id(1), pl.program_id(2)
    tq = qt_ref.shape[1]
    tk = tq
    row = _iota((128, tq), 0)
    qt = qt_ref[...]
    base = 16 * hp
    for e in range(2):
        cq = _split3(LOG2E * cr_ref[pl.ds(F_OFF + 2 * hp + e, 1), :])
        qx = jnp.where((row >= base + 3 * e) & (row < base + 3 * e + 3), 1.0, 0.0).astype(BF16)
        for term in range(3):
            qx = jnp.where(row == base + 6 + term, cq[term], qx)
        qa_scr[e, 0:128, :] = jnp.where((row < C_HD) == (e == 0), qt, jnp.zeros_like(qt))
        qa_scr[e, 128:256, :] = qx
    m_scr[...] = jnp.full(m_scr.shape, -jnp.inf, F32)
    l_scr[...] = jnp.zeros_like(l_scr)
    acc_scr[...] = jnp.zeros_like(acc_scr)

    def block(ki, diagonal):
        rows = pl.ds(pl.multiple_of(ki * tk, tk), tk)
        ka = jnp.concatenate([k_ref[rows, :], kx_ref[rows, :]], axis=1)
        vt = vt_ref[ki]
        for e in range(2):
            s = _dot(ka, qa_scr[e])
            if diagonal:
                s = jnp.where(_iota((tk, tq), 0) <= _iota((tk, tq), 1), s, -jnp.inf)
            m_old = m_scr[e]
            m_new = jnp.maximum(m_old, jnp.max(s, axis=0, keepdims=True))
            alpha = jnp.exp2(m_old - m_new)
            p = jnp.exp2(s - m_new)
            l_scr[e] = alpha * l_scr[e] + jnp.sum(p, axis=0, keepdims=True)
            acc_scr[e] = alpha * acc_scr[e] + _dot(vt, p.astype(BF16))
            m_scr[e] = m_new

    def below_diagonal(ki, carry):
        block(ki, False)
        return carry

    lax.fori_loop(0, qi, below_diagonal, 0)
    block(qi, True)
    ot = jnp.where(row < C_HD, acc_scr[0] / l_scr[0], acc_scr[1] / l_scr[1])
    o_ref[...] = ot.T.astype(o_ref.dtype)


def _fox_flash(qt, k, kx, vt, cr, n, length, tq):
    nq = length // tq
    return pl.pallas_call(
        _fox_flash_body,
        out_shape=_sds((n * length, C_CH), BF16),
        grid=(n, C_HEADS // 2, nq),
        in_specs=[pl.BlockSpec((None, None, None, 128, tq), lambda i, h, a: (i, h, a, 0, 0)),
                  pl.BlockSpec((length, 128), lambda i, h, a: (i, h)),
                  pl.BlockSpec((length, 128), lambda i, h, a: (i, 0)),
                  pl.BlockSpec((None, None, nq, 128, tq), lambda i, h, a: (i, h, 0, 0, 0)),
                  pl.BlockSpec((None, 16, tq), lambda i, h, a: (i * nq + a, 0, 0))],
        out_specs=pl.BlockSpec((tq, 128), lambda i, h, a: (i * nq + a, h)),
        scratch_shapes=[pltpu.VMEM((2, 256, tq), BF16), pltpu.VMEM((2, 1, tq), F32),
                        pltpu.VMEM((2, 1, tq), F32), pltpu.VMEM((2, 128, tq), F32)],
        compiler_params=_cp("arbitrary", "arbitrary", "arbitrary"),
        name="fox_flash",
    )(qt, k, kx, vt, cr)


def _fox_sample_body(pt_ref, q_ref, kn_ref, vn_ref, fn_ref, bf_ref, *refs, pages):
    k_refs = refs[0:pages]
    v_refs = refs[pages:2 * pages]
    f_refs = refs[2 * pages:3 * pages]
    o_ref, lf_ref, lf_scr = refs[3 * pages:3 * pages + 3]
    i = pl.program_id(0)
    row, lane = _iota((8, C_CH), 0), _iota((8, C_CH), 1)
    own = jnp.right_shift(lane, 6) == row
    qm = jnp.where(own, q_ref[...].astype(F32), 0.0)
    qb = qm.astype(BF16)
    lfn = _log_sigmoid(fn_ref[...] + bf_ref[...])
    lf_ref[...] = lfn
    carry = jnp.sum(jnp.where(_iota((8, PAD_W), 1) == F_OFF + _iota((8, PAD_W), 0), lfn, 0.0),
                    axis=1, keepdims=True)
    for p in range(pages):
        r = jnp.bitwise_and(pt_ref[i, p], 7)
        for h in range(C_HEADS):
            lf_scr[8 * p + h:8 * p + h + 1, :] = f_refs[p][h, pl.ds(r, 1), :]
        lf_scr[8 * p + C_HEADS:8 * p + 8, :] = jnp.zeros((8 - C_HEADS, PAGE), F32)
    lf_all = lf_scr[...]
    later = (_iota((PAGE, PAGE), 0) > _iota((PAGE, PAGE), 1)).astype(F32)
    suffix = _dot(lf_all, later, HI)
    total = jnp.sum(lf_all, axis=1, keepdims=True)
    s = [None] * pages
    for p in reversed(range(pages)):
        kt = k_refs[p][...].reshape(C_CH, PAGE).astype(BF16)
        s[p] = _dot(qb, kt) + suffix[8 * p:8 * p + 8, :] + carry
        carry = carry + total[8 * p:8 * p + 8, :]
    s_new = jnp.sum(qm * kn_ref[...], axis=1, keepdims=True)
    m = s_new
    for p in range(pages):
        m = jnp.maximum(m, jnp.max(s[p], axis=1, keepdims=True))
    p_new = jnp.exp(s_new - m)
    l = p_new
    acc = p_new * vn_ref[...]
    for p in range(pages):
        pr = jnp.exp(s[p] - m)
        l = l + jnp.sum(pr, axis=1, keepdims=True)
        acc = acc + _dot_nt(pr.astype(BF16), v_refs[p][...].reshape(C_CH, PAGE).astype(BF16))
    o_ref[...] = jnp.sum(jnp.where(own, acc / l, 0.0), axis=0, keepdims=True)


def _fox_sample(page_table, layer, q, k_new, v_new, f_new, bf_row, cache_kt, cache_vt, cache_lft):
    b = q.shape[0]
    pages = page_table.shape[1]
    tok = lambda w: pl.BlockSpec((None, 1, w), lambda i, pt: (i, 0, 0))
    page = lambda p: pl.BlockSpec((None, None, C_HEADS, C_HD, PAGE), lambda i, pt: (layer, pt[i, p], 0, 0, 0))
    lfpage = lambda p: pl.BlockSpec((None, C_HEADS, 8, PAGE), lambda i, pt: (layer, 0, pt[i, p] // 8, 0))
    in_specs = ([tok(C_CH), tok(C_CH), tok(C_CH), tok(PAD_W), pl.BlockSpec((1, PAD_W), lambda i, pt: (0, 0))]
                + [page(p) for p in range(pages)] + [page(p) for p in range(pages)]
                + [lfpage(p) for p in range(pages)])
    grid_spec = pltpu.PrefetchScalarGridSpec(
        num_scalar_prefetch=1, grid=(b,), in_specs=in_specs,
        out_specs=[tok(C_CH), tok(PAD_W)],
        scratch_shapes=[pltpu.VMEM((8 * pages, PAGE), F32)])
    return pl.pallas_call(
        functools.partial(_fox_sample_body, pages=pages),
        out_shape=[_sds((b, 1, C_CH), F32), _sds((b, 1, PAD_W), F32)],
        grid_spec=grid_spec,
        compiler_params=_cp("arbitrary"),
        name="fox_sample",
    )(page_table, q.reshape(b, 1, C_CH), k_new.reshape(b, 1, C_CH), v_new.reshape(b, 1, C_CH),
      f_new.reshape(b, 1, PAD_W), bf_row,
      *([cache_kt] * pages), *([cache_vt] * pages), *([cache_lft] * pages))


def _outproj_body(x_ref, ya0_ref, ya1_ref, yb_ref, yc_ref, wg_ref, bg_ref, wo_ref, gp_ref, g1_ref, o_ref):
    y = _gelu_tanh(jnp.concatenate([ya0_ref[...], ya1_ref[...]], axis=1))
    gate = jax.nn.sigmoid(_dot(y.astype(BF16), wg_ref[...]) + bg_ref[...])
    ya = (y * gate).astype(BF16)
    mix = (_dot(ya, wo_ref[0:A_CH, :]) + _dot(yb_ref[...], wo_ref[A_CH:A_CH + B_CH, :])
           + _dot(yc_ref[...], wo_ref[A_CH + B_CH:, :]))
    o_ref[...] = x_ref[...] + g1_ref[...] * _rms(mix, gp_ref[...])


def _out_proj(x, ya0, ya1, yb, yc, w_glu, b_glu, w_out, g_post, mod, mod_rows, mod_base, tiles_per_group, tm):
    t = x.shape[0]
    row = lambda w: pl.BlockSpec((tm, w), lambda i: (i, 0))
    const = lambda r, w: pl.BlockSpec((r, w), lambda i: (0, 0))
    return pl.pallas_call(
        _outproj_body,
        out_shape=_sds((t, D_MODEL), F32),
        grid=(t // tm,),
        in_specs=[row(D_MODEL), row(A_CH // 2), row(A_CH // 2), row(B_CH), row(C_CH), const(A_CH, A_CH),
                  const(1, A_CH), const(D_MODEL, D_MODEL), const(1, D_MODEL),
                  _mod_spec(mod_rows, mod_base, tiles_per_group, 2)],
        out_specs=row(D_MODEL),
        compiler_params=_cp("arbitrary"),
        name="out_proj",
    )(x, ya0, ya1, yb, yc, w_glu, b_glu, w_out, g_post, mod)


def _ffn_body(x_ref, gpre_ref, sh_ref, sc_ref, w1_ref, w2_ref, gpost_ref, g2_ref, o_ref, *, tf):
    x = x_ref[...]
    hb = (_rms(x, gpre_ref[...]) * (1.0 + sc_ref[...]) + sh_ref[...]).astype(BF16)
    acc = jnp.zeros(x.shape, F32)
    for j in range(D_FF // tf):
        a = jnp.maximum(_dot(hb, w1_ref[:, j * tf:(j + 1) * tf]), 0.0)
        acc = acc + _dot((a * a).astype(BF16), w2_ref[j * tf:(j + 1) * tf, :])
    o_ref[...] = x + g2_ref[...] * _rms(acc, gpost_ref[...])


def _ffn(x, g_pre, w1, w2, g_post, mod, mod_rows, mod_base, tiles_per_group, tm):
    t = x.shape[0]
    row = pl.BlockSpec((tm, D_MODEL), lambda i: (i, 0))
    const = lambda r, w: pl.BlockSpec((r, w), lambda i: (0, 0))
    weight = lambda r, w: pl.BlockSpec((r, w), lambda i: (0, 0), pipeline_mode=pl.Buffered(1))
    return pl.pallas_call(
        functools.partial(_ffn_body, tf=1024),
        out_shape=_sds((t, D_MODEL), F32),
        grid=(t // tm,),
        in_specs=[row, const(1, D_MODEL), _mod_spec(mod_rows, mod_base, tiles_per_group, 3),
                  _mod_spec(mod_rows, mod_base, tiles_per_group, 4), weight(D_MODEL, D_FF), weight(D_FF, D_MODEL),
                  const(1, D_MODEL), _mod_spec(mod_rows, mod_base, tiles_per_group, 5)],
        out_specs=row,
        compiler_params=_cp("arbitrary"),
        name="ffn",
    )(x, g_pre, mod, mod, w1, w2, g_post, mod)


def _pad_lanes(v, offset, width=PAD_W):
    return jnp.zeros((1, width), F32).at[0, offset:offset + v.shape[0]].set(v)


def _pad_rows(v, offset, rows=16):
    return jnp.zeros((rows, 1), F32).at[offset:offset + v.shape[0], 0].set(v)


def kernel(x_prompt, x_sample, c_prompt, c_sample, cache_k, cache_v, cache_logf, page_table, state_s5_re, state_s5_im, state_conv, state_ssm, w_ada, b_ada, g_pre_mix, g_post_mix, g_pre_ffn, g_post_ffn, w_in, w_out, s5_a_re, s5_a_im, s5_log_dt, s5_b_re, s5_b_im, s5_c_re, s5_c_im, s5_d, s5_w_glu, s5_b_glu, ssd_conv_w, ssd_conv_b, ssd_dt_bias, ssd_a_log, ssd_d, ssd_norm_g, fox_b_f, w_ff1, w_ff2):
    depth = w_ada.shape[0]
    n, length, _ = x_prompt.shape
    nb = x_sample.shape[0]
    t = n * length
    n_pool = cache_k.shape[1]
    chunks = length // S5_T
    tm_p = 512
    assert x_sample.shape[1] == 1 and length % tm_p == 0 and nb % 8 == 0

    rows = -(-(n + nb) // 8) * 8
    cond = jnp.concatenate([c_prompt, c_sample, jnp.zeros((rows - n - nb, D_MODEL), F32)], axis=0)
    mod = _ada_mod(cond, w_ada, b_ada)
    mod_p = mod[:, :n].reshape(depth * n, 1, 6 * D_MODEL)
    mod_s = mod[:, n:n + nb]

    s5krow, s5etr, s5eti, s5abr, s5abi, s5bbr, s5bbi = _s5_params(
        s5_a_re, s5_a_im, s5_log_dt, s5_b_re, s5_b_im, s5_c_re, s5_c_im)
    eye_g = jnp.eye(A_NG, dtype=F32)
    w_s5 = A_NG * A_P
    expand = (jnp.arange(PAD_W)[:, None] == jnp.arange(B_CH)[None, :] // B_HD).astype(F32)

    ck = cache_k.transpose(0, 1, 3, 4, 2)
    cv = cache_v.transpose(0, 1, 3, 4, 2)
    clf = cache_logf.transpose(0, 3, 1, 2)
    conv_t = state_conv.transpose(0, 2, 1, 3)
    ssm_in = state_ssm.reshape(depth, nb, B_CH, B_N)

    xp = x_prompt.reshape(t, D_MODEL)
    xs = x_sample.reshape(nb, D_MODEL)
    zeros_conv = jnp.zeros((n, 8, B_XBC), F32)
    zeros_ssm = jnp.zeros((n, B_HEADS, B_HD, B_N), F32)
    p_out = [[] for _ in range(7)]
    s_out = [[] for _ in range(7)]
    kv_all = None
    ssm_all = None

    for l in range(depth):
        wi = w_in[l]
        o_dt = A_CH + B_CH + B_XBC
        o_f = o_dt + B_HEADS + 3 * C_CH
        w_all = jnp.concatenate(
            [wi[:, :o_dt], wi[:, o_dt + B_HEADS:o_f], wi[:, o_dt:o_dt + B_HEADS], wi[:, o_f:],
             jnp.zeros((D_MODEL, PAD_W - B_HEADS - C_HEADS), F32)], axis=1).astype(BF16)
        w_o = w_out[l].astype(BF16)
        w_g = s5_w_glu[l].astype(BF16)
        w1 = w_ff1[l].astype(BF16)
        w2 = w_ff2[l].astype(BF16)
        row = lambda v: v.reshape(1, -1)
        g_pm, g_qm, g_pf, g_qf = row(g_pre_mix[l]), row(g_post_mix[l]), row(g_pre_ffn[l]), row(g_post_ffn[l])
        b_g = row(s5_b_glu[l])
        cw, cb = ssd_conv_w[l], row(ssd_conv_b[l])
        dtb_c, dtb_r = _pad_lanes(ssd_dt_bias[l], 0), _pad_rows(ssd_dt_bias[l], 0)
        al_c, al_r = _pad_lanes(ssd_a_log[l], 0), _pad_rows(ssd_a_log[l], 0)
        dsk = row(jnp.repeat(ssd_d[l], B_HD))
        ng = row(ssd_norm_g[l])
        bf_row, bf_col = _pad_lanes(fox_b_f[l], F_OFF), _pad_rows(fox_b_f[l], F_OFF)
        d_row = row(s5_d[l])
        bre = jnp.einsum("gcp,gh->gchp", s5bbr[l], eye_g).reshape(A_CH, w_s5).astype(BF16)
        bim = jnp.einsum("gcp,gh->gchp", s5bbi[l], eye_g).reshape(A_CH, w_s5).astype(BF16)
        cre = jnp.einsum("gcp,gh->gphc", s5_c_re[l], eye_g).reshape(w_s5, A_CH).astype(BF16)
        cim = jnp.einsum("gcp,gh->gphc", s5_c_im[l], eye_g).reshape(w_s5, A_CH).astype(BF16)
        klag = jnp.einsum("galc,gh->lgahc", s5krow[l].reshape(A_NG, A_GROUP, S5_T, A_GROUP), eye_g)
        kstack = jnp.concatenate([klag[::-1].reshape(S5_T * A_CH, A_CH), jnp.zeros((S5_T * A_CH, A_CH), F32)],
                                 axis=0).astype(BF16)
        etr = s5etr[l].transpose(1, 0, 2).reshape(S5_ROWS_E, w_s5)
        eti = s5eti[l].transpose(1, 0, 2).reshape(S5_ROWS_E, w_s5)

        tiles = length // tm_p
        ua, ub, z, xbc, _, qtb, kb, vtb, kt_all, vt_all, dtf, dtft = _in_proj(
            xp, mod_p, 1, l * n, tiles, g_pm, w_all, tm_p, l, depth, kv_all, False)
        kv_all = (kt_all, vt_all)
        ya0, ya1, hfr, hfi = _s5_prompt(ua, ub, kstack, bre, bim, cre, cim, etr, eti, d_row, n, length)
        yb, ssm_p = _ssd_prompt(xbc, z, dtf, dtft, zeros_conv, zeros_ssm, cw, cb, dtb_c, dtb_r,
                                al_c, al_r, dsk, ng, n, length)
        lfr, kx, cr = _fox_prep(dtf, dtft, bf_row, bf_col, n, length, tm_p)
        yc = _fox_flash(qtb, kb, kx, vtb, cr, n, length, tm_p)
        xp = _out_proj(xp, ya0, ya1, yb, yc, w_g, b_g, w_o, g_qm, mod_p, 1, l * n, tiles, tm_p)
        xp = _ffn(xp, g_pf, w1, w2, g_qf, mod_p, 1, l * n, tiles, tm_p)
        p_out[2].append(lfr[:, F_OFF:F_OFF + C_HEADS].reshape(n, tiles, C_HEADS, tm_p)
                        .transpose(2, 0, 1, 3).reshape(C_HEADS, n, length))
        p_out[3].append(hfr.reshape(n, A_NG, A_P))
        p_out[4].append(hfi.reshape(n, A_NG, A_P))
        p_out[5].append(xbc.reshape(n, length, B_XBC)[:, length - (B_CONV - 1):])
        p_out[6].append(ssm_p)

        ua, ub, z, xbc, q, _, _, _, kt_s, vt_s, dtf, dtft, k, v = _in_proj(
            xs, mod_s, nb, l, 1, g_pm, w_all, nb, 0, 1, None, True)
        ya, s5r, s5i = _s5_sample(ua, ub, state_s5_re[l].reshape(nb, -1), state_s5_im[l].reshape(nb, -1),
                                  s5abr[l].reshape(1, -1), s5abi[l].reshape(1, -1), bre, bim, cre, cim, d_row)
        yb, ssm_all = _ssd_sample(xbc, conv_t, z, dtf, ssm_in, ssm_all, l, cw, cb, dtb_c, al_c, expand, dsk, ng)
        yc, lfn = _fox_sample(page_table, l, q, k, v, dtf, bf_row, ck, cv, clf)
        xs = _out_proj(xs, ya[:, :A_CH // 2], ya[:, A_CH // 2:], yb, yc.reshape(nb, C_CH).astype(BF16),
                       w_g, b_g, w_o, g_qm, mod_s, nb, l, 1, nb)
        xs = _ffn(xs, g_pf, w1, w2, g_qf, mod_s, nb, l, 1, nb)
        s_out[0].append(kt_s.reshape(C_HEADS, C_HD, nb))
        s_out[1].append(vt_s.reshape(C_HEADS, C_HD, nb))
        s_out[2].append(lfn[:, :, F_OFF:F_OFF + C_HEADS])
        s_out[3].append(s5r.reshape(nb, A_NG, A_P))
        s_out[4].append(s5i.reshape(nb, A_NG, A_P))
        s_out[5].append(jnp.stack([conv_t[l, 1], conv_t[l, 2], xbc], axis=0))

    plf, ps5r, ps5i, pconv, pssm = [jnp.stack(a) for a in p_out[2:]]
    sk, sv, slf, ss5r, ss5i, sconv = [jnp.stack(a) for a in s_out[:6]]
    kt_all, vt_all = kv_all
    pk = kt_all.reshape(depth, n, C_HEADS, C_HD, length).transpose(0, 1, 4, 2, 3)
    pv = vt_all.reshape(depth, n, C_HEADS, C_HD, length).transpose(0, 1, 4, 2, 3)
    plf = plf.transpose(0, 2, 3, 1)
    sk = sk.transpose(0, 3, 1, 2)[:, :, None]
    sv = sv.transpose(0, 3, 1, 2)[:, :, None]
    sconv = sconv.transpose(0, 2, 1, 3)
    sssm = ssm_all.reshape(depth, nb, B_HEADS, B_HD, B_N)
    return (xp.reshape(n, length, D_MODEL), xs.reshape(nb, 1, D_MODEL), pk, pv, plf, ps5r, ps5i, pconv, pssm,
            sk, sv, slf, ss5r, ss5i, sconv, sssm)
```

```python
import functools
import math

import jax
import jax.numpy as jnp
from jax import lax
from jax.experimental import pallas as pl
from jax.experimental.pallas import tpu as pltpu

F32 = jnp.float32
BF16 = jnp.bfloat16
HI = lax.Precision.HIGHEST

D_MODEL = 1024
A_CH = 256
A_GROUP = 16
A_NG = 16
A_P = 64
S5_T = 16
S5_ROW = S5_T * A_GROUP
S5_LEVELS = 8
S5_ROWS_E = 2 * S5_T + S5_LEVELS
B_HD = 64
B_CH = 384
B_HEADS = 6
B_NG = 2
B_N = 128
B_CONV = 4
B_XBC = 896
B_CHUNK = 128
C_HD = 64
C_CH = 384
C_HEADS = 6
PAGE = 128
D_FF = 4096
EPS = 1e-6
LOG2E = math.log2(math.e)
FLASH_TK = 512
PAD_W = 128
F_OFF = 6
N_ALL = A_CH + B_CH + B_XBC + 3 * C_CH + PAD_W
VMEM_LIMIT = 56 * 1024 * 1024


def _cp(*sem):
    return pltpu.CompilerParams(dimension_semantics=sem, vmem_limit_bytes=VMEM_LIMIT)


def _sds(shape, dtype):
    return jax.ShapeDtypeStruct(shape, dtype)


def _dot(a, b, precision=None):
    return jnp.dot(a, b, preferred_element_type=F32, precision=precision)


def _dot_nt(a, b, precision=None):
    return lax.dot_general(a, b, (((1,), (1,)), ((), ())), preferred_element_type=F32, precision=precision)


def _dot_tn(a, b, precision=None):
    return lax.dot_general(a, b, (((0,), (0,)), ((), ())), preferred_element_type=F32, precision=precision)


def _split3(x):
    hi = x.astype(BF16)
    r1 = x - hi.astype(F32)
    mid = r1.astype(BF16)
    lo = (r1 - mid.astype(F32)).astype(BF16)
    return hi, mid, lo


def _sel_dot(sel, x):
    sel = sel.astype(BF16)
    return sum(_dot(sel, part) for part in _split3(x))


def _dot_sel(x, sel):
    sel = sel.astype(BF16)
    return sum(_dot(part, sel) for part in _split3(x))


def _silu(x):
    return x * jax.nn.sigmoid(x)


def _softplus(x):
    return jnp.maximum(x, 0.0) + jnp.log1p(jnp.exp(-jnp.abs(x)))


def _log_sigmoid(x):
    return jnp.minimum(x, 0.0) - jnp.log1p(jnp.exp(-jnp.abs(x)))


def _gelu_tanh(x):
    return 0.5 * x * (1.0 + jnp.tanh(math.sqrt(2.0 / math.pi) * (x + 0.044715 * (x * x * x))))


def _rms(x, g):
    return x * lax.rsqrt(jnp.mean(x * x, axis=-1, keepdims=True) + EPS) * g


def _iota(shape, dim):
    return lax.broadcasted_iota(jnp.int32, shape, dim)


def _ada_body(c_ref, w_ref, b_ref, o_ref):
    s = _silu(c_ref[...]).astype(BF16)
    o_ref[...] = _dot(s, w_ref[...].astype(BF16)) + b_ref[...]


def _ada_mod(cond, w_ada, b_ada):
    depth = w_ada.shape[0]
    rows = cond.shape[0]
    tn = 1536
    return pl.pallas_call(
        _ada_body,
        out_shape=_sds((depth, rows, 6 * D_MODEL), F32),
        grid=(depth, 6 * D_MODEL // tn),
        in_specs=[pl.BlockSpec((rows, D_MODEL), lambda l, j: (0, 0)),
                  pl.BlockSpec((None, D_MODEL, tn), lambda l, j: (l, 0, j)),
                  pl.BlockSpec((None, 1, tn), lambda l, j: (l, 0, j))],
        out_specs=pl.BlockSpec((None, rows, tn), lambda l, j: (l, 0, j)),
        compiler_params=_cp("arbitrary", "arbitrary"),
        name="ada_mod",
    )(cond, w_ada, b_ada.reshape(depth, 1, 6 * D_MODEL))


def _mod_spec(rows, base, tiles_per_group, piece):
    return pl.BlockSpec((None, rows, D_MODEL), lambda i: (base + i // tiles_per_group, 0, piece))


def _inproj_body(*refs, n_in, rows_out):
    x_ref, g_ref, sh_ref, sc_ref, w_ref = refs[:5]
    (ua_ref, ub_ref, z_ref, xbc_ref, q_ref, qtb_ref, kb_ref, vtb_ref, kt_ref, vt_ref,
     dtf_ref, dtft_ref) = refs[n_in:n_in + 12]
    h = _rms(x_ref[...], g_ref[...]) * (1.0 + sc_ref[...]) + sh_ref[...]
    hb = h.astype(BF16)

    def mm(a, b):
        return _dot(hb, w_ref[:, a:b])

    o = 0
    u = mm(o, o + A_CH)
    ua_ref[...] = u[:, :A_CH // 2]
    ub_ref[...] = u[:, A_CH // 2:]
    o += A_CH
    z_ref[...] = mm(o, o + B_CH)
    o += B_CH
    xbc_ref[...] = mm(o, o + B_XBC)
    o += B_XBC
    q = mm(o, o + C_CH) * (C_HD ** -0.5)
    q_ref[...] = q.astype(BF16)
    qtb_ref[...] = (q * LOG2E).T.astype(BF16).reshape(qtb_ref.shape)
    o += C_CH
    k = mm(o, o + C_CH)
    kt_ref[...] = k.T
    kb_ref[...] = k.astype(BF16)
    o += C_CH
    v = mm(o, o + C_CH)
    vt = v.T
    vt_ref[...] = vt
    vtb = vt.astype(BF16)
    tk = vtb_ref.shape[-1]
    for j in range(vtb_ref.shape[1]):
        vtb_ref[:, j] = vtb[:, j * tk:(j + 1) * tk].reshape(C_HEADS // 2, 128, tk)
    o += C_CH
    dtf = mm(o, o + PAD_W)
    dtf_ref[...] = dtf
    dtft_ref[...] = dtf.T[:16, :]
    if rows_out:
        k_ref, v_ref = refs[n_in + 12:n_in + 14]
        k_ref[...] = k
        v_ref[...] = v


def _in_proj(x, mod, mod_rows, mod_base, tiles_per_group, g_pre, w_all, tm, layer, depth, kv_all, rows_out):
    t = x.shape[0]
    groups = t // (tm * tiles_per_group)
    glen = tm * tiles_per_group
    row = lambda w: pl.BlockSpec((tm, w), lambda i: (i, 0))
    kv_spec = pl.BlockSpec((None, None, C_CH, tm),
                           lambda i: (layer, i // tiles_per_group, 0, i % tiles_per_group))
    pair_t = _sds((groups, C_HEADS // 2, tiles_per_group, 128, tm), BF16)
    pair_spec = pl.BlockSpec((None, C_HEADS // 2, None, 128, tm),
                             lambda i: (i // tiles_per_group, 0, i % tiles_per_group, 0, 0))
    tk = min(FLASH_TK, tm)
    sub = tm // tk
    vt_t = _sds((groups, C_HEADS // 2, tiles_per_group * sub, 128, tk), BF16)
    vt_spec = pl.BlockSpec((None, C_HEADS // 2, sub, 128, tk),
                           lambda i: (i // tiles_per_group, 0, i % tiles_per_group, 0, 0))
    outs = [_sds((t, A_CH // 2), F32), _sds((t, A_CH // 2), F32), _sds((t, B_CH), F32), _sds((t, B_XBC), F32),
            _sds((t, C_CH), BF16), pair_t, _sds((t, C_CH), BF16), vt_t,
            _sds((depth, groups, C_CH, glen), F32), _sds((depth, groups, C_CH, glen), F32),
            _sds((t, PAD_W), F32), _sds((16, t), F32)]
    out_specs = [row(A_CH // 2), row(A_CH // 2), row(B_CH), row(B_XBC), row(C_CH), pair_spec, row(C_CH),
                 vt_spec, kv_spec, kv_spec, row(PAD_W), pl.BlockSpec((16, tm), lambda i: (0, i))]
    if rows_out:
        outs += [_sds((t, C_CH), F32), _sds((t, C_CH), F32)]
        out_specs += [row(C_CH), row(C_CH)]
    in_specs = [row(D_MODEL),
                pl.BlockSpec((1, D_MODEL), lambda i: (0, 0)),
                _mod_spec(mod_rows, mod_base, tiles_per_group, 0),
                _mod_spec(mod_rows, mod_base, tiles_per_group, 1),
                pl.BlockSpec((D_MODEL, N_ALL), lambda i: (0, 0))]
    args = [x, g_pre, mod, mod, w_all]
    aliases = {}
    if kv_all is not None:
        in_specs += [pl.BlockSpec(memory_space=pl.ANY), pl.BlockSpec(memory_space=pl.ANY)]
        args += list(kv_all)
        aliases = {5: 8, 6: 9}
    return pl.pallas_call(
        functools.partial(_inproj_body, n_in=len(args), rows_out=rows_out),
        out_shape=outs,
        grid=(t // tm,),
        in_specs=in_specs,
        out_specs=out_specs,
        input_output_aliases=aliases,
        compiler_params=_cp("arbitrary"),
        name="in_proj",
    )(*args)


def _s5_param_body(ex_ref, ldt_ref, arr_ref, air_ref, arc_ref, aic_ref, btr_ref, bti_ref, ctr_ref, cti_ref,
                   krow_ref, etr_ref, eti_ref, abr_ref, abi_ref, bbr_ref, bbi_ref):
    dt = jnp.exp(ldt_ref[...])
    ar, ai = arr_ref[...], air_ref[...]
    mag = jnp.exp(ar * dt)
    abr, abi = mag * jnp.cos(ai * dt), mag * jnp.sin(ai * dt)
    den = ar * ar + ai * ai
    nr, ni = abr - 1.0, abi
    fr, fi = (nr * ar + ni * ai) / den, (ni * ar - nr * ai) / den
    btr, bti = btr_ref[...], bti_ref[...]
    bbr, bbi = fr * btr - fi * bti, fr * bti + fi * btr
    abr_ref[...] = abr
    abi_ref[...] = abi
    bbr_ref[...] = bbr
    bbi_ref[...] = bbi

    def epow(k, a_r, a_i):
        m = jnp.exp(k * a_r * dt)
        th = k * a_i * dt
        return m * jnp.cos(th), m * jnp.sin(th)

    er, ei = epow(ex_ref[...], ar, ai)
    etr_ref[...] = er
    eti_ref[...] = ei
    arc, aic = arc_ref[...], aic_ref[...]
    lag = jnp.right_shift(_iota((1, S5_ROW), 1), 4).astype(F32)
    elr, eli = epow(lag, arc, aic)
    ctr, cti = ctr_ref[...], cti_ref[...]
    bm_re, bm_im = ctr * elr - cti * eli, ctr * eli + cti * elr
    krow_ref[...] = _dot(bbr, bm_re, HI) - _dot(bbi, bm_im, HI)


def _s5_params(a_re, a_im, log_dt, b_re, b_im, c_re, c_im):
    depth = a_re.shape[0]
    steps = jnp.arange(S5_T, dtype=F32)
    ex = jnp.concatenate([S5_T - 1.0 - steps, steps + 1.0,
                          S5_T * 2.0 ** jnp.arange(S5_LEVELS, dtype=F32)]).reshape(S5_ROWS_E, 1)
    arr, air = a_re.reshape(depth, A_NG, 1, A_P), a_im.reshape(depth, A_NG, 1, A_P)
    arc, aic = a_re.reshape(depth, A_NG, A_P, 1), a_im.reshape(depth, A_NG, A_P, 1)
    ldt = log_dt.reshape(depth, A_NG, 1, 1)
    bt = lambda b: jnp.swapaxes(b, 2, 3)
    ct = lambda c: jnp.tile(jnp.swapaxes(c, 2, 3), (1, 1, 1, S5_T))
    blk = lambda r, c: pl.BlockSpec((None, None, r, c), lambda l, g: (l, g, 0, 0))
    outs = [_sds((depth, A_NG, A_GROUP, S5_ROW), F32),
            _sds((depth, A_NG, S5_ROWS_E, A_P), F32), _sds((depth, A_NG, S5_ROWS_E, A_P), F32),
            _sds((depth, A_NG, 1, A_P), F32), _sds((depth, A_NG, 1, A_P), F32),
            _sds((depth, A_NG, A_GROUP, A_P), F32), _sds((depth, A_NG, A_GROUP, A_P), F32)]
    out_specs = [blk(A_GROUP, S5_ROW), blk(S5_ROWS_E, A_P), blk(S5_ROWS_E, A_P), blk(1, A_P), blk(1, A_P),
                 blk(A_GROUP, A_P), blk(A_GROUP, A_P)]
    return pl.pallas_call(
        _s5_param_body,
        out_shape=outs,
        grid=(depth, A_NG),
        in_specs=[pl.BlockSpec((S5_ROWS_E, 1), lambda l, g: (0, 0)),
                  blk(1, 1), blk(1, A_P), blk(1, A_P), blk(A_P, 1), blk(A_P, 1),
                  blk(A_GROUP, A_P), blk(A_GROUP, A_P), blk(A_P, S5_ROW), blk(A_P, S5_ROW)],
        out_specs=out_specs,
        compiler_params=_cp("arbitrary", "arbitrary"),
        name="s5_params",
    )(ex, ldt, arr, air, arc, aic, bt(b_re), bt(b_im), ct(c_re), ct(c_im))


def _s5_prompt_body(ua_ref, ub_ref, kst_ref, bre_ref, bim_ref, cre_ref, cim_ref, etr_ref, eti_ref, d_ref,
                    ya_ref, yb_ref, hfr_ref, hfi_ref, ucat_scr, hr_scr, hi_scr, *, chunks, levels):
    def slab(ref, s):
        return ref[pl.ds(s, chunks, stride=S5_T), :]

    def local_state(s, carry):
        x = jnp.concatenate([slab(ua_ref, s), slab(ub_ref, s)], axis=1).astype(BF16)
        bur, bui = _dot(x, bre_ref[...]), _dot(x, bim_ref[...])
        er, ei = etr_ref[pl.ds(s, 1), :], eti_ref[pl.ds(s, 1), :]
        hr_scr[...] += er * bur - ei * bui
        hi_scr[...] += er * bui + ei * bur
        return carry

    hr_scr[...] = jnp.zeros_like(hr_scr)
    hi_scr[...] = jnp.zeros_like(hi_scr)
    lax.fori_loop(0, S5_T, local_state, 0)
    for s in range(S5_T):
        ucat_scr[:, A_CH * s:A_CH * s + A_CH // 2] = slab(ua_ref, s).astype(BF16)
        ucat_scr[:, A_CH * s + A_CH // 2:A_CH * (s + 1)] = slab(ub_ref, s).astype(BF16)
    hr, hi = hr_scr[...], hi_scr[...]
    cidx = _iota((chunks, 1), 0)
    for k in range(levels):
        d = 1 << k
        keep = cidx >= d
        sr = jnp.where(keep, pltpu.roll(hr, d, axis=0), 0.0)
        si = jnp.where(keep, pltpu.roll(hi, d, axis=0), 0.0)
        lr, li = etr_ref[2 * S5_T + k:2 * S5_T + k + 1, :], eti_ref[2 * S5_T + k:2 * S5_T + k + 1, :]
        hr, hi = hr + lr * sr - li * si, hi + lr * si + li * sr
    hfr_ref[...] = hr[chunks - 1:chunks, :]
    hfi_ref[...] = hi[chunks - 1:chunks, :]
    keep = cidx >= 1
    hr_scr[...] = jnp.where(keep, pltpu.roll(hr, 1, axis=0), 0.0)
    hi_scr[...] = jnp.where(keep, pltpu.roll(hi, 1, axis=0), 0.0)
    half = A_CH // 2

    def outputs(t, span):
        er, ei = etr_ref[pl.ds(S5_T + t, 1), :], eti_ref[pl.ds(S5_T + t, 1), :]
        pr, pi = hr_scr[...], hi_scr[...]
        gr = (er * pr - ei * pi).astype(BF16)
        gi = (er * pi + ei * pr).astype(BF16)
        first = pl.multiple_of((S5_T - 1 - t) * A_CH, A_CH)
        y = (_dot(ucat_scr[:, :A_CH * span], kst_ref[pl.ds(first, A_CH * span), :])
             + _dot(gr, cre_ref[...]) - _dot(gi, cim_ref[...]))
        dd = d_ref[...]
        ya_ref[pl.ds(t, chunks, stride=S5_T), :] = y[:, :half] + dd[:, :half] * slab(ua_ref, t)
        yb_ref[pl.ds(t, chunks, stride=S5_T), :] = y[:, half:] + dd[:, half:] * slab(ub_ref, t)

    def first_half(t, carry):
        outputs(t, S5_T // 2)
        return carry

    def second_half(t, carry):
        outputs(t, S5_T)
        return carry

    lax.fori_loop(0, S5_T // 2, first_half, 0)
    lax.fori_loop(S5_T // 2, S5_T, second_half, 0)


def _s5_prompt(ua, ub, kstack, bre, bim, cre, cim, etr, eti, d, n, length):
    chunks = length // S5_T
    levels = max(1, (chunks - 1).bit_length())
    assert chunks & (chunks - 1) == 0 and levels <= S5_LEVELS
    half = A_CH // 2
    w = A_NG * A_P
    tok = pl.BlockSpec((length, half), lambda i: (i, 0))
    const = lambda r, c: pl.BlockSpec((r, c), lambda i: (0, 0))
    fin = pl.BlockSpec((None, 1, w), lambda i: (i, 0, 0))
    return pl.pallas_call(
        functools.partial(_s5_prompt_body, chunks=chunks, levels=levels),
        out_shape=[_sds((n * length, half), F32), _sds((n * length, half), F32),
                   _sds((n, 1, w), F32), _sds((n, 1, w), F32)],
        grid=(n,),
        in_specs=[tok, tok, const(2 * S5_T * A_CH, A_CH), const(A_CH, w), const(A_CH, w), const(w, A_CH),
                  const(w, A_CH), const(S5_ROWS_E, w), const(S5_ROWS_E, w), const(1, A_CH)],
        out_specs=[tok, tok, fin, fin],
        scratch_shapes=[pltpu.VMEM((chunks, S5_T * A_CH), BF16), pltpu.VMEM((chunks, w), F32),
                        pltpu.VMEM((chunks, w), F32)],
        compiler_params=_cp("arbitrary"),
        name="s5_prompt",
    )(ua, ub, kstack, bre, bim, cre, cim, etr, eti, d)


def _s5_sample_body(ua_ref, ub_ref, hr_ref, hi_ref, abr_ref, abi_ref, bre_ref, bim_ref, cre_ref, cim_ref,
                    d_ref, y_ref, sr_ref, si_ref):
    u = jnp.concatenate([ua_ref[...], ub_ref[...]], axis=1)
    ub = u.astype(BF16)
    hr, hi = hr_ref[...], hi_ref[...]
    abr, abi = abr_ref[...], abi_ref[...]
    sr = abr * hr - abi * hi + _dot(ub, bre_ref[...])
    si = abr * hi + abi * hr + _dot(ub, bim_ref[...])
    sr_ref[...] = sr
    si_ref[...] = si
    y_ref[...] = _dot(sr.astype(BF16), cre_ref[...]) - _dot(si.astype(BF16), cim_ref[...]) + d_ref[...] * u


def _s5_sample(ua, ub, hr, hi, abr, abi, bre, bim, cre, cim, d):
    b = ua.shape[0]
    w = A_NG * A_P
    return pl.pallas_call(
        _s5_sample_body,
        out_shape=[_sds((b, A_CH), F32), _sds((b, w), F32), _sds((b, w), F32)],
        compiler_params=pltpu.CompilerParams(vmem_limit_bytes=VMEM_LIMIT),
        name="s5_sample",
    )(ua, ub, hr, hi, abr, abi, bre, bim, cre, cim, d)


def _ssd_prompt_body(xbc_ref, z_ref, dtc_ref, dtr_ref, cprev_ref, h0_ref, cw_ref, cb_ref,
                     dtbc_ref, dtbr_ref, alc_ref, alr_ref, dsk_ref, ng_ref,
                     y_ref, hf_ref, ext_scr, h_scr):
    c = pl.program_id(1)
    q = B_CHUNK

    @pl.when(c == 0)
    def _():
        ext_scr[0:8, :] = cprev_ref[...]
        h_scr[...] = h0_ref[...]

    x = xbc_ref[...]
    ext_scr[8:8 + q, :] = x
    cw = cw_ref[...]
    conv = (cb_ref[...] + cw[3:4] * x + cw[2:3] * ext_scr[7:7 + q, :]
            + cw[1:2] * ext_scr[6:6 + q, :] + cw[0:1] * ext_scr[5:5 + q, :])
    ext_scr[0:8, :] = x[q - 8:q, :]
    xc = _silu(conv)

    r_i, c_i = _iota((q, q), 0), _iota((q, q), 1)
    causal = r_i >= c_i
    dtc = _softplus(dtc_ref[...] + dtbc_ref[...])
    da_c = dtc * (-jnp.exp(alc_ref[...]))
    acc_c = _sel_dot(causal, da_c)
    dtr = _softplus(dtr_ref[...] + dtbr_ref[...])
    da_r = dtr * (-jnp.exp(alr_ref[...]))
    acc_r = _dot_sel(da_r, r_i <= c_i)

    bm = [xc[:, B_CH + B_N * g:B_CH + B_N * (g + 1)].astype(BF16) for g in range(B_NG)]
    cm = [xc[:, B_CH + B_N * (B_NG + g):B_CH + B_N * (B_NG + g + 1)].astype(BF16) for g in range(B_NG)]
    cbm = [_dot_nt(cm[g], bm[g]) for g in range(B_NG)]
    lane = _iota((q, 128), 1)
    heads_per_group = B_HEADS // B_NG
    ys = []
    for j in range(B_HEADS // 2):
        xs_pair = xc[:, 128 * j:128 * (j + 1)]
        dt_pair = jnp.where(lane < B_HD, dtc[:, 2 * j:2 * j + 1], dtc[:, 2 * j + 1:2 * j + 2])
        xdt = xs_pair * dt_pair
        xdt_b = xdt.astype(BF16)
        hprev_b = h_scr[2 * j:2 * j + 2].reshape(2 * B_HD, B_N).astype(BF16)
        outs = []
        for e in range(2):
            h = 2 * j + e
            g = h // heads_per_group
            ac = acc_c[:, h:h + 1]
            ar = acc_r[h:h + 1, :]
            lm = jnp.exp(jnp.where(causal, ac - ar, -jnp.inf))
            yd = _dot((cbm[g] * lm).astype(BF16), xdt_b)
            yo = _dot_nt(cm[g], hprev_b) * jnp.exp(ac)
            outs.append(yd + yo)
            alast = acc_r[h:h + 1, q - 1:q]
            st = _dot_tn((xdt * jnp.exp(alast - ac)).astype(BF16), bm[g])
            h_scr[h] = jnp.exp(alast) * h_scr[h] + st[B_HD * e:B_HD * (e + 1), :]
        ys.append(jnp.where(lane < B_HD, outs[0], outs[1]) + dsk_ref[:, 128 * j:128 * (j + 1)] * xs_pair)
    y = jnp.concatenate(ys, axis=1)
    y_ref[...] = _rms(y * _silu(z_ref[...]), ng_ref[...]).astype(y_ref.dtype)

    @pl.when(c == pl.num_programs(1) - 1)
    def _():
        hf_ref[...] = h_scr[...]


def _ssd_prompt(xbc, z, dtf, dtft, cprev, h0, cw, cb, dtb_c, dtb_r, al_c, al_r, dsk, ng, n, length):
    nc = length // B_CHUNK
    q = B_CHUNK
    tok = lambda w: pl.BlockSpec((q, w), lambda i, c: (i * nc + c, 0))
    const = lambda r, w: pl.BlockSpec((r, w), lambda i, c: (0, 0))
    return pl.pallas_call(
        _ssd_prompt_body,
        out_shape=[_sds((n * length, B_CH), BF16), _sds((n, B_HEADS, B_HD, B_N), F32)],
        grid=(n, nc),
        in_specs=[tok(B_XBC), tok(B_CH), tok(PAD_W),
                  pl.BlockSpec((16, q), lambda i, c: (0, i * nc + c)),
                  pl.BlockSpec((None, 8, B_XBC), lambda i, c: (i, 0, 0)),
                  pl.BlockSpec((None, B_HEADS, B_HD, B_N), lambda i, c: (i, 0, 0, 0)),
                  const(B_CONV, B_XBC), const(1, B_XBC), const(1, PAD_W), const(16, 1),
                  const(1, PAD_W), const(16, 1), const(1, B_CH), const(1, B_CH)],
        out_specs=[tok(B_CH), pl.BlockSpec((None, B_HEADS, B_HD, B_N), lambda i, c: (i, 0, 0, 0))],
        scratch_shapes=[pltpu.VMEM((8 + q, B_XBC), F32), pltpu.VMEM((B_HEADS, B_HD, B_N), F32)],
        compiler_params=_cp("arbitrary", "arbitrary"),
        name="ssd_prompt",
    )(xbc, z, dtf, dtft, cprev, h0, cw, cb, dtb_c, dtb_r, al_c, al_r, dsk, ng)


def _ssd_sample_body(xbc_ref, p0_ref, p1_ref, p2_ref, z_ref, dtf_ref, h0_ref, cw_ref, cb_ref,
                     dtb_ref, al_ref, ex_ref, dsk_ref, ng_ref, *rest, tb):
    y_ref, hn_ref, y_scr = rest[-3:]
    x = xbc_ref[...]
    cw = cw_ref[...]
    conv = cb_ref[...] + cw[0:1] * p0_ref[...] + cw[1:2] * p1_ref[...] + cw[2:3] * p2_ref[...] + cw[3:4] * x
    xc = _silu(conv)
    dt = _softplus(dtf_ref[...] + dtb_ref[...])
    e = jnp.exp(dt * (-jnp.exp(al_ref[...])))
    ex = ex_ref[...]
    dt_e = _dot_sel(dt, ex)
    e_e = _dot_sel(e, ex)
    xs = xc[:, :B_CH]
    xdt = xs * dt_e
    half = B_CH // B_NG
    row8, lane8 = _iota((8, B_CH), 0), _iota((8, B_CH), 1)
    r8 = _iota((8, B_N), 0)
    lane1 = _iota((1, B_CH), 1)
    for b in range(tb):
        xr = xdt[b:b + 1, :]
        lhs = jnp.where(row8 == 0, e_e[b:b + 1, :],
                        jnp.where(((row8 == 1) & (lane8 < half)) | ((row8 == 2) & (lane8 >= half)), xr, 0.0))
        rhs_e = jnp.where(r8 == 0, 1.0, 0.0)
        rhs_s = jnp.where(r8 == 1, xc[b:b + 1, B_CH:B_CH + B_N],
                          jnp.where(r8 == 2, xc[b:b + 1, B_CH + B_N:B_CH + 2 * B_N], 0.0))
        hn = _dot_tn(lhs, rhs_e, HI) * h0_ref[b] + _dot_tn(lhs, rhs_s, HI)
        hn_ref[b] = hn
        cmat = jnp.where(r8 == 0, xc[b:b + 1, B_CH + 2 * B_N:B_CH + 3 * B_N],
                         jnp.where(r8 == 1, xc[b:b + 1, B_CH + 3 * B_N:B_CH + 4 * B_N], 0.0))
        yt = _dot_nt(cmat.astype(BF16), hn.astype(BF16))
        y_scr[b:b + 1, :] = jnp.where(lane1 < half, yt[0:1, :], yt[1:2, :])
    y = y_scr[...] + dsk_ref[...] * xs
    y_ref[...] = _rms(y * _silu(z_ref[...]), ng_ref[...]).astype(y_ref.dtype)


def _ssd_sample(xbc, conv_t, z, dtf, ssm_in, ssm_out, layer, cw, cb, dtb, al, ex, dsk, ng):
    depth, b = ssm_in.shape[0], xbc.shape[0]
    tb = 8
    row = lambda w: pl.BlockSpec((tb, w), lambda i: (i, 0))
    const = lambda r, w: pl.BlockSpec((r, w), lambda i: (0, 0))
    prev = lambda k: pl.BlockSpec((None, None, tb, B_XBC), lambda i: (layer, k, i, 0))
    st = pl.BlockSpec((None, tb, B_CH, B_N), lambda i: (layer, i, 0, 0))
    in_specs = [row(B_XBC), prev(0), prev(1), prev(2), row(B_CH), row(PAD_W), st,
                const(B_CONV, B_XBC), const(1, B_XBC), const(1, PAD_W), const(1, PAD_W),
                const(PAD_W, B_CH), const(1, B_CH), const(1, B_CH)]
    args = [xbc, conv_t, conv_t, conv_t, z, dtf, ssm_in, cw, cb, dtb, al, ex, dsk, ng]
    aliases = {}
    if ssm_out is not None:
        in_specs.append(pl.BlockSpec(memory_space=pl.ANY))
        args.append(ssm_out)
        aliases = {len(args) - 1: 1}
    return pl.pallas_call(
        functools.partial(_ssd_sample_body, tb=tb),
        out_shape=[_sds((b, B_CH), BF16), _sds((depth, b, B_CH, B_N), F32)],
        grid=(b // tb,),
        in_specs=in_specs,
        out_specs=[row(B_CH), st],
        scratch_shapes=[pltpu.VMEM((tb, B_CH), F32)],
        input_output_aliases=aliases,
        compiler_params=_cp("arbitrary"),
        name="ssd_sample",
    )(*args)


def _fox_prep_body(fc_ref, fr_ref, bfr_ref, bfc_ref, place_ref, ones_ref, lfr_ref, kx_ref, cr_ref,
                   carry_c, carry_r):
    j = pl.program_id(1)
    tl = fc_ref.shape[0]

    @pl.when(j == 0)
    def _():
        carry_c[...] = jnp.zeros_like(carry_c)
        carry_r[...] = jnp.zeros_like(carry_r)

    r_i, c_i = _iota((tl, tl), 0), _iota((tl, tl), 1)
    lfc = _log_sigmoid(fc_ref[...] + bfr_ref[...])
    cc = _sel_dot(r_i >= c_i, lfc) + carry_c[0:1, :]
    carry_c[...] = jnp.broadcast_to(cc[tl - 1:tl, :], carry_c.shape)
    parts = _split3(-LOG2E * cc)
    kx = ones_ref[...] + sum(_dot(parts[i], place_ref[i]) for i in range(3))
    kx_ref[...] = kx.astype(BF16)
    lfr = _log_sigmoid(fr_ref[...] + bfc_ref[...])
    lfr_ref[...] = lfr
    cr = _dot_sel(lfr, r_i <= c_i) + carry_r[:, 0:1]
    cr_ref[...] = cr
    carry_r[...] = jnp.broadcast_to(cr[:, tl - 1:tl], carry_r.shape)


def _bias_slot(pair, head, term):
    return 16 * pair + 3 * head + term


def _fox_prep(dtf, dtft, bf_row, bf_col, n, length, tl):
    nt = length // tl
    t = n * length
    place = jnp.zeros((3, PAD_W, 128), F32)
    ones = jnp.zeros((1, 128), F32)
    for pair in range(C_HEADS // 2):
        for term in range(3):
            ones = ones.at[0, _bias_slot(pair, 2, term)].set(1.0)
            for head in range(2):
                place = place.at[term, F_OFF + 2 * pair + head, _bias_slot(pair, head, term)].set(1.0)
    tok = pl.BlockSpec((tl, PAD_W), lambda i, j: (i * nt + j, 0))
    rowt = pl.BlockSpec((None, 16, tl), lambda i, j: (i * nt + j, 0, 0))
    return pl.pallas_call(
        _fox_prep_body,
        out_shape=[_sds((n * nt, 16, tl), F32), _sds((t, 128), BF16), _sds((n * nt, 16, tl), F32)],
        grid=(n, nt),
        in_specs=[tok, pl.BlockSpec((16, tl), lambda i, j: (0, i * nt + j)),
                  pl.BlockSpec((1, PAD_W), lambda i, j: (0, 0)), pl.BlockSpec((16, 1), lambda i, j: (0, 0)),
                  pl.BlockSpec((3, PAD_W, 128), lambda i, j: (0, 0, 0)),
                  pl.BlockSpec((1, 128), lambda i, j: (0, 0))],
        out_specs=[rowt, pl.BlockSpec((tl, 128), lambda i, j: (i * nt + j, 0)), rowt],
        scratch_shapes=[pltpu.VMEM((8, PAD_W), F32), pltpu.VMEM((16, 128), F32)],
        compiler_params=_cp("arbitrary", "arbitrary"),
        name="fox_prep",
    )(dtf, dtft, bf_row, bf_col, place.astype(BF16), ones)


def _fox_flash_body(qt_ref, k_ref, kx_ref, vt_ref, cr_ref, o_ref, qa_scr, m_scr, l_scr, acc_scr):
    hp, qi = pl.program_id(1), pl.program_id(2)
    tq = qt_ref.shape[1]
    tk = vt_ref.shape[2]
    row = _iota((128, tq), 0)
    qt = qt_ref[...]
    base = 16 * hp
    for e in range(2):
        cq = _split3(LOG2E * cr_ref[pl.ds(F_OFF + 2 * hp + e, 1), :])
        qx = jnp.where((row >= base + 3 * e) & (row < base + 3 * e + 3), 1.0, 0.0).astype(BF16)
        for term in range(3):
            qx = jnp.where(row == base + 6 + term, cq[term], qx)
        qa_scr[e, 0:128, :] = jnp.where((row < C_HD) == (e == 0), qt, jnp.zeros_like(qt))
        qa_scr[e, 128:256, :] = qx
    m_scr[...] = jnp.full(m_scr.shape, -jnp.inf, F32)
    l_scr[...] = jnp.zeros_like(l_scr)
    acc_scr[...] = jnp.zeros_like(acc_scr)

    def block(ki, first_key):
        rows = pl.ds(pl.multiple_of(ki * tk, tk), tk)
        ka = jnp.concatenate([k_ref[rows, :], kx_ref[rows, :]], axis=1)
        vt = vt_ref[ki]
        scores = [_dot(ka, qa_scr[e]) for e in range(2)]
        probs, alphas = [], []
        for e in range(2):
            s = scores[e]
            if first_key is not None:
                s = jnp.where(first_key + _iota((tk, tq), 0) <= _iota((tk, tq), 1), s, -jnp.inf)
            m_old = m_scr[e]
            m_new = jnp.maximum(m_old, jnp.max(s, axis=0, keepdims=True))
            alpha = jnp.exp2(m_old - m_new)
            p = jnp.exp2(s - m_new)
            l_scr[e] = alpha * l_scr[e] + jnp.sum(p, axis=0, keepdims=True)
            m_scr[e] = m_new
            probs.append(p.astype(BF16))
            alphas.append(alpha)
        for e in range(2):
            acc_scr[e] = alphas[e] * acc_scr[e] + _dot(vt, probs[e])

    def below_diagonal(ki, carry):
        block(ki, None)
        return carry

    sub = tq // tk
    lax.fori_loop(0, qi * sub, below_diagonal, 0)
    for j in range(sub):
        block(qi * sub + j, j * tk)
    ot = jnp.where(row < C_HD, acc_scr[0] / l_scr[0], acc_scr[1] / l_scr[1])
    o_ref[...] = ot.T.astype(o_ref.dtype)


def _fox_flash(qt, k, kx, vt, cr, n, length, tq):
    nq = length // tq
    nk, tk = vt.shape[2], vt.shape[4]
    return pl.pallas_call(
        _fox_flash_body,
        out_shape=_sds((n * length, C_CH), BF16),
        grid=(n, C_HEADS // 2, nq),
        in_specs=[pl.BlockSpec((None, None, None, 128, tq), lambda i, h, a: (i, h, a, 0, 0)),
                  pl.BlockSpec((length, 128), lambda i, h, a: (i, h)),
                  pl.BlockSpec((length, 128), lambda i, h, a: (i, 0)),
                  pl.BlockSpec((None, None, nk, 128, tk), lambda i, h, a: (i, h, 0, 0, 0)),
                  pl.BlockSpec((None, 16, tq), lambda i, h, a: (i * nq + a, 0, 0))],
        out_specs=pl.BlockSpec((tq, 128), lambda i, h, a: (i * nq + a, h)),
        scratch_shapes=[pltpu.VMEM((2, 256, tq), BF16), pltpu.VMEM((2, 1, tq), F32),
                        pltpu.VMEM((2, 1, tq), F32), pltpu.VMEM((2, 128, tq), F32)],
        compiler_params=_cp("arbitrary", "arbitrary", "arbitrary"),
        name="fox_flash",
    )(qt, k, kx, vt, cr)


def _fox_sample_body(pt_ref, q_ref, kn_ref, vn_ref, fn_ref, bf_ref, *refs, pages):
    k_refs = refs[0:pages]
    v_refs = refs[pages:2 * pages]
    f_refs = refs[2 * pages:3 * pages]
    o_ref, lf_ref, lf_scr, s_scr = refs[3 * pages:3 * pages + 4]
    i = pl.program_id(0)
    row, lane = _iota((8, C_CH), 0), _iota((8, C_CH), 1)
    own = jnp.right_shift(lane, 6) == row
    qf = q_ref[...].astype(F32)
    qm = jnp.where(own, qf, 0.0)
    qcol = [jnp.broadcast_to(qf[:, 128 * j:128 * (j + 1)], (128, 128)).T for j in range(C_CH // 128)]
    lfn = _log_sigmoid(fn_ref[...] + bf_ref[...])
    lf_ref[...] = lfn
    carry = jnp.sum(jnp.where(_iota((8, PAD_W), 1) == F_OFF + _iota((8, PAD_W), 0), lfn, 0.0),
                    axis=1, keepdims=True)
    for p in range(pages):
        r = jnp.bitwise_and(pt_ref[i, p], 7)
        for h in range(C_HEADS):
            lf_scr[8 * p + h:8 * p + h + 1, :] = f_refs[p][h, pl.ds(r, 1), :]
        lf_scr[8 * p + C_HEADS:8 * p + 8, :] = jnp.zeros((8 - C_HEADS, PAGE), F32)
    lf_all = lf_scr[...]
    later = _iota((PAGE, PAGE), 0) > _iota((PAGE, PAGE), 1)
    suffix = _dot_sel(lf_all, later)
    total = jnp.sum(lf_all, axis=1, keepdims=True)
    carries = [None] * pages
    for p in reversed(range(pages)):
        carries[p] = carry
        carry = carry + total[8 * p:8 * p + 8, :]
    for p in range(pages):
        for h in range(C_HEADS):
            qh = qcol[h // 2][C_HD * (h % 2):C_HD * (h % 2 + 1), :]
            s_scr[8 * p + h:8 * p + h + 1, :] = jnp.sum(k_refs[p][h] * qh, axis=0, keepdims=True)
        s_scr[8 * p + C_HEADS:8 * p + 8, :] = jnp.zeros((8 - C_HEADS, PAGE), F32)
    s = s_scr[...] + suffix + jnp.concatenate(carries, axis=0)
    s_new = jnp.sum(qm * kn_ref[...], axis=1, keepdims=True)
    row_max = jnp.max(s, axis=1, keepdims=True)
    m = s_new
    for p in range(pages):
        m = jnp.maximum(m, row_max[8 * p:8 * p + 8, :])
    pr = jnp.exp(s - jnp.concatenate([m] * pages, axis=0))
    p_new = jnp.exp(s_new - m)
    row_sum = jnp.sum(pr, axis=1, keepdims=True)
    l = p_new
    for p in range(pages):
        l = l + row_sum[8 * p:8 * p + 8, :]
    outs = []
    for j in range(C_HEADS // 2):
        pair = []
        for h in (2 * j, 2 * j + 1):
            acc = jnp.zeros((C_HD, PAGE), F32)
            for p in range(pages):
                acc = acc + v_refs[p][h] * pr[8 * p + h:8 * p + h + 1, :]
            pair.append(acc)
        outs.append(jnp.sum(jnp.concatenate(pair, axis=0).T, axis=0, keepdims=True))
    spread = lambda c: jnp.sum(jnp.where(own, c, 0.0), axis=0, keepdims=True)
    o_ref[...] = (jnp.concatenate(outs, axis=1) + spread(p_new) * vn_ref[...]) / spread(l)


def _fox_sample(page_table, layer, q, k_new, v_new, f_new, bf_row, cache_kt, cache_vt, cache_lft):
    b = q.shape[0]
    pages = page_table.shape[1]
    tok = lambda w: pl.BlockSpec((None, 1, w), lambda i, pt: (i, 0, 0))
    page = lambda p: pl.BlockSpec((None, None, C_HEADS, C_HD, PAGE), lambda i, pt: (layer, pt[i, p], 0, 0, 0))
    lfpage = lambda p: pl.BlockSpec((None, C_HEADS, 8, PAGE), lambda i, pt: (layer, 0, pt[i, p] // 8, 0))
    in_specs = ([tok(C_CH), tok(C_CH), tok(C_CH), tok(PAD_W), pl.BlockSpec((1, PAD_W), lambda i, pt: (0, 0))]
                + [page(p) for p in range(pages)] + [page(p) for p in range(pages)]
                + [lfpage(p) for p in range(pages)])
    grid_spec = pltpu.PrefetchScalarGridSpec(
        num_scalar_prefetch=1, grid=(b,), in_specs=in_specs,
        out_specs=[tok(C_CH), tok(PAD_W)],
        scratch_shapes=[pltpu.VMEM((8 * pages, PAGE), F32), pltpu.VMEM((8 * pages, PAGE), F32)])
    return pl.pallas_call(
        functools.partial(_fox_sample_body, pages=pages),
        out_shape=[_sds((b, 1, C_CH), F32), _sds((b, 1, PAD_W), F32)],
        grid_spec=grid_spec,
        compiler_params=_cp("arbitrary"),
        name="fox_sample",
    )(page_table, q.reshape(b, 1, C_CH), k_new.reshape(b, 1, C_CH), v_new.reshape(b, 1, C_CH),
      f_new.reshape(b, 1, PAD_W), bf_row,
      *([cache_kt] * pages), *([cache_vt] * pages), *([cache_lft] * pages))


def _outproj_body(x_ref, ya0_ref, ya1_ref, yb_ref, yc_ref, wg_ref, bg_ref, wo_ref, gp_ref, g1_ref, o_ref):
    y = _gelu_tanh(jnp.concatenate([ya0_ref[...], ya1_ref[...]], axis=1))
    gate = jax.nn.sigmoid(_dot(y.astype(BF16), wg_ref[...]) + bg_ref[...])
    ya = (y * gate).astype(BF16)
    mix = (_dot(ya, wo_ref[0:A_CH, :]) + _dot(yb_ref[...], wo_ref[A_CH:A_CH + B_CH, :])
           + _dot(yc_ref[...], wo_ref[A_CH + B_CH:, :]))
    o_ref[...] = x_ref[...] + g1_ref[...] * _rms(mix, gp_ref[...])


def _out_proj(x, ya0, ya1, yb, yc, w_glu, b_glu, w_out, g_post, mod, mod_rows, mod_base, tiles_per_group, tm):
    t = x.shape[0]
    row = lambda w: pl.BlockSpec((tm, w), lambda i: (i, 0))
    const = lambda r, w: pl.BlockSpec((r, w), lambda i: (0, 0))
    return pl.pallas_call(
        _outproj_body,
        out_shape=_sds((t, D_MODEL), F32),
        grid=(t // tm,),
        in_specs=[row(D_MODEL), row(A_CH // 2), row(A_CH // 2), row(B_CH), row(C_CH), const(A_CH, A_CH),
                  const(1, A_CH), const(D_MODEL, D_MODEL), const(1, D_MODEL),
                  _mod_spec(mod_rows, mod_base, tiles_per_group, 2)],
        out_specs=row(D_MODEL),
        compiler_params=_cp("arbitrary"),
        name="out_proj",
    )(x, ya0, ya1, yb, yc, w_glu, b_glu, w_out, g_post, mod)


def _ffn_body(x_ref, gpre_ref, sh_ref, sc_ref, w1_ref, w2_ref, gpost_ref, g2_ref, o_ref, *, tf):
    x = x_ref[...]
    hb = (_rms(x, gpre_ref[...]) * (1.0 + sc_ref[...]) + sh_ref[...]).astype(BF16)
    acc = jnp.zeros(x.shape, F32)
    for j in range(D_FF // tf):
        a = jnp.maximum(_dot(hb, w1_ref[:, j * tf:(j + 1) * tf]), 0.0)
        acc = acc + _dot((a * a).astype(BF16), w2_ref[j * tf:(j + 1) * tf, :])
    o_ref[...] = x + g2_ref[...] * _rms(acc, gpost_ref[...])


def _ffn(x, g_pre, w1, w2, g_post, mod, mod_rows, mod_base, tiles_per_group, tm):
    t = x.shape[0]
    row = pl.BlockSpec((tm, D_MODEL), lambda i: (i, 0))
    const = lambda r, w: pl.BlockSpec((r, w), lambda i: (0, 0))
    weight = lambda r, w: pl.BlockSpec((r, w), lambda i: (0, 0), pipeline_mode=pl.Buffered(1))
    return pl.pallas_call(
        functools.partial(_ffn_body, tf=1024),
        out_shape=_sds((t, D_MODEL), F32),
        grid=(t // tm,),
        in_specs=[row, const(1, D_MODEL), _mod_spec(mod_rows, mod_base, tiles_per_group, 3),
                  _mod_spec(mod_rows, mod_base, tiles_per_group, 4), weight(D_MODEL, D_FF), weight(D_FF, D_MODEL),
                  const(1, D_MODEL), _mod_spec(mod_rows, mod_base, tiles_per_group, 5)],
        out_specs=row,
        compiler_params=_cp("arbitrary"),
        name="ffn",
    )(x, g_pre, mod, mod, w1, w2, g_post, mod)


def _pad_lanes(v, offset, width=PAD_W):
    return jnp.zeros((1, width), F32).at[0, offset:offset + v.shape[0]].set(v)


def _pad_rows(v, offset, rows=16):
    return jnp.zeros((rows, 1), F32).at[offset:offset + v.shape[0], 0].set(v)


def kernel(x_prompt, x_sample, c_prompt, c_sample, cache_k, cache_v, cache_logf, page_table, state_s5_re, state_s5_im, state_conv, state_ssm, w_ada, b_ada, g_pre_mix, g_post_mix, g_pre_ffn, g_post_ffn, w_in, w_out, s5_a_re, s5_a_im, s5_log_dt, s5_b_re, s5_b_im, s5_c_re, s5_c_im, s5_d, s5_w_glu, s5_b_glu, ssd_conv_w, ssd_conv_b, ssd_dt_bias, ssd_a_log, ssd_d, ssd_norm_g, fox_b_f, w_ff1, w_ff2):
    depth = w_ada.shape[0]
    n, length, _ = x_prompt.shape
    nb = x_sample.shape[0]
    t = n * length
    n_pool = cache_k.shape[1]
    chunks = length // S5_T
    tm_p = 512
    assert x_sample.shape[1] == 1 and length % tm_p == 0 and nb % 8 == 0

    rows = -(-(n + nb) // 8) * 8
    cond = jnp.concatenate([c_prompt, c_sample, jnp.zeros((rows - n - nb, D_MODEL), F32)], axis=0)
    mod = _ada_mod(cond, w_ada, b_ada)
    mod_p = mod[:, :n].reshape(depth * n, 1, 6 * D_MODEL)
    mod_s = mod[:, n:n + nb]

    s5krow, s5etr, s5eti, s5abr, s5abi, s5bbr, s5bbi = _s5_params(
        s5_a_re, s5_a_im, s5_log_dt, s5_b_re, s5_b_im, s5_c_re, s5_c_im)
    eye_g = jnp.eye(A_NG, dtype=F32)
    w_s5 = A_NG * A_P
    expand = (jnp.arange(PAD_W)[:, None] == jnp.arange(B_CH)[None, :] // B_HD).astype(F32)

    ck = cache_k.transpose(0, 1, 3, 4, 2)
    cv = cache_v.transpose(0, 1, 3, 4, 2)
    clf = cache_logf.transpose(0, 3, 1, 2)
    conv_t = state_conv.transpose(0, 2, 1, 3)
    ssm_in = state_ssm.reshape(depth, nb, B_CH, B_N)

    xp = x_prompt.reshape(t, D_MODEL)
    xs = x_sample.reshape(nb, D_MODEL)
    zeros_conv = jnp.zeros((n, 8, B_XBC), F32)
    zeros_ssm = jnp.zeros((n, B_HEADS, B_HD, B_N), F32)
    p_out = [[] for _ in range(7)]
    s_out = [[] for _ in range(7)]
    kv_all = None
    ssm_all = None

    for l in range(depth):
        wi = w_in[l]
        o_dt = A_CH + B_CH + B_XBC
        o_f = o_dt + B_HEADS + 3 * C_CH
        w_all = jnp.concatenate(
            [wi[:, :o_dt], wi[:, o_dt + B_HEADS:o_f], wi[:, o_dt:o_dt + B_HEADS], wi[:, o_f:],
             jnp.zeros((D_MODEL, PAD_W - B_HEADS - C_HEADS), F32)], axis=1).astype(BF16)
        w_o = w_out[l].astype(BF16)
        w_g = s5_w_glu[l].astype(BF16)
        w1 = w_ff1[l].astype(BF16)
        w2 = w_ff2[l].astype(BF16)
        row = lambda v: v.reshape(1, -1)
        g_pm, g_qm, g_pf, g_qf = row(g_pre_mix[l]), row(g_post_mix[l]), row(g_pre_ffn[l]), row(g_post_ffn[l])
        b_g = row(s5_b_glu[l])
        cw, cb = ssd_conv_w[l], row(ssd_conv_b[l])
        dtb_c, dtb_r = _pad_lanes(ssd_dt_bias[l], 0), _pad_rows(ssd_dt_bias[l], 0)
        al_c, al_r = _pad_lanes(ssd_a_log[l], 0), _pad_rows(ssd_a_log[l], 0)
        dsk = row(jnp.repeat(ssd_d[l], B_HD))
        ng = row(ssd_norm_g[l])
        bf_row, bf_col = _pad_lanes(fox_b_f[l], F_OFF), _pad_rows(fox_b_f[l], F_OFF)
        d_row = row(s5_d[l])
        bre = jnp.einsum("gcp,gh->gchp", s5bbr[l], eye_g).reshape(A_CH, w_s5).astype(BF16)
        bim = jnp.einsum("gcp,gh->gchp", s5bbi[l], eye_g).reshape(A_CH, w_s5).astype(BF16)
        cre = jnp.einsum("gcp,gh->gphc", s5_c_re[l], eye_g).reshape(w_s5, A_CH).astype(BF16)
        cim = jnp.einsum("gcp,gh->gphc", s5_c_im[l], eye_g).reshape(w_s5, A_CH).astype(BF16)
        klag = jnp.einsum("galc,gh->lgahc", s5krow[l].reshape(A_NG, A_GROUP, S5_T, A_GROUP), eye_g)
        kstack = jnp.concatenate([klag[::-1].reshape(S5_T * A_CH, A_CH), jnp.zeros((S5_T * A_CH, A_CH), F32)],
                                 axis=0).astype(BF16)
        etr = s5etr[l].transpose(1, 0, 2).reshape(S5_ROWS_E, w_s5)
        eti = s5eti[l].transpose(1, 0, 2).reshape(S5_ROWS_E, w_s5)

        tiles = length // tm_p
        ua, ub, z, xbc, _, qtb, kb, vtb, kt_all, vt_all, dtf, dtft = _in_proj(
            xp, mod_p, 1, l * n, tiles, g_pm, w_all, tm_p, l, depth, kv_all, False)
        kv_all = (kt_all, vt_all)
        ya0, ya1, hfr, hfi = _s5_prompt(ua, ub, kstack, bre, bim, cre, cim, etr, eti, d_row, n, length)
        yb, ssm_p = _ssd_prompt(xbc, z, dtf, dtft, zeros_conv, zeros_ssm, cw, cb, dtb_c, dtb_r,
                                al_c, al_r, dsk, ng, n, length)
        lfr, kx, cr = _fox_prep(dtf, dtft, bf_row, bf_col, n, length, tm_p)
        yc = _fox_flash(qtb, kb, kx, vtb, cr, n, length, tm_p)
        xp = _out_proj(xp, ya0, ya1, yb, yc, w_g, b_g, w_o, g_qm, mod_p, 1, l * n, tiles, tm_p)
        xp = _ffn(xp, g_pf, w1, w2, g_qf, mod_p, 1, l * n, tiles, tm_p)
        p_out[2].append(lfr[:, F_OFF:F_OFF + C_HEADS].reshape(n, tiles, C_HEADS, tm_p)
                        .transpose(2, 0, 1, 3).reshape(C_HEADS, n, length))
        p_out[3].append(hfr.reshape(n, A_NG, A_P))
        p_out[4].append(hfi.reshape(n, A_NG, A_P))
        p_out[5].append(xbc.reshape(n, length, B_XBC)[:, length - (B_CONV - 1):])
        p_out[6].append(ssm_p)

        ua, ub, z, xbc, q, _, _, _, kt_s, vt_s, dtf, dtft, k, v = _in_proj(
            xs, mod_s, nb, l, 1, g_pm, w_all, nb, 0, 1, None, True)
        ya, s5r, s5i = _s5_sample(ua, ub, state_s5_re[l].reshape(nb, -1), state_s5_im[l].reshape(nb, -1),
                                  s5abr[l].reshape(1, -1), s5abi[l].reshape(1, -1), bre, bim, cre, cim, d_row)
        yb, ssm_all = _ssd_sample(xbc, conv_t, z, dtf, ssm_in, ssm_all, l, cw, cb, dtb_c, al_c, expand, dsk, ng)
        yc, lfn = _fox_sample(page_table, l, q, k, v, dtf, bf_row, ck, cv, clf)
        xs = _out_proj(xs, ya[:, :A_CH // 2], ya[:, A_CH // 2:], yb, yc.reshape(nb, C_CH).astype(BF16),
                       w_g, b_g, w_o, g_qm, mod_s, nb, l, 1, nb)
        xs = _ffn(xs, g_pf, w1, w2, g_qf, mod_s, nb, l, 1, nb)
        s_out[0].append(kt_s.reshape(C_HEADS, C_HD, nb))
        s_out[1].append(vt_s.reshape(C_HEADS, C_HD, nb))
        s_out[2].append(lfn[:, :, F_OFF:F_OFF + C_HEADS])
        s_out[3].append(s5r.reshape(nb, A_NG, A_P))
        s_out[4].append(s5i.reshape(nb, A_NG, A_P))
        s_out[5].append(jnp.stack([conv_t[l, 1], conv_t[l, 2], xbc], axis=0))

    plf, ps5r, ps5i, pconv, pssm = [jnp.stack(a) for a in p_out[2:]]
    sk, sv, slf, ss5r, ss5i, sconv = [jnp.stack(a) for a in s_out[:6]]
    kt_all, vt_all = kv_all
    pk = kt_all.reshape(depth, n, C_HEADS, C_HD, length).transpose(0, 1, 4, 2, 3)
    pv = vt_all.reshape(depth, n, C_HEADS, C_HD, length).transpose(0, 1, 4, 2, 3)
    plf = plf.transpose(0, 2, 3, 1)
    sk = sk.transpose(0, 3, 1, 2)[:, :, None]
    sv = sv.transpose(0, 3, 1, 2)[:, :, None]
    sconv = sconv.transpose(0, 2, 1, 3)
    sssm = ssm_all.reshape(depth, nb, B_HEADS, B_HD, B_N)
    return (xp.reshape(n, length, D_MODEL), xs.reshape(nb, 1, D_MODEL), pk, pv, plf, ps5r, ps5i, pconv, pssm,
            sk, sv, slf, ss5r, ss5i, sconv, sssm)
```

```python
import functools
import math

import jax
import jax.numpy as jnp
from jax import lax
from jax.experimental import pallas as pl
from jax.experimental.pallas import tpu as pltpu

F32 = jnp.float32
BF16 = jnp.bfloat16
HI = lax.Precision.HIGHEST

D_MODEL = 1024
A_CH = 256
A_GROUP = 16
A_NG = 16
A_P = 64
S5_T = 16
S5_ROW = S5_T * A_GROUP
S5_LEVELS = 8
S5_ROWS_E = 2 * S5_T + S5_LEVELS
B_HD = 64
B_CH = 384
B_HEADS = 6
B_NG = 2
B_N = 128
B_CONV = 4
B_XBC = 896
B_CHUNK = 128
C_HD = 64
C_CH = 384
C_HEADS = 6
PAGE = 128
D_FF = 4096
EPS = 1e-6
LOG2E = math.log2(math.e)
FLASH_TK = 512
SSD_STEP = 512
PAD_W = 128
F_OFF = 6
N_ALL = A_CH + B_CH + B_XBC + 3 * C_CH + PAD_W
VMEM_LIMIT = 56 * 1024 * 1024


def _cp(*sem):
    return pltpu.CompilerParams(dimension_semantics=sem, vmem_limit_bytes=VMEM_LIMIT)


def _sds(shape, dtype):
    return jax.ShapeDtypeStruct(shape, dtype)


def _dot(a, b, precision=None):
    return jnp.dot(a, b, preferred_element_type=F32, precision=precision)


def _dot_nt(a, b, precision=None):
    return lax.dot_general(a, b, (((1,), (1,)), ((), ())), preferred_element_type=F32, precision=precision)


def _dot_tn(a, b, precision=None):
    return lax.dot_general(a, b, (((0,), (0,)), ((), ())), preferred_element_type=F32, precision=precision)


def _split3(x):
    hi = x.astype(BF16)
    r1 = x - hi.astype(F32)
    mid = r1.astype(BF16)
    lo = (r1 - mid.astype(F32)).astype(BF16)
    return hi, mid, lo


def _sel_dot(sel, x):
    sel = sel.astype(BF16)
    return sum(_dot(sel, part) for part in _split3(x))


def _dot_sel(x, sel):
    sel = sel.astype(BF16)
    return sum(_dot(part, sel) for part in _split3(x))


def _silu(x):
    return x * jax.nn.sigmoid(x)


def _softplus(x):
    return jnp.maximum(x, 0.0) + jnp.log1p(jnp.exp(-jnp.abs(x)))


def _log_sigmoid(x):
    return jnp.minimum(x, 0.0) - jnp.log1p(jnp.exp(-jnp.abs(x)))


def _gelu_tanh(x):
    return 0.5 * x * (1.0 + jnp.tanh(math.sqrt(2.0 / math.pi) * (x + 0.044715 * (x * x * x))))


def _rms(x, g):
    return x * lax.rsqrt(jnp.mean(x * x, axis=-1, keepdims=True) + EPS) * g


def _iota(shape, dim):
    return lax.broadcasted_iota(jnp.int32, shape, dim)


def _ada_body(c_ref, w_ref, b_ref, o_ref):
    s = _silu(c_ref[...]).astype(BF16)
    o_ref[...] = _dot(s, w_ref[...].astype(BF16)) + b_ref[...]


def _ada_mod(cond, w_ada, b_ada):
    depth = w_ada.shape[0]
    rows = cond.shape[0]
    tn = 1536
    return pl.pallas_call(
        _ada_body,
        out_shape=_sds((depth, rows, 6 * D_MODEL), F32),
        grid=(depth, 6 * D_MODEL // tn),
        in_specs=[pl.BlockSpec((rows, D_MODEL), lambda l, j: (0, 0)),
                  pl.BlockSpec((None, D_MODEL, tn), lambda l, j: (l, 0, j)),
                  pl.BlockSpec((None, 1, tn), lambda l, j: (l, 0, j))],
        out_specs=pl.BlockSpec((None, rows, tn), lambda l, j: (l, 0, j)),
        compiler_params=_cp("arbitrary", "arbitrary"),
        name="ada_mod",
    )(cond, w_ada, b_ada.reshape(depth, 1, 6 * D_MODEL))


def _lspec(arr, layer, **kw):
    return pl.BlockSpec((None,) + arr.shape[1:], lambda *_: (layer,) + (0,) * (arr.ndim - 1), **kw)


def _mod_spec(rows, base, tiles_per_group, piece):
    return pl.BlockSpec((None, rows, D_MODEL), lambda i: (base + i // tiles_per_group, 0, piece))


def _inproj_body(*refs, n_in, rows_out):
    x_ref, g_ref, sh_ref, sc_ref, w_ref = refs[:5]
    (ua_ref, ub_ref, z_ref, xbc_ref, q_ref, qtb_ref, kb_ref, vtb_ref, kt_ref, vt_ref,
     dtf_ref, dtft_ref) = refs[n_in:n_in + 12]
    h = _rms(x_ref[...], g_ref[...]) * (1.0 + sc_ref[...]) + sh_ref[...]
    hb = h.astype(BF16)

    def mm(a, b):
        return _dot(hb, w_ref[:, a:b])

    o = 0
    u = mm(o, o + A_CH)
    ua_ref[...] = u[:, :A_CH // 2]
    ub_ref[...] = u[:, A_CH // 2:]
    o += A_CH
    z_ref[...] = mm(o, o + B_CH)
    o += B_CH
    xbc_ref[...] = mm(o, o + B_XBC)
    o += B_XBC
    q = mm(o, o + C_CH) * (C_HD ** -0.5)
    q_ref[...] = q.astype(BF16)
    qtb_ref[...] = (q * LOG2E).T.astype(BF16).reshape(qtb_ref.shape)
    o += C_CH
    k = mm(o, o + C_CH)
    kt_ref[...] = k.T
    kb_ref[...] = k.astype(BF16)
    o += C_CH
    v = mm(o, o + C_CH)
    vt = v.T
    vt_ref[...] = vt
    vtb = vt.astype(BF16)
    tk = vtb_ref.shape[-1]
    for j in range(vtb_ref.shape[1]):
        vtb_ref[:, j] = vtb[:, j * tk:(j + 1) * tk].reshape(C_HEADS // 2, 128, tk)
    o += C_CH
    dtf = mm(o, o + PAD_W)
    dtf_ref[...] = dtf
    dtft_ref[...] = dtf.T[:16, :]
    if rows_out:
        k_ref, v_ref = refs[n_in + 12:n_in + 14]
        k_ref[...] = k
        v_ref[...] = v


def _in_proj(x, mod, mod_rows, mod_base, tiles_per_group, g_pre, w_all, wlayer, tm, layer, depth, kv_all,
             rows_out):
    t = x.shape[0]
    groups = t // (tm * tiles_per_group)
    glen = tm * tiles_per_group
    row = lambda w: pl.BlockSpec((tm, w), lambda i: (i, 0))
    kv_spec = pl.BlockSpec((None, None, C_CH, tm),
                           lambda i: (layer, i // tiles_per_group, 0, i % tiles_per_group))
    pair_t = _sds((groups, C_HEADS // 2, tiles_per_group, 128, tm), BF16)
    pair_spec = pl.BlockSpec((None, C_HEADS // 2, None, 128, tm),
                             lambda i: (i // tiles_per_group, 0, i % tiles_per_group, 0, 0))
    tk = min(FLASH_TK, tm)
    sub = tm // tk
    vt_t = _sds((groups, C_HEADS // 2, tiles_per_group * sub, 128, tk), BF16)
    vt_spec = pl.BlockSpec((None, C_HEADS // 2, sub, 128, tk),
                           lambda i: (i // tiles_per_group, 0, i % tiles_per_group, 0, 0))
    outs = [_sds((t, A_CH // 2), F32), _sds((t, A_CH // 2), F32), _sds((t, B_CH), F32), _sds((t, B_XBC), F32),
            _sds((t, C_CH), BF16), pair_t, _sds((t, C_CH), BF16), vt_t,
            _sds((depth, groups, C_CH, glen), F32), _sds((depth, groups, C_CH, glen), F32),
            _sds((t, PAD_W), F32), _sds((16, t), F32)]
    out_specs = [row(A_CH // 2), row(A_CH // 2), row(B_CH), row(B_XBC), row(C_CH), pair_spec, row(C_CH),
                 vt_spec, kv_spec, kv_spec, row(PAD_W), pl.BlockSpec((16, tm), lambda i: (0, i))]
    if rows_out:
        outs += [_sds((t, C_CH), F32), _sds((t, C_CH), F32)]
        out_specs += [row(C_CH), row(C_CH)]
    in_specs = [row(D_MODEL),
                _lspec(g_pre, wlayer),
                _mod_spec(mod_rows, mod_base, tiles_per_group, 0),
                _mod_spec(mod_rows, mod_base, tiles_per_group, 1),
                _lspec(w_all, wlayer)]
    args = [x, g_pre, mod, mod, w_all]
    aliases = {}
    if kv_all is not None:
        in_specs += [pl.BlockSpec(memory_space=pl.ANY), pl.BlockSpec(memory_space=pl.ANY)]
        args += list(kv_all)
        aliases = {5: 8, 6: 9}
    return pl.pallas_call(
        functools.partial(_inproj_body, n_in=len(args), rows_out=rows_out),
        out_shape=outs,
        grid=(t // tm,),
        in_specs=in_specs,
        out_specs=out_specs,
        input_output_aliases=aliases,
        compiler_params=_cp("arbitrary"),
        name="in_proj",
    )(*args)


def _s5_param_body(ex_ref, ldt_ref, arr_ref, air_ref, arc_ref, aic_ref, btr_ref, bti_ref, ctr_ref, cti_ref,
                   krow_ref, etr_ref, eti_ref, abr_ref, abi_ref, bbr_ref, bbi_ref):
    dt = jnp.exp(ldt_ref[...])
    ar, ai = arr_ref[...], air_ref[...]
    mag = jnp.exp(ar * dt)
    abr, abi = mag * jnp.cos(ai * dt), mag * jnp.sin(ai * dt)
    den = ar * ar + ai * ai
    nr, ni = abr - 1.0, abi
    fr, fi = (nr * ar + ni * ai) / den, (ni * ar - nr * ai) / den
    btr, bti = btr_ref[...], bti_ref[...]
    bbr, bbi = fr * btr - fi * bti, fr * bti + fi * btr
    abr_ref[...] = abr
    abi_ref[...] = abi
    bbr_ref[...] = bbr
    bbi_ref[...] = bbi

    def epow(k, a_r, a_i):
        m = jnp.exp(k * a_r * dt)
        th = k * a_i * dt
        return m * jnp.cos(th), m * jnp.sin(th)

    er, ei = epow(ex_ref[...], ar, ai)
    etr_ref[...] = er
    eti_ref[...] = ei
    arc, aic = arc_ref[...], aic_ref[...]
    lag = jnp.right_shift(_iota((1, S5_ROW), 1), 4).astype(F32)
    elr, eli = epow(lag, arc, aic)
    ctr, cti = ctr_ref[...], cti_ref[...]
    bm_re, bm_im = ctr * elr - cti * eli, ctr * eli + cti * elr
    krow_ref[...] = _dot(bbr, bm_re, HI) - _dot(bbi, bm_im, HI)


def _s5_params(a_re, a_im, log_dt, b_re, b_im, c_re, c_im):
    depth = a_re.shape[0]
    steps = jnp.arange(S5_T, dtype=F32)
    ex = jnp.concatenate([S5_T - 1.0 - steps, steps + 1.0,
                          S5_T * 2.0 ** jnp.arange(S5_LEVELS, dtype=F32)]).reshape(S5_ROWS_E, 1)
    arr, air = a_re.reshape(depth, A_NG, 1, A_P), a_im.reshape(depth, A_NG, 1, A_P)
    arc, aic = a_re.reshape(depth, A_NG, A_P, 1), a_im.reshape(depth, A_NG, A_P, 1)
    ldt = log_dt.reshape(depth, A_NG, 1, 1)
    bt = lambda b: jnp.swapaxes(b, 2, 3)
    ct = lambda c: jnp.tile(jnp.swapaxes(c, 2, 3), (1, 1, 1, S5_T))
    blk = lambda r, c: pl.BlockSpec((None, None, r, c), lambda l, g: (l, g, 0, 0))
    outs = [_sds((depth, A_NG, A_GROUP, S5_ROW), F32),
            _sds((depth, A_NG, S5_ROWS_E, A_P), F32), _sds((depth, A_NG, S5_ROWS_E, A_P), F32),
            _sds((depth, A_NG, 1, A_P), F32), _sds((depth, A_NG, 1, A_P), F32),
            _sds((depth, A_NG, A_GROUP, A_P), F32), _sds((depth, A_NG, A_GROUP, A_P), F32)]
    out_specs = [blk(A_GROUP, S5_ROW), blk(S5_ROWS_E, A_P), blk(S5_ROWS_E, A_P), blk(1, A_P), blk(1, A_P),
                 blk(A_GROUP, A_P), blk(A_GROUP, A_P)]
    return pl.pallas_call(
        _s5_param_body,
        out_shape=outs,
        grid=(depth, A_NG),
        in_specs=[pl.BlockSpec((S5_ROWS_E, 1), lambda l, g: (0, 0)),
                  blk(1, 1), blk(1, A_P), blk(1, A_P), blk(A_P, 1), blk(A_P, 1),
                  blk(A_GROUP, A_P), blk(A_GROUP, A_P), blk(A_P, S5_ROW), blk(A_P, S5_ROW)],
        out_specs=out_specs,
        compiler_params=_cp("arbitrary", "arbitrary"),
        name="s5_params",
    )(ex, ldt, arr, air, arc, aic, bt(b_re), bt(b_im), ct(c_re), ct(c_im))


def _s5_prompt_body(ua_ref, ub_ref, kst_ref, bre_ref, bim_ref, cre_ref, cim_ref, etr_ref, eti_ref, d_ref,
                    ya_ref, yb_ref, hfr_ref, hfi_ref, ucat_scr, hr_scr, hi_scr, *, chunks, levels):
    def slab(ref, s):
        return ref[pl.ds(s, chunks, stride=S5_T), :]

    def local_state(s, carry):
        x = jnp.concatenate([slab(ua_ref, s), slab(ub_ref, s)], axis=1).astype(BF16)
        bur, bui = _dot(x, bre_ref[...]), _dot(x, bim_ref[...])
        er, ei = etr_ref[pl.ds(s, 1), :], eti_ref[pl.ds(s, 1), :]
        hr_scr[...] += er * bur - ei * bui
        hi_scr[...] += er * bui + ei * bur
        return carry

    hr_scr[...] = jnp.zeros_like(hr_scr)
    hi_scr[...] = jnp.zeros_like(hi_scr)
    lax.fori_loop(0, S5_T, local_state, 0)
    for s in range(S5_T):
        ucat_scr[:, A_CH * s:A_CH * s + A_CH // 2] = slab(ua_ref, s).astype(BF16)
        ucat_scr[:, A_CH * s + A_CH // 2:A_CH * (s + 1)] = slab(ub_ref, s).astype(BF16)
    hr, hi = hr_scr[...], hi_scr[...]
    cidx = _iota((chunks, 1), 0)
    for k in range(levels):
        d = 1 << k
        keep = cidx >= d
        sr = jnp.where(keep, pltpu.roll(hr, d, axis=0), 0.0)
        si = jnp.where(keep, pltpu.roll(hi, d, axis=0), 0.0)
        lr, li = etr_ref[2 * S5_T + k:2 * S5_T + k + 1, :], eti_ref[2 * S5_T + k:2 * S5_T + k + 1, :]
        hr, hi = hr + lr * sr - li * si, hi + lr * si + li * sr
    hfr_ref[...] = hr[chunks - 1:chunks, :]
    hfi_ref[...] = hi[chunks - 1:chunks, :]
    keep = cidx >= 1
    hr_scr[...] = jnp.where(keep, pltpu.roll(hr, 1, axis=0), 0.0)
    hi_scr[...] = jnp.where(keep, pltpu.roll(hi, 1, axis=0), 0.0)
    half = A_CH // 2

    def outputs(t, span):
        er, ei = etr_ref[pl.ds(S5_T + t, 1), :], eti_ref[pl.ds(S5_T + t, 1), :]
        pr, pi = hr_scr[...], hi_scr[...]
        gr = (er * pr - ei * pi).astype(BF16)
        gi = (er * pi + ei * pr).astype(BF16)
        first = pl.multiple_of((S5_T - 1 - t) * A_CH, A_CH)
        y = (_dot(ucat_scr[:, :A_CH * span], kst_ref[pl.ds(first, A_CH * span), :])
             + _dot(gr, cre_ref[...]) - _dot(gi, cim_ref[...]))
        dd = d_ref[...]
        ya_ref[pl.ds(t, chunks, stride=S5_T), :] = y[:, :half] + dd[:, :half] * slab(ua_ref, t)
        yb_ref[pl.ds(t, chunks, stride=S5_T), :] = y[:, half:] + dd[:, half:] * slab(ub_ref, t)

    def first_half(t, carry):
        outputs(t, S5_T // 2)
        return carry

    def second_half(t, carry):
        outputs(t, S5_T)
        return carry

    lax.fori_loop(0, S5_T // 2, first_half, 0)
    lax.fori_loop(S5_T // 2, S5_T, second_half, 0)


def _s5_prompt(ua, ub, kstack, bre, bim, cre, cim, etr, eti, d, layer, n, length):
    chunks = length // S5_T
    levels = max(1, (chunks - 1).bit_length())
    assert chunks & (chunks - 1) == 0 and levels <= S5_LEVELS
    half = A_CH // 2
    w = A_NG * A_P
    tok = pl.BlockSpec((length, half), lambda i: (i, 0))
    const = lambda r, c: pl.BlockSpec((r, c), lambda i: (0, 0))
    fin = pl.BlockSpec((None, 1, w), lambda i: (i, 0, 0))
    return pl.pallas_call(
        functools.partial(_s5_prompt_body, chunks=chunks, levels=levels),
        out_shape=[_sds((n * length, half), F32), _sds((n * length, half), F32),
                   _sds((n, 1, w), F32), _sds((n, 1, w), F32)],
        grid=(n,),
        in_specs=[tok, tok] + [_lspec(a, layer) for a in (kstack, bre, bim, cre, cim, etr, eti, d)],
        out_specs=[tok, tok, fin, fin],
        scratch_shapes=[pltpu.VMEM((chunks, S5_T * A_CH), BF16), pltpu.VMEM((chunks, w), F32),
                        pltpu.VMEM((chunks, w), F32)],
        compiler_params=_cp("arbitrary"),
        name="s5_prompt",
    )(ua, ub, kstack, bre, bim, cre, cim, etr, eti, d)


def _s5_sample_body(ua_ref, ub_ref, hr_ref, hi_ref, abr_ref, abi_ref, bre_ref, bim_ref, cre_ref, cim_ref,
                    d_ref, y0_ref, y1_ref, sr_ref, si_ref):
    u = jnp.concatenate([ua_ref[...], ub_ref[...]], axis=1)
    ub = u.astype(BF16)
    hr, hi = hr_ref[...], hi_ref[...]
    abr, abi = abr_ref[...], abi_ref[...]
    sr = abr * hr - abi * hi + _dot(ub, bre_ref[...])
    si = abr * hi + abi * hr + _dot(ub, bim_ref[...])
    sr_ref[...] = sr
    si_ref[...] = si
    y = _dot(sr.astype(BF16), cre_ref[...]) - _dot(si.astype(BF16), cim_ref[...]) + d_ref[...] * u
    y0_ref[...] = y[:, :A_CH // 2]
    y1_ref[...] = y[:, A_CH // 2:]


def _s5_sample(ua, ub, hr, hi, abr, abi, bre, bim, cre, cim, d, layer):
    b = ua.shape[0]
    w = A_NG * A_P
    full = lambda a: pl.BlockSpec(a.shape, lambda i: (0,) * a.ndim)
    return pl.pallas_call(
        _s5_sample_body,
        out_shape=[_sds((b, A_CH // 2), F32), _sds((b, A_CH // 2), F32), _sds((b, w), F32), _sds((b, w), F32)],
        grid=(1,),
        in_specs=[full(ua), full(ub), full(hr), full(hi)]
        + [_lspec(a, layer) for a in (abr, abi, bre, bim, cre, cim, d)],
        out_specs=[pl.BlockSpec((b, A_CH // 2), lambda i: (0, 0)), pl.BlockSpec((b, A_CH // 2), lambda i: (0, 0)),
                   pl.BlockSpec((b, w), lambda i: (0, 0)), pl.BlockSpec((b, w), lambda i: (0, 0))],
        compiler_params=_cp("arbitrary"),
        name="s5_sample",
    )(ua, ub, hr, hi, abr, abi, bre, bim, cre, cim, d)


def _ssd_prompt_body(xbc_ref, z_ref, dtc_ref, dtr_ref, cprev_ref, h0_ref, cw_ref, cb_ref,
                     dtbc_ref, dtbr_ref, alc_ref, alr_ref, dsk_ref, ng_ref,
                     y_ref, hf_ref, ext_scr, h_scr):
    c = pl.program_id(1)
    q = B_CHUNK

    @pl.when(c == 0)
    def _():
        ext_scr[0:8, :] = cprev_ref[...]
        h_scr[...] = h0_ref[...]

    for sub in range(xbc_ref.shape[0] // q):
        _ssd_chunk(sub, xbc_ref, z_ref, dtc_ref, dtr_ref, cw_ref, cb_ref, dtbc_ref, dtbr_ref, alc_ref, alr_ref,
                   dsk_ref, ng_ref, y_ref, ext_scr, h_scr)

    @pl.when(c == pl.num_programs(1) - 1)
    def _():
        hf_ref[...] = h_scr[...]


def _ssd_chunk(sub, xbc_ref, z_ref, dtc_ref, dtr_ref, cw_ref, cb_ref, dtbc_ref, dtbr_ref, alc_ref, alr_ref,
               dsk_ref, ng_ref, y_ref, ext_scr, h_scr):
    q = B_CHUNK
    rows = slice(sub * q, (sub + 1) * q)
    x = xbc_ref[rows, :]
    ext_scr[8:8 + q, :] = x
    cw = cw_ref[...]
    conv = (cb_ref[...] + cw[3:4] * x + cw[2:3] * ext_scr[7:7 + q, :]
            + cw[1:2] * ext_scr[6:6 + q, :] + cw[0:1] * ext_scr[5:5 + q, :])
    ext_scr[0:8, :] = x[q - 8:q, :]
    xc = _silu(conv)

    r_i, c_i = _iota((q, q), 0), _iota((q, q), 1)
    causal = r_i >= c_i
    dtc = _softplus(dtc_ref[rows, :] + dtbc_ref[...])
    da_c = dtc * (-jnp.exp(alc_ref[...]))
    acc_c = _sel_dot(causal, da_c)
    dtr = _softplus(dtr_ref[:, rows] + dtbr_ref[...])
    da_r = dtr * (-jnp.exp(alr_ref[...]))
    acc_r = _dot_sel(da_r, r_i <= c_i)

    bm = [xc[:, B_CH + B_N * g:B_CH + B_N * (g + 1)].astype(BF16) for g in range(B_NG)]
    cm = [xc[:, B_CH + B_N * (B_NG + g):B_CH + B_N * (B_NG + g + 1)].astype(BF16) for g in range(B_NG)]
    cbm = [_dot_nt(cm[g], bm[g]) for g in range(B_NG)]
    lane = _iota((q, 128), 1)
    heads_per_group = B_HEADS // B_NG
    ys = []
    for j in range(B_HEADS // 2):
        xs_pair = xc[:, 128 * j:128 * (j + 1)]
        dt_pair = jnp.where(lane < B_HD, dtc[:, 2 * j:2 * j + 1], dtc[:, 2 * j + 1:2 * j + 2])
        xdt = xs_pair * dt_pair
        xdt_b = xdt.astype(BF16)
        hprev_b = h_scr[2 * j:2 * j + 2].reshape(2 * B_HD, B_N).astype(BF16)
        outs = []
        for e in range(2):
            h = 2 * j + e
            g = h // heads_per_group
            ac = acc_c[:, h:h + 1]
            ar = acc_r[h:h + 1, :]
            lm = jnp.exp(jnp.where(causal, ac - ar, -jnp.inf))
            yd = _dot((cbm[g] * lm).astype(BF16), xdt_b)
            yo = _dot_nt(cm[g], hprev_b) * jnp.exp(ac)
            outs.append(yd + yo)
            alast = acc_r[h:h + 1, q - 1:q]
            st = _dot_tn((xdt * jnp.exp(alast - ac)).astype(BF16), bm[g])
            h_scr[h] = jnp.exp(alast) * h_scr[h] + st[B_HD * e:B_HD * (e + 1), :]
        ys.append(jnp.where(lane < B_HD, outs[0], outs[1]) + dsk_ref[:, 128 * j:128 * (j + 1)] * xs_pair)
    y = jnp.concatenate(ys, axis=1)
    y_ref[rows, :] = _rms(y * _silu(z_ref[rows, :]), ng_ref[...]).astype(y_ref.dtype)


def _ssd_prompt(xbc, z, dtf, dtft, cprev, h0, cw, cb, dtb_c, dtb_r, al_c, al_r, dsk, ng, layer, n, length):
    tq = min(SSD_STEP, length)
    nc = length // tq
    tok = lambda w: pl.BlockSpec((tq, w), lambda i, c: (i * nc + c, 0))
    return pl.pallas_call(
        _ssd_prompt_body,
        out_shape=[_sds((n * length, B_CH), BF16), _sds((n, B_HEADS, B_HD, B_N), F32)],
        grid=(n, nc),
        in_specs=[tok(B_XBC), tok(B_CH), tok(PAD_W),
                  pl.BlockSpec((16, tq), lambda i, c: (0, i * nc + c)),
                  pl.BlockSpec((None, 8, B_XBC), lambda i, c: (i, 0, 0)),
                  pl.BlockSpec((None, B_HEADS, B_HD, B_N), lambda i, c: (i, 0, 0, 0))]
        + [_lspec(a, layer) for a in (cw, cb, dtb_c, dtb_r, al_c, al_r, dsk, ng)],
        out_specs=[tok(B_CH), pl.BlockSpec((None, B_HEADS, B_HD, B_N), lambda i, c: (i, 0, 0, 0))],
        scratch_shapes=[pltpu.VMEM((8 + B_CHUNK, B_XBC), F32), pltpu.VMEM((B_HEADS, B_HD, B_N), F32)],
        compiler_params=_cp("arbitrary", "arbitrary"),
        name="ssd_prompt",
    )(xbc, z, dtf, dtft, cprev, h0, cw, cb, dtb_c, dtb_r, al_c, al_r, dsk, ng)


def _ssd_sample_body(xbc_ref, p0_ref, p1_ref, p2_ref, z_ref, dtf_ref, h0_ref, cw_ref, cb_ref,
                     dtb_ref, al_ref, ex_ref, dsk_ref, ng_ref, *rest, tb):
    y_ref, hn_ref, y_scr = rest[-3:]
    x = xbc_ref[...]
    cw = cw_ref[...]
    conv = cb_ref[...] + cw[0:1] * p0_ref[...] + cw[1:2] * p1_ref[...] + cw[2:3] * p2_ref[...] + cw[3:4] * x
    xc = _silu(conv)
    dt = _softplus(dtf_ref[...] + dtb_ref[...])
    e = jnp.exp(dt * (-jnp.exp(al_ref[...])))
    ex = ex_ref[...]
    dt_e = _dot_sel(dt, ex)
    e_e = _dot_sel(e, ex)
    xs = xc[:, :B_CH]
    xdt = xs * dt_e
    half = B_CH // B_NG
    row8, lane8 = _iota((8, B_CH), 0), _iota((8, B_CH), 1)
    r8 = _iota((8, B_N), 0)
    lane1 = _iota((1, B_CH), 1)
    for b in range(tb):
        xr = xdt[b:b + 1, :]
        lhs = jnp.where(row8 == 0, e_e[b:b + 1, :],
                        jnp.where(((row8 == 1) & (lane8 < half)) | ((row8 == 2) & (lane8 >= half)), xr, 0.0))
        rhs_e = jnp.where(r8 == 0, 1.0, 0.0)
        rhs_s = jnp.where(r8 == 1, xc[b:b + 1, B_CH:B_CH + B_N],
                          jnp.where(r8 == 2, xc[b:b + 1, B_CH + B_N:B_CH + 2 * B_N], 0.0))
        hn = _dot_tn(lhs, rhs_e, HI) * h0_ref[b] + _dot_tn(lhs, rhs_s, HI)
        hn_ref[b] = hn
        cmat = jnp.where(r8 == 0, xc[b:b + 1, B_CH + 2 * B_N:B_CH + 3 * B_N],
                         jnp.where(r8 == 1, xc[b:b + 1, B_CH + 3 * B_N:B_CH + 4 * B_N], 0.0))
        yt = _dot_nt(cmat.astype(BF16), hn.astype(BF16))
        y_scr[b:b + 1, :] = jnp.where(lane1 < half, yt[0:1, :], yt[1:2, :])
    y = y_scr[...] + dsk_ref[...] * xs
    y_ref[...] = _rms(y * _silu(z_ref[...]), ng_ref[...]).astype(y_ref.dtype)


def _ssd_sample(xbc, conv_t, z, dtf, ssm_in, ssm_out, layer, cw, cb, dtb, al, ex, dsk, ng):
    depth, b = ssm_in.shape[0], xbc.shape[0]
    tb = 8
    row = lambda w: pl.BlockSpec((tb, w), lambda i: (i, 0))
    const = lambda r, w: pl.BlockSpec((r, w), lambda i: (0, 0))
    prev = lambda k: pl.BlockSpec((None, None, tb, B_XBC), lambda i: (layer, k, i, 0))
    st = pl.BlockSpec((None, tb, B_CH, B_N), lambda i: (layer, i, 0, 0))
    in_specs = [row(B_XBC), prev(0), prev(1), prev(2), row(B_CH), row(PAD_W), st,
                _lspec(cw, layer), _lspec(cb, layer), _lspec(dtb, layer), _lspec(al, layer),
                const(PAD_W, B_CH), _lspec(dsk, layer), _lspec(ng, layer)]
    args = [xbc, conv_t, conv_t, conv_t, z, dtf, ssm_in, cw, cb, dtb, al, ex, dsk, ng]
    aliases = {}
    if ssm_out is not None:
        in_specs.append(pl.BlockSpec(memory_space=pl.ANY))
        args.append(ssm_out)
        aliases = {len(args) - 1: 1}
    return pl.pallas_call(
        functools.partial(_ssd_sample_body, tb=tb),
        out_shape=[_sds((b, B_CH), BF16), _sds((depth, b, B_CH, B_N), F32)],
        grid=(b // tb,),
        in_specs=in_specs,
        out_specs=[row(B_CH), st],
        scratch_shapes=[pltpu.VMEM((tb, B_CH), F32)],
        input_output_aliases=aliases,
        compiler_params=_cp("arbitrary"),
        name="ssd_sample",
    )(*args)


def _fox_prep_body(fc_ref, fr_ref, bfr_ref, bfc_ref, place_ref, ones_ref, lfr_ref, kx_ref, cr_ref,
                   carry_c, carry_r):
    j = pl.program_id(1)
    tl = fc_ref.shape[0]

    @pl.when(j == 0)
    def _():
        carry_c[...] = jnp.zeros_like(carry_c)
        carry_r[...] = jnp.zeros_like(carry_r)

    r_i, c_i = _iota((tl, tl), 0), _iota((tl, tl), 1)
    lfc = _log_sigmoid(fc_ref[...] + bfr_ref[...])
    cc = _sel_dot(r_i >= c_i, lfc) + carry_c[0:1, :]
    carry_c[...] = jnp.broadcast_to(cc[tl - 1:tl, :], carry_c.shape)
    parts = _split3(-LOG2E * cc)
    kx = ones_ref[...] + sum(_dot(parts[i], place_ref[i]) for i in range(3))
    kx_ref[...] = kx.astype(BF16)
    lfr = _log_sigmoid(fr_ref[...] + bfc_ref[...])
    lfr_ref[...] = lfr
    cr = _dot_sel(lfr, r_i <= c_i) + carry_r[:, 0:1]
    cr_ref[...] = cr
    carry_r[...] = jnp.broadcast_to(cr[:, tl - 1:tl], carry_r.shape)


def _bias_slot(pair, head, term):
    return 16 * pair + 3 * head + term


def _fox_prep(dtf, dtft, bf_row, bf_col, layer, n, length, tl):
    nt = length // tl
    t = n * length
    place = jnp.zeros((3, PAD_W, 128), F32)
    ones = jnp.zeros((1, 128), F32)
    for pair in range(C_HEADS // 2):
        for term in range(3):
            ones = ones.at[0, _bias_slot(pair, 2, term)].set(1.0)
            for head in range(2):
                place = place.at[term, F_OFF + 2 * pair + head, _bias_slot(pair, head, term)].set(1.0)
    tok = pl.BlockSpec((tl, PAD_W), lambda i, j: (i * nt + j, 0))
    rowt = pl.BlockSpec((None, 16, tl), lambda i, j: (i * nt + j, 0, 0))
    return pl.pallas_call(
        _fox_prep_body,
        out_shape=[_sds((n * nt, 16, tl), F32), _sds((t, 128), BF16), _sds((n * nt, 16, tl), F32)],
        grid=(n, nt),
        in_specs=[tok, pl.BlockSpec((16, tl), lambda i, j: (0, i * nt + j)),
                  _lspec(bf_row, layer), _lspec(bf_col, layer),
                  pl.BlockSpec((3, PAD_W, 128), lambda i, j: (0, 0, 0)),
                  pl.BlockSpec((1, 128), lambda i, j: (0, 0))],
        out_specs=[rowt, pl.BlockSpec((tl, 128), lambda i, j: (i * nt + j, 0)), rowt],
        scratch_shapes=[pltpu.VMEM((8, PAD_W), F32), pltpu.VMEM((16, 128), F32)],
        compiler_params=_cp("arbitrary", "arbitrary"),
        name="fox_prep",
    )(dtf, dtft, bf_row, bf_col, place.astype(BF16), ones)


def _fox_flash_body(qt_ref, k_ref, kx_ref, vt_ref, cr_ref, o_ref, qa_scr, m_scr, l_scr, acc_scr):
    hp, qi = pl.program_id(1), pl.program_id(2)
    tq = qt_ref.shape[1]
    tk = vt_ref.shape[2]
    row = _iota((128, tq), 0)
    qt = qt_ref[...]
    base = 16 * hp
    for e in range(2):
        cq = _split3(LOG2E * cr_ref[pl.ds(F_OFF + 2 * hp + e, 1), :])
        qx = jnp.where((row >= base + 3 * e) & (row < base + 3 * e + 3), 1.0, 0.0).astype(BF16)
        for term in range(3):
            qx = jnp.where(row == base + 6 + term, cq[term], qx)
        qa_scr[e, 0:128, :] = jnp.where((row < C_HD) == (e == 0), qt, jnp.zeros_like(qt))
        qa_scr[e, 128:256, :] = qx
    m_scr[...] = jnp.full(m_scr.shape, -jnp.inf, F32)
    l_scr[...] = jnp.zeros_like(l_scr)
    acc_scr[...] = jnp.zeros_like(acc_scr)

    def block(ki, first_key):
        rows = pl.ds(pl.multiple_of(ki * tk, tk), tk)
        ka = jnp.concatenate([k_ref[rows, :], kx_ref[rows, :]], axis=1)
        vt = vt_ref[ki]
        scores = [_dot(ka, qa_scr[e]) for e in range(2)]
        probs, alphas = [], []
        for e in range(2):
            s = scores[e]
            if first_key is not None:
                s = jnp.where(first_key + _iota((tk, tq), 0) <= _iota((tk, tq), 1), s, -jnp.inf)
            m_old = m_scr[e]
            m_new = jnp.maximum(m_old, jnp.max(s, axis=0, keepdims=True))
            alpha = jnp.exp2(m_old - m_new)
            p = jnp.exp2(s - m_new)
            l_scr[e] = alpha * l_scr[e] + jnp.sum(p, axis=0, keepdims=True)
            m_scr[e] = m_new
            probs.append(p.astype(BF16))
            alphas.append(alpha)
        for e in range(2):
            acc_scr[e] = alphas[e] * acc_scr[e] + _dot(vt, probs[e])

    def below_diagonal(ki, carry):
        block(ki, None)
        return carry

    sub = tq // tk
    lax.fori_loop(0, qi * sub, below_diagonal, 0)
    for j in range(sub):
        block(qi * sub + j, j * tk)
    ot = jnp.where(row < C_HD, acc_scr[0] / l_scr[0], acc_scr[1] / l_scr[1])
    o_ref[...] = ot.T.astype(o_ref.dtype)


def _fox_flash(qt, k, kx, vt, cr, n, length, tq):
    nq = length // tq
    nk, tk = vt.shape[2], vt.shape[4]
    return pl.pallas_call(
        _fox_flash_body,
        out_shape=_sds((n * length, C_CH), BF16),
        grid=(n, C_HEADS // 2, nq),
        in_specs=[pl.BlockSpec((None, None, None, 128, tq), lambda i, h, a: (i, h, a, 0, 0)),
                  pl.BlockSpec((length, 128), lambda i, h, a: (i, h)),
                  pl.BlockSpec((length, 128), lambda i, h, a: (i, 0)),
                  pl.BlockSpec((None, None, nk, 128, tk), lambda i, h, a: (i, h, 0, 0, 0)),
                  pl.BlockSpec((None, 16, tq), lambda i, h, a: (i * nq + a, 0, 0))],
        out_specs=pl.BlockSpec((tq, 128), lambda i, h, a: (i * nq + a, h)),
        scratch_shapes=[pltpu.VMEM((2, 256, tq), BF16), pltpu.VMEM((2, 1, tq), F32),
                        pltpu.VMEM((2, 1, tq), F32), pltpu.VMEM((2, 128, tq), F32)],
        compiler_params=_cp("arbitrary", "arbitrary", "arbitrary"),
        name="fox_flash",
    )(qt, k, kx, vt, cr)


def _fox_sample_body(pt_ref, q_ref, kn_ref, vn_ref, fn_ref, bf_ref, *refs, pages):
    k_refs = refs[0:pages]
    v_refs = refs[pages:2 * pages]
    f_refs = refs[2 * pages:3 * pages]
    o_ref, lf_ref, lf_scr, s_scr = refs[3 * pages:3 * pages + 4]
    i = pl.program_id(0)
    row, lane = _iota((8, C_CH), 0), _iota((8, C_CH), 1)
    own = jnp.right_shift(lane, 6) == row
    qf = q_ref[...].astype(F32)
    qm = jnp.where(own, qf, 0.0)
    qcol = [jnp.broadcast_to(qf[:, 128 * j:128 * (j + 1)], (128, 128)).T for j in range(C_CH // 128)]
    lfn = _log_sigmoid(fn_ref[...] + bf_ref[...])
    lf_ref[...] = lfn
    carry = jnp.sum(jnp.where(_iota((8, PAD_W), 1) == F_OFF + _iota((8, PAD_W), 0), lfn, 0.0),
                    axis=1, keepdims=True)
    for p in range(pages):
        r = jnp.bitwise_and(pt_ref[i, p], 7)
        for h in range(C_HEADS):
            lf_scr[8 * p + h:8 * p + h + 1, :] = f_refs[p][h, pl.ds(r, 1), :]
        lf_scr[8 * p + C_HEADS:8 * p + 8, :] = jnp.zeros((8 - C_HEADS, PAGE), F32)
    lf_all = lf_scr[...]
    later = _iota((PAGE, PAGE), 0) > _iota((PAGE, PAGE), 1)
    suffix = _dot_sel(lf_all, later)
    total = jnp.sum(lf_all, axis=1, keepdims=True)
    carries = [None] * pages
    for p in reversed(range(pages)):
        carries[p] = carry
        carry = carry + total[8 * p:8 * p + 8, :]
    for p in range(pages):
        for h in range(C_HEADS):
            qh = qcol[h // 2][C_HD * (h % 2):C_HD * (h % 2 + 1), :]
            s_scr[8 * p + h:8 * p + h + 1, :] = jnp.sum(k_refs[p][h] * qh, axis=0, keepdims=True)
        s_scr[8 * p + C_HEADS:8 * p + 8, :] = jnp.zeros((8 - C_HEADS, PAGE), F32)
    s = s_scr[...] + suffix + jnp.concatenate(carries, axis=0)
    s_new = jnp.sum(qm * kn_ref[...], axis=1, keepdims=True)
    row_max = jnp.max(s, axis=1, keepdims=True)
    m = s_new
    for p in range(pages):
        m = jnp.maximum(m, row_max[8 * p:8 * p + 8, :])
    pr = jnp.exp(s - jnp.concatenate([m] * pages, axis=0))
    p_new = jnp.exp(s_new - m)
    row_sum = jnp.sum(pr, axis=1, keepdims=True)
    l = p_new
    for p in range(pages):
        l = l + row_sum[8 * p:8 * p + 8, :]
    outs = []
    for j in range(C_HEADS // 2):
        pair = []
        for h in (2 * j, 2 * j + 1):
            acc = jnp.zeros((C_HD, PAGE), F32)
            for p in range(pages):
                acc = acc + v_refs[p][h] * pr[8 * p + h:8 * p + h + 1, :]
            pair.append(acc)
        outs.append(jnp.sum(jnp.concatenate(pair, axis=0).T, axis=0, keepdims=True))
    spread = lambda c: jnp.sum(jnp.where(own, c, 0.0), axis=0, keepdims=True)
    o_ref[...] = (jnp.concatenate(outs, axis=1) + spread(p_new) * vn_ref[...]) / spread(l)


def _fox_sample(page_table, layer, q, k_new, v_new, f_new, bf_row, cache_kt, cache_vt, cache_lft):
    b = q.shape[0]
    pages = page_table.shape[1]
    tok = lambda w: pl.BlockSpec((None, 1, w), lambda i, pt: (i, 0, 0))
    page = lambda p: pl.BlockSpec((None, None, C_HEADS, C_HD, PAGE), lambda i, pt: (layer, pt[i, p], 0, 0, 0))
    lfpage = lambda p: pl.BlockSpec((None, C_HEADS, 8, PAGE), lambda i, pt: (layer, 0, pt[i, p] // 8, 0))
    in_specs = ([tok(C_CH), tok(C_CH), tok(C_CH), tok(PAD_W), _lspec(bf_row, layer)]
                + [page(p) for p in range(pages)] + [page(p) for p in range(pages)]
                + [lfpage(p) for p in range(pages)])
    grid_spec = pltpu.PrefetchScalarGridSpec(
        num_scalar_prefetch=1, grid=(b,), in_specs=in_specs,
        out_specs=[tok(C_CH), tok(PAD_W)],
        scratch_shapes=[pltpu.VMEM((8 * pages, PAGE), F32), pltpu.VMEM((8 * pages, PAGE), F32)])
    return pl.pallas_call(
        functools.partial(_fox_sample_body, pages=pages),
        out_shape=[_sds((b, 1, C_CH), F32), _sds((b, 1, PAD_W), F32)],
        grid_spec=grid_spec,
        compiler_params=_cp("arbitrary"),
        name="fox_sample",
    )(page_table, q.reshape(b, 1, C_CH), k_new.reshape(b, 1, C_CH), v_new.reshape(b, 1, C_CH),
      f_new.reshape(b, 1, PAD_W), bf_row,
      *([cache_kt] * pages), *([cache_vt] * pages), *([cache_lft] * pages))


def _outproj_body(x_ref, ya0_ref, ya1_ref, yb_ref, yc_ref, wg_ref, bg_ref, wo_ref, gp_ref, g1_ref, o_ref):
    y = _gelu_tanh(jnp.concatenate([ya0_ref[...], ya1_ref[...]], axis=1))
    gate = jax.nn.sigmoid(_dot(y.astype(BF16), wg_ref[...]) + bg_ref[...])
    ya = (y * gate).astype(BF16)
    mix = (_dot(ya, wo_ref[0:A_CH, :]) + _dot(yb_ref[...], wo_ref[A_CH:A_CH + B_CH, :])
           + _dot(yc_ref[...], wo_ref[A_CH + B_CH:, :]))
    o_ref[...] = x_ref[...] + g1_ref[...] * _rms(mix, gp_ref[...])


def _out_proj(x, ya0, ya1, yb, yc, w_glu, b_glu, w_out, g_post, layer, mod, mod_rows, mod_base,
              tiles_per_group, tm):
    t = x.shape[0]
    row = lambda w: pl.BlockSpec((tm, w), lambda i: (i, 0))
    return pl.pallas_call(
        _outproj_body,
        out_shape=_sds((t, D_MODEL), F32),
        grid=(t // tm,),
        in_specs=[row(D_MODEL), row(A_CH // 2), row(A_CH // 2), row(B_CH), row(C_CH), _lspec(w_glu, layer),
                  _lspec(b_glu, layer), _lspec(w_out, layer), _lspec(g_post, layer),
                  _mod_spec(mod_rows, mod_base, tiles_per_group, 2)],
        out_specs=row(D_MODEL),
        compiler_params=_cp("arbitrary"),
        name="out_proj",
    )(x, ya0, ya1, yb, yc, w_glu, b_glu, w_out, g_post, mod)


def _ffn_body(x_ref, gpre_ref, sh_ref, sc_ref, w1_ref, w2_ref, gpost_ref, g2_ref, o_ref, *, tf):
    x = x_ref[...]
    hb = (_rms(x, gpre_ref[...]) * (1.0 + sc_ref[...]) + sh_ref[...]).astype(BF16)
    acc = jnp.zeros(x.shape, F32)
    for j in range(D_FF // tf):
        a = jnp.maximum(_dot(hb, w1_ref[:, j * tf:(j + 1) * tf]), 0.0)
        acc = acc + _dot((a * a).astype(BF16), w2_ref[j * tf:(j + 1) * tf, :])
    o_ref[...] = x + g2_ref[...] * _rms(acc, gpost_ref[...])


def _ffn(x, g_pre, w1, w2, g_post, layer, mod, mod_rows, mod_base, tiles_per_group, tm):
    t = x.shape[0]
    row = pl.BlockSpec((tm, D_MODEL), lambda i: (i, 0))
    return pl.pallas_call(
        functools.partial(_ffn_body, tf=1024),
        out_shape=_sds((t, D_MODEL), F32),
        grid=(t // tm,),
        in_specs=[row, _lspec(g_pre, layer), _mod_spec(mod_rows, mod_base, tiles_per_group, 3),
                  _mod_spec(mod_rows, mod_base, tiles_per_group, 4),
                  _lspec(w1, layer, pipeline_mode=pl.Buffered(1)),
                  _lspec(w2, layer, pipeline_mode=pl.Buffered(1)),
                  _lspec(g_post, layer), _mod_spec(mod_rows, mod_base, tiles_per_group, 5)],
        out_specs=row,
        compiler_params=_cp("arbitrary"),
        name="ffn",
    )(x, g_pre, mod, mod, w1, w2, g_post, mod)


def _pad_lanes(v, offset, width=PAD_W):
    return jnp.zeros((1, width), F32).at[0, offset:offset + v.shape[0]].set(v)


def _pad_rows(v, offset, rows=16):
    return jnp.zeros((rows, 1), F32).at[offset:offset + v.shape[0], 0].set(v)


def kernel(x_prompt, x_sample, c_prompt, c_sample, cache_k, cache_v, cache_logf, page_table, state_s5_re, state_s5_im, state_conv, state_ssm, w_ada, b_ada, g_pre_mix, g_post_mix, g_pre_ffn, g_post_ffn, w_in, w_out, s5_a_re, s5_a_im, s5_log_dt, s5_b_re, s5_b_im, s5_c_re, s5_c_im, s5_d, s5_w_glu, s5_b_glu, ssd_conv_w, ssd_conv_b, ssd_dt_bias, ssd_a_log, ssd_d, ssd_norm_g, fox_b_f, w_ff1, w_ff2):
    depth = w_ada.shape[0]
    n, length, _ = x_prompt.shape
    nb = x_sample.shape[0]
    t = n * length
    n_pool = cache_k.shape[1]
    chunks = length // S5_T
    tm_p = 512
    assert x_sample.shape[1] == 1 and length % tm_p == 0 and nb % 8 == 0

    rows = -(-(n + nb) // 8) * 8
    cond = jnp.concatenate([c_prompt, c_sample, jnp.zeros((rows - n - nb, D_MODEL), F32)], axis=0)
    mod = _ada_mod(cond, w_ada, b_ada)
    mod_p = mod[:, :n].reshape(depth * n, 1, 6 * D_MODEL)
    mod_s = mod[:, n:n + nb]

    s5krow, s5etr, s5eti, s5abr, s5abi, s5bbr, s5bbi = _s5_params(
        s5_a_re, s5_a_im, s5_log_dt, s5_b_re, s5_b_im, s5_c_re, s5_c_im)
    eye_g = jnp.eye(A_NG, dtype=F32)
    w_s5 = A_NG * A_P
    expand = (jnp.arange(PAD_W)[:, None] == jnp.arange(B_CH)[None, :] // B_HD).astype(F32)

    ck = cache_k.transpose(0, 1, 3, 4, 2)
    cv = cache_v.transpose(0, 1, 3, 4, 2)
    clf = cache_logf.transpose(0, 3, 1, 2)
    conv_t = state_conv.transpose(0, 2, 1, 3)
    ssm_in = state_ssm.reshape(depth, nb, B_CH, B_N)

    xp = x_prompt.reshape(t, D_MODEL)
    xs = x_sample.reshape(nb, D_MODEL)
    zeros_conv = jnp.zeros((n, 8, B_XBC), F32)
    zeros_ssm = jnp.zeros((n, B_HEADS, B_HD, B_N), F32)
    p_out = [[] for _ in range(7)]
    s_out = [[] for _ in range(7)]
    kv_all = None
    ssm_all = None

    o_dt = A_CH + B_CH + B_XBC
    o_f = o_dt + B_HEADS + 3 * C_CH
    w_all = jnp.concatenate(
        [w_in[..., :o_dt], w_in[..., o_dt + B_HEADS:o_f], w_in[..., o_dt:o_dt + B_HEADS], w_in[..., o_f:],
         jnp.zeros((depth, D_MODEL, PAD_W - B_HEADS - C_HEADS), F32)], axis=2).astype(BF16)
    w_o, w_g, w1, w2 = (w.astype(BF16) for w in (w_out, s5_w_glu, w_ff1, w_ff2))
    rows3 = lambda v: v.reshape(depth, 1, -1)
    g_pm, g_qm, g_pf, g_qf = rows3(g_pre_mix), rows3(g_post_mix), rows3(g_pre_ffn), rows3(g_post_ffn)
    b_g, cw, cb, ng, d_row = rows3(s5_b_glu), ssd_conv_w, rows3(ssd_conv_b), rows3(ssd_norm_g), rows3(s5_d)
    dsk = rows3(jnp.repeat(ssd_d, B_HD, axis=1))
    lanes = lambda v, off: jnp.zeros((depth, 1, PAD_W), F32).at[:, 0, off:off + v.shape[1]].set(v)
    subl = lambda v, off: jnp.zeros((depth, 16, 1), F32).at[:, off:off + v.shape[1], 0].set(v)
    dtb_c, dtb_r = lanes(ssd_dt_bias, 0), subl(ssd_dt_bias, 0)
    al_c, al_r = lanes(ssd_a_log, 0), subl(ssd_a_log, 0)
    bf_row, bf_col = lanes(fox_b_f, F_OFF), subl(fox_b_f, F_OFF)
    bre = jnp.einsum("lgcp,gh->lgchp", s5bbr, eye_g).reshape(depth, A_CH, w_s5).astype(BF16)
    bim = jnp.einsum("lgcp,gh->lgchp", s5bbi, eye_g).reshape(depth, A_CH, w_s5).astype(BF16)
    cre = jnp.einsum("lgcp,gh->lgphc", s5_c_re, eye_g).reshape(depth, w_s5, A_CH).astype(BF16)
    cim = jnp.einsum("lgcp,gh->lgphc", s5_c_im, eye_g).reshape(depth, w_s5, A_CH).astype(BF16)
    krow_rev = s5krow.reshape(depth, A_NG, A_GROUP, S5_T, A_GROUP)[:, :, :, ::-1, :]
    klag = jnp.einsum("lgakc,gh->lkgahc", krow_rev, eye_g).reshape(depth, S5_T * A_CH, A_CH)
    kstack = jnp.concatenate([klag, jnp.zeros_like(klag)], axis=1).astype(BF16)
    etr = s5etr.transpose(0, 2, 1, 3).reshape(depth, S5_ROWS_E, w_s5)
    eti = s5eti.transpose(0, 2, 1, 3).reshape(depth, S5_ROWS_E, w_s5)
    abr, abi = s5abr.reshape(depth, 1, w_s5), s5abi.reshape(depth, 1, w_s5)

    for l in range(depth):
        tiles = length // tm_p
        ua, ub, z, xbc, _, qtb, kb, vtb, kt_all, vt_all, dtf, dtft = _in_proj(
            xp, mod_p, 1, l * n, tiles, g_pm, w_all, l, tm_p, l, depth, kv_all, False)
        kv_all = (kt_all, vt_all)
        ya0, ya1, hfr, hfi = _s5_prompt(ua, ub, kstack, bre, bim, cre, cim, etr, eti, d_row, l, n, length)
        yb, ssm_p = _ssd_prompt(xbc, z, dtf, dtft, zeros_conv, zeros_ssm, cw, cb, dtb_c, dtb_r,
                                al_c, al_r, dsk, ng, l, n, length)
        lfr, kx, cr = _fox_prep(dtf, dtft, bf_row, bf_col, l, n, length, tm_p)
        yc = _fox_flash(qtb, kb, kx, vtb, cr, n, length, tm_p)
        xp = _out_proj(xp, ya0, ya1, yb, yc, w_g, b_g, w_o, g_qm, l, mod_p, 1, l * n, tiles, tm_p)
        xp = _ffn(xp, g_pf, w1, w2, g_qf, l, mod_p, 1, l * n, tiles, tm_p)
        p_out[2].append(lfr[:, F_OFF:F_OFF + C_HEADS].reshape(n, tiles, C_HEADS, tm_p)
                        .transpose(2, 0, 1, 3).reshape(C_HEADS, n, length))
        p_out[3].append(hfr.reshape(n, A_NG, A_P))
        p_out[4].append(hfi.reshape(n, A_NG, A_P))
        p_out[5].append(xbc.reshape(n, length, B_XBC)[:, length - (B_CONV - 1):])
        p_out[6].append(ssm_p)

        ua, ub, z, xbc, q, _, _, _, kt_s, vt_s, dtf, dtft, k, v = _in_proj(
            xs, mod_s, nb, l, 1, g_pm, w_all, l, nb, 0, 1, None, True)
        ya0, ya1, s5r, s5i = _s5_sample(ua, ub, state_s5_re[l].reshape(nb, -1), state_s5_im[l].reshape(nb, -1),
                                        abr, abi, bre, bim, cre, cim, d_row, l)
        yb, ssm_all = _ssd_sample(xbc, conv_t, z, dtf, ssm_in, ssm_all, l, cw, cb, dtb_c, al_c, expand, dsk, ng)
        yc, lfn = _fox_sample(page_table, l, q, k, v, dtf, bf_row, ck, cv, clf)
        xs = _out_proj(xs, ya0, ya1, yb, yc.reshape(nb, C_CH).astype(BF16),
                       w_g, b_g, w_o, g_qm, l, mod_s, nb, l, 1, nb)
        xs = _ffn(xs, g_pf, w1, w2, g_qf, l, mod_s, nb, l, 1, nb)
        s_out[0].append(kt_s.reshape(C_HEADS, C_HD, nb))
        s_out[1].append(vt_s.reshape(C_HEADS, C_HD, nb))
        s_out[2].append(lfn[:, :, F_OFF:F_OFF + C_HEADS])
        s_out[3].append(s5r.reshape(nb, A_NG, A_P))
        s_out[4].append(s5i.reshape(nb, A_NG, A_P))
        s_out[5].append(jnp.stack([conv_t[l, 1], conv_t[l, 2], xbc], axis=0))

    plf, ps5r, ps5i, pconv, pssm = [jnp.stack(a) for a in p_out[2:]]
    sk, sv, slf, ss5r, ss5i, sconv = [jnp.stack(a) for a in s_out[:6]]
    kt_all, vt_all = kv_all
    pk = kt_all.reshape(depth, n, C_HEADS, C_HD, length).transpose(0, 1, 4, 2, 3)
    pv = vt_all.reshape(depth, n, C_HEADS, C_HD, length).transpose(0, 1, 4, 2, 3)
    plf = plf.transpose(0, 2, 3, 1)
    sk = sk.transpose(0, 3, 1, 2)[:, :, None]
    sv = sv.transpose(0, 3, 1, 2)[:, :, None]
    sconv = sconv.transpose(0, 2, 1, 3)
    sssm = ssm_all.reshape(depth, nb, B_HEADS, B_HD, B_N)
    return (xp.reshape(n, length, D_MODEL), xs.reshape(nb, 1, D_MODEL), pk, pv, plf, ps5r, ps5i, pconv, pssm,
            sk, sv, slf, ss5r, ss5i, sconv, sssm)
```

```python
import functools
import math

import jax
import jax.numpy as jnp
from jax import lax
from jax.experimental import pallas as pl
from jax.experimental.pallas import tpu as pltpu

F32 = jnp.float32
BF16 = jnp.bfloat16
HI = lax.Precision.HIGHEST

D_MODEL = 1024
A_CH = 256
A_GROUP = 16
A_NG = 16
A_P = 64
S5_T = 16
S5_ROW = S5_T * A_GROUP
S5_LEVELS = 8
S5_ROWS_E = 2 * S5_T + S5_LEVELS
B_HD = 64
B_CH = 384
B_HEADS = 6
B_NG = 2
B_N = 128
B_CONV = 4
B_XBC = 896
B_CHUNK = 128
C_HD = 64
C_CH = 384
C_HEADS = 6
PAGE = 128
D_FF = 4096
EPS = 1e-6
LOG2E = math.log2(math.e)
FLASH_TK = 512
SSD_STEP = 128
PAD_W = 128
F_OFF = 6
N_ALL = A_CH + B_CH + B_XBC + 3 * C_CH + PAD_W
VMEM_LIMIT = 56 * 1024 * 1024


def _cp(*sem):
    return pltpu.CompilerParams(dimension_semantics=sem, vmem_limit_bytes=VMEM_LIMIT)


def _sds(shape, dtype):
    return jax.ShapeDtypeStruct(shape, dtype)


def _dot(a, b, precision=None):
    return jnp.dot(a, b, preferred_element_type=F32, precision=precision)


def _dot_nt(a, b, precision=None):
    return lax.dot_general(a, b, (((1,), (1,)), ((), ())), preferred_element_type=F32, precision=precision)


def _dot_tn(a, b, precision=None):
    return lax.dot_general(a, b, (((0,), (0,)), ((), ())), preferred_element_type=F32, precision=precision)


def _split3(x):
    hi = x.astype(BF16)
    r1 = x - hi.astype(F32)
    mid = r1.astype(BF16)
    lo = (r1 - mid.astype(F32)).astype(BF16)
    return hi, mid, lo


def _sel_dot(sel, x):
    sel = sel.astype(BF16)
    return sum(_dot(sel, part) for part in _split3(x))


def _dot_sel(x, sel):
    sel = sel.astype(BF16)
    return sum(_dot(part, sel) for part in _split3(x))


def _silu(x):
    return x * jax.nn.sigmoid(x)


def _softplus(x):
    return jnp.maximum(x, 0.0) + jnp.log1p(jnp.exp(-jnp.abs(x)))


def _log_sigmoid(x):
    return jnp.minimum(x, 0.0) - jnp.log1p(jnp.exp(-jnp.abs(x)))


def _gelu_tanh(x):
    return 0.5 * x * (1.0 + jnp.tanh(math.sqrt(2.0 / math.pi) * (x + 0.044715 * (x * x * x))))


def _rms(x, g):
    return x * lax.rsqrt(jnp.mean(x * x, axis=-1, keepdims=True) + EPS) * g


def _iota(shape, dim):
    return lax.broadcasted_iota(jnp.int32, shape, dim)


def _ada_body(c_ref, w_ref, b_ref, o_ref):
    s = _silu(c_ref[...]).astype(BF16)
    o_ref[...] = _dot(s, w_ref[...].astype(BF16)) + b_ref[...]


def _ada_mod(cond, w_ada, b_ada):
    depth = w_ada.shape[0]
    rows = cond.shape[0]
    tn = 1536
    return pl.pallas_call(
        _ada_body,
        out_shape=_sds((depth, rows, 6 * D_MODEL), F32),
        grid=(depth, 6 * D_MODEL // tn),
        in_specs=[pl.BlockSpec((rows, D_MODEL), lambda l, j: (0, 0)),
                  pl.BlockSpec((None, D_MODEL, tn), lambda l, j: (l, 0, j)),
                  pl.BlockSpec((None, 1, tn), lambda l, j: (l, 0, j))],
        out_specs=pl.BlockSpec((None, rows, tn), lambda l, j: (l, 0, j)),
        compiler_params=_cp("arbitrary", "arbitrary"),
        name="ada_mod",
    )(cond, w_ada, b_ada.reshape(depth, 1, 6 * D_MODEL))


def _lspec(arr, layer, **kw):
    return pl.BlockSpec((None,) + arr.shape[1:], lambda *_: (layer,) + (0,) * (arr.ndim - 1), **kw)


def _mod_spec(rows, base, tiles_per_group, piece):
    return pl.BlockSpec((None, rows, D_MODEL), lambda i: (base + i // tiles_per_group, 0, piece))


def _inproj_body(*refs, n_in, rows_out):
    x_ref, g_ref, sh_ref, sc_ref, w_ref = refs[:5]
    (ua_ref, ub_ref, z_ref, xbc_ref, q_ref, qtb_ref, kb_ref, vtb_ref, kt_ref, vt_ref,
     dtf_ref, dtft_ref) = refs[n_in:n_in + 12]
    h = _rms(x_ref[...], g_ref[...]) * (1.0 + sc_ref[...]) + sh_ref[...]
    hb = h.astype(BF16)

    def mm(a, b):
        return _dot(hb, w_ref[:, a:b])

    o = 0
    u = mm(o, o + A_CH)
    ua_ref[...] = u[:, :A_CH // 2]
    ub_ref[...] = u[:, A_CH // 2:]
    o += A_CH
    z_ref[...] = mm(o, o + B_CH)
    o += B_CH
    xbc_ref[...] = mm(o, o + B_XBC)
    o += B_XBC
    q = mm(o, o + C_CH) * (C_HD ** -0.5)
    q_ref[...] = q.astype(BF16)
    qtb_ref[...] = (q * LOG2E).T.astype(BF16).reshape(qtb_ref.shape)
    o += C_CH
    k = mm(o, o + C_CH)
    kt_ref[...] = k.T
    kb_ref[...] = k.astype(BF16)
    o += C_CH
    v = mm(o, o + C_CH)
    vt = v.T
    vt_ref[...] = vt
    vtb = vt.astype(BF16)
    tk = vtb_ref.shape[-1]
    for j in range(vtb_ref.shape[1]):
        vtb_ref[:, j] = vtb[:, j * tk:(j + 1) * tk].reshape(C_HEADS // 2, 128, tk)
    o += C_CH
    dtf = mm(o, o + PAD_W)
    dtf_ref[...] = dtf
    dtft_ref[...] = dtf.T[:16, :]
    if rows_out:
        k_ref, v_ref = refs[n_in + 12:n_in + 14]
        k_ref[...] = k
        v_ref[...] = v


def _in_proj(x, mod, mod_rows, mod_base, tiles_per_group, g_pre, w_all, wlayer, tm, layer, depth, kv_all,
             rows_out):
    t = x.shape[0]
    groups = t // (tm * tiles_per_group)
    glen = tm * tiles_per_group
    row = lambda w: pl.BlockSpec((tm, w), lambda i: (i, 0))
    kv_spec = pl.BlockSpec((None, None, C_CH, tm),
                           lambda i: (layer, i // tiles_per_group, 0, i % tiles_per_group))
    pair_t = _sds((groups, C_HEADS // 2, tiles_per_group, 128, tm), BF16)
    pair_spec = pl.BlockSpec((None, C_HEADS // 2, None, 128, tm),
                             lambda i: (i // tiles_per_group, 0, i % tiles_per_group, 0, 0))
    tk = min(FLASH_TK, tm)
    sub = tm // tk
    vt_t = _sds((groups, C_HEADS // 2, tiles_per_group * sub, 128, tk), BF16)
    vt_spec = pl.BlockSpec((None, C_HEADS // 2, sub, 128, tk),
                           lambda i: (i // tiles_per_group, 0, i % tiles_per_group, 0, 0))
    outs = [_sds((t, A_CH // 2), F32), _sds((t, A_CH // 2), F32), _sds((t, B_CH), F32), _sds((t, B_XBC), F32),
            _sds((t, C_CH), BF16), pair_t, _sds((t, C_CH), BF16), vt_t,
            _sds((depth, groups, C_CH, glen), F32), _sds((depth, groups, C_CH, glen), F32),
            _sds((t, PAD_W), F32), _sds((16, t), F32)]
    out_specs = [row(A_CH // 2), row(A_CH // 2), row(B_CH), row(B_XBC), row(C_CH), pair_spec, row(C_CH),
                 vt_spec, kv_spec, kv_spec, row(PAD_W), pl.BlockSpec((16, tm), lambda i: (0, i))]
    if rows_out:
        outs += [_sds((t, C_CH), F32), _sds((t, C_CH), F32)]
        out_specs += [row(C_CH), row(C_CH)]
    in_specs = [row(D_MODEL),
                _lspec(g_pre, wlayer),
                _mod_spec(mod_rows, mod_base, tiles_per_group, 0),
                _mod_spec(mod_rows, mod_base, tiles_per_group, 1),
                _lspec(w_all, wlayer)]
    args = [x, g_pre, mod, mod, w_all]
    aliases = {}
    if kv_all is not None:
        in_specs += [pl.BlockSpec(memory_space=pl.ANY), pl.BlockSpec(memory_space=pl.ANY)]
        args += list(kv_all)
        aliases = {5: 8, 6: 9}
    return pl.pallas_call(
        functools.partial(_inproj_body, n_in=len(args), rows_out=rows_out),
        out_shape=outs,
        grid=(t // tm,),
        in_specs=in_specs,
        out_specs=out_specs,
        input_output_aliases=aliases,
        compiler_params=_cp("arbitrary"),
        name="in_proj",
    )(*args)


def _s5_param_body(ex_ref, ldt_ref, arr_ref, air_ref, arc_ref, aic_ref, btr_ref, bti_ref, ctr_ref, cti_ref,
                   krow_ref, etr_ref, eti_ref, abr_ref, abi_ref, bbr_ref, bbi_ref):
    dt = jnp.exp(ldt_ref[...])
    ar, ai = arr_ref[...], air_ref[...]
    mag = jnp.exp(ar * dt)
    abr, abi = mag * jnp.cos(ai * dt), mag * jnp.sin(ai * dt)
    den = ar * ar + ai * ai
    nr, ni = abr - 1.0, abi
    fr, fi = (nr * ar + ni * ai) / den, (ni * ar - nr * ai) / den
    btr, bti = btr_ref[...], bti_ref[...]
    bbr, bbi = fr * btr - fi * bti, fr * bti + fi * btr
    abr_ref[...] = abr
    abi_ref[...] = abi
    bbr_ref[...] = bbr
    bbi_ref[...] = bbi

    def epow(k, a_r, a_i):
        m = jnp.exp(k * a_r * dt)
        th = k * a_i * dt
        return m * jnp.cos(th), m * jnp.sin(th)

    er, ei = epow(ex_ref[...], ar, ai)
    etr_ref[...] = er
    eti_ref[...] = ei
    arc, aic = arc_ref[...], aic_ref[...]
    lag = jnp.right_shift(_iota((1, S5_ROW), 1), 4).astype(F32)
    elr, eli = epow(lag, arc, aic)
    ctr, cti = ctr_ref[...], cti_ref[...]
    bm_re, bm_im = ctr * elr - cti * eli, ctr * eli + cti * elr
    krow_ref[...] = _dot(bbr, bm_re, HI) - _dot(bbi, bm_im, HI)


def _s5_params(a_re, a_im, log_dt, b_re, b_im, c_re, c_im):
    depth = a_re.shape[0]
    steps = jnp.arange(S5_T, dtype=F32)
    ex = jnp.concatenate([S5_T - 1.0 - steps, steps + 1.0,
                          S5_T * 2.0 ** jnp.arange(S5_LEVELS, dtype=F32)]).reshape(S5_ROWS_E, 1)
    arr, air = a_re.reshape(depth, A_NG, 1, A_P), a_im.reshape(depth, A_NG, 1, A_P)
    arc, aic = a_re.reshape(depth, A_NG, A_P, 1), a_im.reshape(depth, A_NG, A_P, 1)
    ldt = log_dt.reshape(depth, A_NG, 1, 1)
    bt = lambda b: jnp.swapaxes(b, 2, 3)
    ct = lambda c: jnp.tile(jnp.swapaxes(c, 2, 3), (1, 1, 1, S5_T))
    blk = lambda r, c: pl.BlockSpec((None, None, r, c), lambda l, g: (l, g, 0, 0))
    outs = [_sds((depth, A_NG, A_GROUP, S5_ROW), F32),
            _sds((depth, A_NG, S5_ROWS_E, A_P), F32), _sds((depth, A_NG, S5_ROWS_E, A_P), F32),
            _sds((depth, A_NG, 1, A_P), F32), _sds((depth, A_NG, 1, A_P), F32),
            _sds((depth, A_NG, A_GROUP, A_P), F32), _sds((depth, A_NG, A_GROUP, A_P), F32)]
    out_specs = [blk(A_GROUP, S5_ROW), blk(S5_ROWS_E, A_P), blk(S5_ROWS_E, A_P), blk(1, A_P), blk(1, A_P),
                 blk(A_GROUP, A_P), blk(A_GROUP, A_P)]
    return pl.pallas_call(
        _s5_param_body,
        out_shape=outs,
        grid=(depth, A_NG),
        in_specs=[pl.BlockSpec((S5_ROWS_E, 1), lambda l, g: (0, 0)),
                  blk(1, 1), blk(1, A_P), blk(1, A_P), blk(A_P, 1), blk(A_P, 1),
                  blk(A_GROUP, A_P), blk(A_GROUP, A_P), blk(A_P, S5_ROW), blk(A_P, S5_ROW)],
        out_specs=out_specs,
        compiler_params=_cp("arbitrary", "arbitrary"),
        name="s5_params",
    )(ex, ldt, arr, air, arc, aic, bt(b_re), bt(b_im), ct(c_re), ct(c_im))


def _s5_prompt_body(ua_ref, ub_ref, kst_ref, bre_ref, bim_ref, cre_ref, cim_ref, etr_ref, eti_ref, d_ref,
                    ya_ref, yb_ref, hfr_ref, hfi_ref, ucat_scr, hr_scr, hi_scr, *, chunks, levels):
    def slab(ref, s):
        return ref[pl.ds(s, chunks, stride=S5_T), :]

    def local_state(s, carry):
        x = jnp.concatenate([slab(ua_ref, s), slab(ub_ref, s)], axis=1).astype(BF16)
        bur, bui = _dot(x, bre_ref[...]), _dot(x, bim_ref[...])
        er, ei = etr_ref[pl.ds(s, 1), :], eti_ref[pl.ds(s, 1), :]
        hr_scr[...] += er * bur - ei * bui
        hi_scr[...] += er * bui + ei * bur
        return carry

    hr_scr[...] = jnp.zeros_like(hr_scr)
    hi_scr[...] = jnp.zeros_like(hi_scr)
    lax.fori_loop(0, S5_T, local_state, 0)
    for s in range(S5_T):
        ucat_scr[:, A_CH * s:A_CH * s + A_CH // 2] = slab(ua_ref, s).astype(BF16)
        ucat_scr[:, A_CH * s + A_CH // 2:A_CH * (s + 1)] = slab(ub_ref, s).astype(BF16)
    hr, hi = hr_scr[...], hi_scr[...]
    cidx = _iota((chunks, 1), 0)
    for k in range(levels):
        d = 1 << k
        keep = cidx >= d
        sr = jnp.where(keep, pltpu.roll(hr, d, axis=0), 0.0)
        si = jnp.where(keep, pltpu.roll(hi, d, axis=0), 0.0)
        lr, li = etr_ref[2 * S5_T + k:2 * S5_T + k + 1, :], eti_ref[2 * S5_T + k:2 * S5_T + k + 1, :]
        hr, hi = hr + lr * sr - li * si, hi + lr * si + li * sr
    hfr_ref[...] = hr[chunks - 1:chunks, :]
    hfi_ref[...] = hi[chunks - 1:chunks, :]
    keep = cidx >= 1
    hr_scr[...] = jnp.where(keep, pltpu.roll(hr, 1, axis=0), 0.0)
    hi_scr[...] = jnp.where(keep, pltpu.roll(hi, 1, axis=0), 0.0)
    half = A_CH // 2

    def outputs(t, span):
        er, ei = etr_ref[pl.ds(S5_T + t, 1), :], eti_ref[pl.ds(S5_T + t, 1), :]
        pr, pi = hr_scr[...], hi_scr[...]
        gr = (er * pr - ei * pi).astype(BF16)
        gi = (er * pi + ei * pr).astype(BF16)
        first = pl.multiple_of((S5_T - 1 - t) * A_CH, A_CH)
        y = (_dot(ucat_scr[:, :A_CH * span], kst_ref[pl.ds(first, A_CH * span), :])
             + _dot(gr, cre_ref[...]) - _dot(gi, cim_ref[...]))
        dd = d_ref[...]
        ya_ref[pl.ds(t, chunks, stride=S5_T), :] = y[:, :half] + dd[:, :half] * slab(ua_ref, t)
        yb_ref[pl.ds(t, chunks, stride=S5_T), :] = y[:, half:] + dd[:, half:] * slab(ub_ref, t)

    def first_half(t, carry):
        outputs(t, S5_T // 2)
        return carry

    def second_half(t, carry):
        outputs(t, S5_T)
        return carry

    lax.fori_loop(0, S5_T // 2, first_half, 0)
    lax.fori_loop(S5_T // 2, S5_T, second_half, 0)


def _s5_prompt(ua, ub, kstack, bre, bim, cre, cim, etr, eti, d, layer, n, length):
    chunks = length // S5_T
    levels = max(1, (chunks - 1).bit_length())
    assert chunks & (chunks - 1) == 0 and levels <= S5_LEVELS
    half = A_CH // 2
    w = A_NG * A_P
    tok = pl.BlockSpec((length, half), lambda i: (i, 0))
    const = lambda r, c: pl.BlockSpec((r, c), lambda i: (0, 0))
    fin = pl.BlockSpec((None, 1, w), lambda i: (i, 0, 0))
    return pl.pallas_call(
        functools.partial(_s5_prompt_body, chunks=chunks, levels=levels),
        out_shape=[_sds((n * length, half), F32), _sds((n * length, half), F32),
                   _sds((n, 1, w), F32), _sds((n, 1, w), F32)],
        grid=(n,),
        in_specs=[tok, tok] + [_lspec(a, layer) for a in (kstack, bre, bim, cre, cim, etr, eti, d)],
        out_specs=[tok, tok, fin, fin],
        scratch_shapes=[pltpu.VMEM((chunks, S5_T * A_CH), BF16), pltpu.VMEM((chunks, w), F32),
                        pltpu.VMEM((chunks, w), F32)],
        compiler_params=_cp("arbitrary"),
        name="s5_prompt",
    )(ua, ub, kstack, bre, bim, cre, cim, etr, eti, d)


def _s5_sample_body(ua_ref, ub_ref, hr_ref, hi_ref, abr_ref, abi_ref, bre_ref, bim_ref, cre_ref, cim_ref,
                    d_ref, y0_ref, y1_ref, sr_ref, si_ref):
    u = jnp.concatenate([ua_ref[...], ub_ref[...]], axis=1)
    ub = u.astype(BF16)
    hr, hi = hr_ref[...], hi_ref[...]
    abr, abi = abr_ref[...], abi_ref[...]
    sr = abr * hr - abi * hi + _dot(ub, bre_ref[...])
    si = abr * hi + abi * hr + _dot(ub, bim_ref[...])
    sr_ref[...] = sr
    si_ref[...] = si
    y = _dot(sr.astype(BF16), cre_ref[...]) - _dot(si.astype(BF16), cim_ref[...]) + d_ref[...] * u
    y0_ref[...] = y[:, :A_CH // 2]
    y1_ref[...] = y[:, A_CH // 2:]


def _s5_sample(ua, ub, hr, hi, abr, abi, bre, bim, cre, cim, d, layer):
    b = ua.shape[0]
    w = A_NG * A_P
    full = lambda a: pl.BlockSpec(a.shape, lambda i: (0,) * a.ndim)
    return pl.pallas_call(
        _s5_sample_body,
        out_shape=[_sds((b, A_CH // 2), F32), _sds((b, A_CH // 2), F32), _sds((b, w), F32), _sds((b, w), F32)],
        grid=(1,),
        in_specs=[full(ua), full(ub), full(hr), full(hi)]
        + [_lspec(a, layer) for a in (abr, abi, bre, bim, cre, cim, d)],
        out_specs=[pl.BlockSpec((b, A_CH // 2), lambda i: (0, 0)), pl.BlockSpec((b, A_CH // 2), lambda i: (0, 0)),
                   pl.BlockSpec((b, w), lambda i: (0, 0)), pl.BlockSpec((b, w), lambda i: (0, 0))],
        compiler_params=_cp("arbitrary"),
        name="s5_sample",
    )(ua, ub, hr, hi, abr, abi, bre, bim, cre, cim, d)


def _ssd_prompt_body(xbc_ref, z_ref, dtc_ref, dtr_ref, cprev_ref, h0_ref, cw_ref, cb_ref,
                     dtbc_ref, dtbr_ref, alc_ref, alr_ref, dsk_ref, ng_ref,
                     y_ref, hf_ref, ext_scr, h_scr):
    c = pl.program_id(1)
    q = B_CHUNK

    @pl.when(c == 0)
    def _():
        ext_scr[0:8, :] = cprev_ref[...]
        h_scr[...] = h0_ref[...]

    for sub in range(xbc_ref.shape[0] // q):
        _ssd_chunk(sub, xbc_ref, z_ref, dtc_ref, dtr_ref, cw_ref, cb_ref, dtbc_ref, dtbr_ref, alc_ref, alr_ref,
                   dsk_ref, ng_ref, y_ref, ext_scr, h_scr)

    @pl.when(c == pl.num_programs(1) - 1)
    def _():
        hf_ref[...] = h_scr[...]


def _ssd_chunk(sub, xbc_ref, z_ref, dtc_ref, dtr_ref, cw_ref, cb_ref, dtbc_ref, dtbr_ref, alc_ref, alr_ref,
               dsk_ref, ng_ref, y_ref, ext_scr, h_scr):
    q = B_CHUNK
    rows = slice(sub * q, (sub + 1) * q)
    x = xbc_ref[rows, :]
    ext_scr[8:8 + q, :] = x
    cw = cw_ref[...]
    conv = (cb_ref[...] + cw[3:4] * x + cw[2:3] * ext_scr[7:7 + q, :]
            + cw[1:2] * ext_scr[6:6 + q, :] + cw[0:1] * ext_scr[5:5 + q, :])
    ext_scr[0:8, :] = x[q - 8:q, :]
    xc = _silu(conv)

    r_i, c_i = _iota((q, q), 0), _iota((q, q), 1)
    causal = r_i >= c_i
    dtc = _softplus(dtc_ref[rows, :] + dtbc_ref[...])
    da_c = dtc * (-jnp.exp(alc_ref[...]))
    acc_c = _sel_dot(causal, da_c)
    dtr = _softplus(dtr_ref[:, rows] + dtbr_ref[...])
    da_r = dtr * (-jnp.exp(alr_ref[...]))
    acc_r = _dot_sel(da_r, r_i <= c_i)

    bm = [xc[:, B_CH + B_N * g:B_CH + B_N * (g + 1)].astype(BF16) for g in range(B_NG)]
    cm = [xc[:, B_CH + B_N * (B_NG + g):B_CH + B_N * (B_NG + g + 1)].astype(BF16) for g in range(B_NG)]
    cbm = [_dot_nt(cm[g], bm[g]) for g in range(B_NG)]
    lane = _iota((q, 128), 1)
    heads_per_group = B_HEADS // B_NG
    ys = []
    for j in range(B_HEADS // 2):
        xs_pair = xc[:, 128 * j:128 * (j + 1)]
        dt_pair = jnp.where(lane < B_HD, dtc[:, 2 * j:2 * j + 1], dtc[:, 2 * j + 1:2 * j + 2])
        xdt = xs_pair * dt_pair
        xdt_b = xdt.astype(BF16)
        hprev_b = h_scr[2 * j:2 * j + 2].reshape(2 * B_HD, B_N).astype(BF16)
        outs = []
        for e in range(2):
            h = 2 * j + e
            g = h // heads_per_group
            ac = acc_c[:, h:h + 1]
            ar = acc_r[h:h + 1, :]
            lm = jnp.exp(jnp.where(causal, ac - ar, -jnp.inf))
            yd = _dot((cbm[g] * lm).astype(BF16), xdt_b)
            yo = _dot_nt(cm[g], hprev_b) * jnp.exp(ac)
            outs.append(yd + yo)
            alast = acc_r[h:h + 1, q - 1:q]
            st = _dot_tn((xdt * jnp.exp(alast - ac)).astype(BF16), bm[g])
            h_scr[h] = jnp.exp(alast) * h_scr[h] + st[B_HD * e:B_HD * (e + 1), :]
        ys.append(jnp.where(lane < B_HD, outs[0], outs[1]) + dsk_ref[:, 128 * j:128 * (j + 1)] * xs_pair)
    y = jnp.concatenate(ys, axis=1)
    y_ref[rows, :] = _rms(y * _silu(z_ref[rows, :]), ng_ref[...]).astype(y_ref.dtype)


def _ssd_prompt(xbc, z, dtf, dtft, cprev, h0, cw, cb, dtb_c, dtb_r, al_c, al_r, dsk, ng, layer, n, length):
    tq = min(SSD_STEP, length)
    nc = length // tq
    tok = lambda w: pl.BlockSpec((tq, w), lambda i, c: (i * nc + c, 0))
    return pl.pallas_call(
        _ssd_prompt_body,
        out_shape=[_sds((n * length, B_CH), BF16), _sds((n, B_HEADS, B_HD, B_N), F32)],
        grid=(n, nc),
        in_specs=[tok(B_XBC), tok(B_CH), tok(PAD_W),
                  pl.BlockSpec((16, tq), lambda i, c: (0, i * nc + c)),
                  pl.BlockSpec((None, 8, B_XBC), lambda i, c: (i, 0, 0)),
                  pl.BlockSpec((None, B_HEADS, B_HD, B_N), lambda i, c: (i, 0, 0, 0))]
        + [_lspec(a, layer) for a in (cw, cb, dtb_c, dtb_r, al_c, al_r, dsk, ng)],
        out_specs=[tok(B_CH), pl.BlockSpec((None, B_HEADS, B_HD, B_N), lambda i, c: (i, 0, 0, 0))],
        scratch_shapes=[pltpu.VMEM((8 + B_CHUNK, B_XBC), F32), pltpu.VMEM((B_HEADS, B_HD, B_N), F32)],
        compiler_params=_cp("arbitrary", "arbitrary"),
        name="ssd_prompt",
    )(xbc, z, dtf, dtft, cprev, h0, cw, cb, dtb_c, dtb_r, al_c, al_r, dsk, ng)


def _ssd_sample_body(xbc_ref, p0_ref, p1_ref, p2_ref, z_ref, dtf_ref, h0_ref, cw_ref, cb_ref,
                     dtb_ref, al_ref, ex_ref, dsk_ref, ng_ref, *rest, tb):
    y_ref, hn_ref, y_scr = rest[-3:]
    x = xbc_ref[...]
    cw = cw_ref[...]
    conv = cb_ref[...] + cw[0:1] * p0_ref[...] + cw[1:2] * p1_ref[...] + cw[2:3] * p2_ref[...] + cw[3:4] * x
    xc = _silu(conv)
    dt = _softplus(dtf_ref[...] + dtb_ref[...])
    e = jnp.exp(dt * (-jnp.exp(al_ref[...])))
    ex = ex_ref[...]
    dt_e = _dot_sel(dt, ex)
    e_e = _dot_sel(e, ex)
    xs = xc[:, :B_CH]
    xdt = xs * dt_e
    half = B_CH // B_NG
    row8, lane8 = _iota((8, B_CH), 0), _iota((8, B_CH), 1)
    r8 = _iota((8, B_N), 0)
    lane1 = _iota((1, B_CH), 1)
    for b in range(tb):
        xr = xdt[b:b + 1, :]
        lhs = jnp.where(row8 == 0, e_e[b:b + 1, :],
                        jnp.where(((row8 == 1) & (lane8 < half)) | ((row8 == 2) & (lane8 >= half)), xr, 0.0))
        rhs_e = jnp.where(r8 == 0, 1.0, 0.0)
        rhs_s = jnp.where(r8 == 1, xc[b:b + 1, B_CH:B_CH + B_N],
                          jnp.where(r8 == 2, xc[b:b + 1, B_CH + B_N:B_CH + 2 * B_N], 0.0))
        hn = _dot_tn(lhs, rhs_e, HI) * h0_ref[b] + _dot_tn(lhs, rhs_s, HI)
        hn_ref[b] = hn
        cmat = jnp.where(r8 == 0, xc[b:b + 1, B_CH + 2 * B_N:B_CH + 3 * B_N],
                         jnp.where(r8 == 1, xc[b:b + 1, B_CH + 3 * B_N:B_CH + 4 * B_N], 0.0))
        yt = _dot_nt(cmat.astype(BF16), hn.astype(BF16))
        y_scr[b:b + 1, :] = jnp.where(lane1 < half, yt[0:1, :], yt[1:2, :])
    y = y_scr[...] + dsk_ref[...] * xs
    y_ref[...] = _rms(y * _silu(z_ref[...]), ng_ref[...]).astype(y_ref.dtype)


def _ssd_sample(xbc, conv_t, z, dtf, ssm_in, ssm_out, layer, cw, cb, dtb, al, ex, dsk, ng):
    depth, b = ssm_in.shape[0], xbc.shape[0]
    tb = 8
    row = lambda w: pl.BlockSpec((tb, w), lambda i: (i, 0))
    const = lambda r, w: pl.BlockSpec((r, w), lambda i: (0, 0))
    prev = lambda k: pl.BlockSpec((None, None, tb, B_XBC), lambda i: (layer, k, i, 0))
    st = pl.BlockSpec((None, tb, B_CH, B_N), lambda i: (layer, i, 0, 0))
    in_specs = [row(B_XBC), prev(0), prev(1), prev(2), row(B_CH), row(PAD_W), st,
                _lspec(cw, layer), _lspec(cb, layer), _lspec(dtb, layer), _lspec(al, layer),
                const(PAD_W, B_CH), _lspec(dsk, layer), _lspec(ng, layer)]
    args = [xbc, conv_t, conv_t, conv_t, z, dtf, ssm_in, cw, cb, dtb, al, ex, dsk, ng]
    aliases = {}
    if ssm_out is not None:
        in_specs.append(pl.BlockSpec(memory_space=pl.ANY))
        args.append(ssm_out)
        aliases = {len(args) - 1: 1}
    return pl.pallas_call(
        functools.partial(_ssd_sample_body, tb=tb),
        out_shape=[_sds((b, B_CH), BF16), _sds((depth, b, B_CH, B_N), F32)],
        grid=(b // tb,),
        in_specs=in_specs,
        out_specs=[row(B_CH), st],
        scratch_shapes=[pltpu.VMEM((tb, B_CH), F32)],
        input_output_aliases=aliases,
        compiler_params=_cp("arbitrary"),
        name="ssd_sample",
    )(*args)


def _fox_prep_body(fc_ref, fr_ref, bfr_ref, bfc_ref, place_ref, ones_ref, lfr_ref, kx_ref, cr_ref,
                   carry_c, carry_r):
    j = pl.program_id(1)
    tl = fc_ref.shape[0]

    @pl.when(j == 0)
    def _():
        carry_c[...] = jnp.zeros_like(carry_c)
        carry_r[...] = jnp.zeros_like(carry_r)

    r_i, c_i = _iota((tl, tl), 0), _iota((tl, tl), 1)
    lfc = _log_sigmoid(fc_ref[...] + bfr_ref[...])
    cc = _sel_dot(r_i >= c_i, lfc) + carry_c[0:1, :]
    carry_c[...] = jnp.broadcast_to(cc[tl - 1:tl, :], carry_c.shape)
    parts = _split3(-LOG2E * cc)
    kx = ones_ref[...] + sum(_dot(parts[i], place_ref[i]) for i in range(3))
    kx_ref[...] = kx.astype(BF16)
    lfr = _log_sigmoid(fr_ref[...] + bfc_ref[...])
    lfr_ref[...] = lfr
    cr = _dot_sel(lfr, r_i <= c_i) + carry_r[:, 0:1]
    cr_ref[...] = cr
    carry_r[...] = jnp.broadcast_to(cr[:, tl - 1:tl], carry_r.shape)


def _bias_slot(pair, head, term):
    return 16 * pair + 3 * head + term


def _fox_prep(dtf, dtft, bf_row, bf_col, layer, n, length, tl):
    nt = length // tl
    t = n * length
    place = jnp.zeros((3, PAD_W, 128), F32)
    ones = jnp.zeros((1, 128), F32)
    for pair in range(C_HEADS // 2):
        for term in range(3):
            ones = ones.at[0, _bias_slot(pair, 2, term)].set(1.0)
            for head in range(2):
                place = place.at[term, F_OFF + 2 * pair + head, _bias_slot(pair, head, term)].set(1.0)
    tok = pl.BlockSpec((tl, PAD_W), lambda i, j: (i * nt + j, 0))
    rowt = pl.BlockSpec((None, 16, tl), lambda i, j: (i * nt + j, 0, 0))
    return pl.pallas_call(
        _fox_prep_body,
        out_shape=[_sds((n * nt, 16, tl), F32), _sds((t, 128), BF16), _sds((n * nt, 16, tl), F32)],
        grid=(n, nt),
        in_specs=[tok, pl.BlockSpec((16, tl), lambda i, j: (0, i * nt + j)),
                  _lspec(bf_row, layer), _lspec(bf_col, layer),
                  pl.BlockSpec((3, PAD_W, 128), lambda i, j: (0, 0, 0)),
                  pl.BlockSpec((1, 128), lambda i, j: (0, 0))],
        out_specs=[rowt, pl.BlockSpec((tl, 128), lambda i, j: (i * nt + j, 0)), rowt],
        scratch_shapes=[pltpu.VMEM((8, PAD_W), F32), pltpu.VMEM((16, 128), F32)],
        compiler_params=_cp("arbitrary", "arbitrary"),
        name="fox_prep",
    )(dtf, dtft, bf_row, bf_col, place.astype(BF16), ones)


def _fox_flash_body(qt_ref, k_ref, kx_ref, vt_ref, cr_ref, o_ref, qa_scr, m_scr, l_scr, acc_scr):
    hp, qi = pl.program_id(1), pl.program_id(2)
    tq = qt_ref.shape[1]
    tk = vt_ref.shape[2]
    row = _iota((128, tq), 0)
    qt = qt_ref[...]
    base = 16 * hp
    for e in range(2):
        cq = _split3(LOG2E * cr_ref[pl.ds(F_OFF + 2 * hp + e, 1), :])
        qx = jnp.where((row >= base + 3 * e) & (row < base + 3 * e + 3), 1.0, 0.0).astype(BF16)
        for term in range(3):
            qx = jnp.where(row == base + 6 + term, cq[term], qx)
        qa_scr[e, 0:128, :] = jnp.where((row < C_HD) == (e == 0), qt, jnp.zeros_like(qt))
        qa_scr[e, 128:256, :] = qx
    m_scr[...] = jnp.full(m_scr.shape, -jnp.inf, F32)
    l_scr[...] = jnp.zeros_like(l_scr)
    acc_scr[...] = jnp.zeros_like(acc_scr)

    def block(ki, first_key):
        rows = pl.ds(pl.multiple_of(ki * tk, tk), tk)
        ka = jnp.concatenate([k_ref[rows, :], kx_ref[rows, :]], axis=1)
        vt = vt_ref[ki]
        scores = [_dot(ka, qa_scr[e]) for e in range(2)]
        probs, alphas = [], []
        for e in range(2):
            s = scores[e]
            if first_key is not None:
                s = jnp.where(first_key + _iota((tk, tq), 0) <= _iota((tk, tq), 1), s, -jnp.inf)
            m_old = m_scr[e]
            m_new = jnp.maximum(m_old, jnp.max(s, axis=0, keepdims=True))
            alpha = jnp.exp2(m_old - m_new)
            p = jnp.exp2(s - m_new)
            l_scr[e] = alpha * l_scr[e] + jnp.sum(p, axis=0, keepdims=True)
            m_scr[e] = m_new
            probs.append(p.astype(BF16))
            alphas.append(alpha)
        for e in range(2):
            acc_scr[e] = alphas[e] * acc_scr[e] + _dot(vt, probs[e])

    def below_diagonal(ki, carry):
        block(ki, None)
        return carry

    sub = tq // tk
    lax.fori_loop(0, qi * sub, below_diagonal, 0)
    for j in range(sub):
        block(qi * sub + j, j * tk)
    ot = jnp.where(row < C_HD, acc_scr[0] / l_scr[0], acc_scr[1] / l_scr[1])
    o_ref[...] = ot.T.astype(o_ref.dtype)


def _fox_flash(qt, k, kx, vt, cr, n, length, tq):
    nq = length // tq
    nk, tk = vt.shape[2], vt.shape[4]
    return pl.pallas_call(
        _fox_flash_body,
        out_shape=_sds((n * length, C_CH), BF16),
        grid=(n, C_HEADS // 2, nq),
        in_specs=[pl.BlockSpec((None, None, None, 128, tq), lambda i, h, a: (i, h, a, 0, 0)),
                  pl.BlockSpec((length, 128), lambda i, h, a: (i, h)),
                  pl.BlockSpec((length, 128), lambda i, h, a: (i, 0)),
                  pl.BlockSpec((None, None, nk, 128, tk), lambda i, h, a: (i, h, 0, 0, 0)),
                  pl.BlockSpec((None, 16, tq), lambda i, h, a: (i * nq + a, 0, 0))],
        out_specs=pl.BlockSpec((tq, 128), lambda i, h, a: (i * nq + a, h)),
        scratch_shapes=[pltpu.VMEM((2, 256, tq), BF16), pltpu.VMEM((2, 1, tq), F32),
                        pltpu.VMEM((2, 1, tq), F32), pltpu.VMEM((2, 128, tq), F32)],
        compiler_params=_cp("arbitrary", "arbitrary", "arbitrary"),
        name="fox_flash",
    )(qt, k, kx, vt, cr)


def _fox_sample_body(pt_ref, q_ref, kn_ref, vn_ref, fn_ref, bf_ref, *refs, pages):
    k_refs = refs[0:pages]
    v_refs = refs[pages:2 * pages]
    f_refs = refs[2 * pages:3 * pages]
    o_ref, lf_ref, lf_scr, s_scr = refs[3 * pages:3 * pages + 4]
    i = pl.program_id(0)
    row, lane = _iota((8, C_CH), 0), _iota((8, C_CH), 1)
    own = jnp.right_shift(lane, 6) == row
    qf = q_ref[...].astype(F32)
    qm = jnp.where(own, qf, 0.0)
    qcol = [jnp.broadcast_to(qf[:, 128 * j:128 * (j + 1)], (128, 128)).T for j in range(C_CH // 128)]
    lfn = _log_sigmoid(fn_ref[...] + bf_ref[...])
    lf_ref[...] = lfn
    carry = jnp.sum(jnp.where(_iota((8, PAD_W), 1) == F_OFF + _iota((8, PAD_W), 0), lfn, 0.0),
                    axis=1, keepdims=True)
    for p in range(pages):
        r = jnp.bitwise_and(pt_ref[i, p], 7)
        for h in range(C_HEADS):
            lf_scr[8 * p + h:8 * p + h + 1, :] = f_refs[p][h, pl.ds(r, 1), :]
        lf_scr[8 * p + C_HEADS:8 * p + 8, :] = jnp.zeros((8 - C_HEADS, PAGE), F32)
    lf_all = lf_scr[...]
    later = _iota((PAGE, PAGE), 0) > _iota((PAGE, PAGE), 1)
    suffix = _dot_sel(lf_all, later)
    total = jnp.sum(lf_all, axis=1, keepdims=True)
    carries = [None] * pages
    for p in reversed(range(pages)):
        carries[p] = carry
        carry = carry + total[8 * p:8 * p + 8, :]
    for p in range(pages):
        for h in range(C_HEADS):
            qh = qcol[h // 2][C_HD * (h % 2):C_HD * (h % 2 + 1), :]
            s_scr[8 * p + h:8 * p + h + 1, :] = jnp.sum(k_refs[p][h] * qh, axis=0, keepdims=True)
        s_scr[8 * p + C_HEADS:8 * p + 8, :] = jnp.zeros((8 - C_HEADS, PAGE), F32)
    s = s_scr[...] + suffix + jnp.concatenate(carries, axis=0)
    s_new = jnp.sum(qm * kn_ref[...], axis=1, keepdims=True)
    row_max = jnp.max(s, axis=1, keepdims=True)
    m = s_new
    for p in range(pages):
        m = jnp.maximum(m, row_max[8 * p:8 * p + 8, :])
    pr = jnp.exp(s - jnp.concatenate([m] * pages, axis=0))
    p_new = jnp.exp(s_new - m)
    row_sum = jnp.sum(pr, axis=1, keepdims=True)
    l = p_new
    for p in range(pages):
        l = l + row_sum[8 * p:8 * p + 8, :]
    outs = []
    for j in range(C_HEADS // 2):
        pair = []
        for h in (2 * j, 2 * j + 1):
            acc = jnp.zeros((C_HD, PAGE), F32)
            for p in range(pages):
                acc = acc + v_refs[p][h] * pr[8 * p + h:8 * p + h + 1, :]
            pair.append(acc)
        outs.append(jnp.sum(jnp.concatenate(pair, axis=0).T, axis=0, keepdims=True))
    spread = lambda c: jnp.sum(jnp.where(own, c, 0.0), axis=0, keepdims=True)
    o_ref[...] = (jnp.concatenate(outs, axis=1) + spread(p_new) * vn_ref[...]) / spread(l)


def _fox_sample(page_table, layer, q, k_new, v_new, f_new, bf_row, cache_kt, cache_vt, cache_lft):
    b = q.shape[0]
    pages = page_table.shape[1]
    tok = lambda w: pl.BlockSpec((None, 1, w), lambda i, pt: (i, 0, 0))
    page = lambda p: pl.BlockSpec((None, None, C_HEADS, C_HD, PAGE), lambda i, pt: (layer, pt[i, p], 0, 0, 0))
    lfpage = lambda p: pl.BlockSpec((None, C_HEADS, 8, PAGE), lambda i, pt: (layer, 0, pt[i, p] // 8, 0))
    in_specs = ([tok(C_CH), tok(C_CH), tok(C_CH), tok(PAD_W), _lspec(bf_row, layer)]
                + [page(p) for p in range(pages)] + [page(p) for p in range(pages)]
                + [lfpage(p) for p in range(pages)])
    grid_spec = pltpu.PrefetchScalarGridSpec(
        num_scalar_prefetch=1, grid=(b,), in_specs=in_specs,
        out_specs=[tok(C_CH), tok(PAD_W)],
        scratch_shapes=[pltpu.VMEM((8 * pages, PAGE), F32), pltpu.VMEM((8 * pages, PAGE), F32)])
    return pl.pallas_call(
        functools.partial(_fox_sample_body, pages=pages),
        out_shape=[_sds((b, 1, C_CH), F32), _sds((b, 1, PAD_W), F32)],
        grid_spec=grid_spec,
        compiler_params=_cp("arbitrary"),
        name="fox_sample",
    )(page_table, q.reshape(b, 1, C_CH), k_new.reshape(b, 1, C_CH), v_new.reshape(b, 1, C_CH),
      f_new.reshape(b, 1, PAD_W), bf_row,
      *([cache_kt] * pages), *([cache_vt] * pages), *([cache_lft] * pages))


def _outproj_body(x_ref, ya0_ref, ya1_ref, yb_ref, yc_ref, wg_ref, bg_ref, wo_ref, gp_ref, g1_ref, o_ref):
    y = _gelu_tanh(jnp.concatenate([ya0_ref[...], ya1_ref[...]], axis=1))
    gate = jax.nn.sigmoid(_dot(y.astype(BF16), wg_ref[...]) + bg_ref[...])
    ya = (y * gate).astype(BF16)
    mix = (_dot(ya, wo_ref[0:A_CH, :]) + _dot(yb_ref[...], wo_ref[A_CH:A_CH + B_CH, :])
           + _dot(yc_ref[...], wo_ref[A_CH + B_CH:, :]))
    o_ref[...] = x_ref[...] + g1_ref[...] * _rms(mix, gp_ref[...])


def _ffn_body(x_ref, gpre_ref, sh_ref, sc_ref, w1_ref, w2_ref, gpost_ref, g2_ref, o_ref, *, tf):
    x = x_ref[...]
    hb = (_rms(x, gpre_ref[...]) * (1.0 + sc_ref[...]) + sh_ref[...]).astype(BF16)
    acc = jnp.zeros(x.shape, F32)
    for j in range(D_FF // tf):
        a = jnp.maximum(_dot(hb, w1_ref[:, j * tf:(j + 1) * tf]), 0.0)
        acc = acc + _dot((a * a).astype(BF16), w2_ref[j * tf:(j + 1) * tf, :])
    o_ref[...] = x + g2_ref[...] * _rms(acc, gpost_ref[...])


def _mix_ffn_body(x_ref, ya0_ref, ya1_ref, yb_ref, yc_ref, wg_ref, bg_ref, wo_ref, gqm_ref, g1_ref,
                  gpf_ref, sh_ref, sc_ref, w1_ref, w2_ref, gqf_ref, g2_ref, o_ref, mid_scr, *, tf):
    _outproj_body(x_ref, ya0_ref, ya1_ref, yb_ref, yc_ref, wg_ref, bg_ref, wo_ref, gqm_ref, g1_ref, mid_scr)
    _ffn_body(mid_scr, gpf_ref, sh_ref, sc_ref, w1_ref, w2_ref, gqf_ref, g2_ref, o_ref, tf=tf)


def _mix_ffn(x, ya0, ya1, yb, yc, w_glu, b_glu, w_out, g_post_mix, g_pre_ffn, w1, w2, g_post_ffn, layer,
             mod, mod_rows, mod_base, tiles_per_group, tm):
    t = x.shape[0]
    row = lambda w: pl.BlockSpec((tm, w), lambda i: (i, 0))
    once = dict(pipeline_mode=pl.Buffered(1))
    piece = lambda j: _mod_spec(mod_rows, mod_base, tiles_per_group, j)
    return pl.pallas_call(
        functools.partial(_mix_ffn_body, tf=1024),
        out_shape=_sds((t, D_MODEL), F32),
        grid=(t // tm,),
        in_specs=[row(D_MODEL), row(A_CH // 2), row(A_CH // 2), row(B_CH), row(C_CH), _lspec(w_glu, layer),
                  _lspec(b_glu, layer), _lspec(w_out, layer, **once), _lspec(g_post_mix, layer), piece(2),
                  _lspec(g_pre_ffn, layer), piece(3), piece(4), _lspec(w1, layer, **once),
                  _lspec(w2, layer, **once), _lspec(g_post_ffn, layer), piece(5)],
        out_specs=row(D_MODEL),
        scratch_shapes=[pltpu.VMEM((tm, D_MODEL), F32)],
        compiler_params=_cp("arbitrary"),
        name="mix_ffn",
    )(x, ya0, ya1, yb, yc, w_glu, b_glu, w_out, g_post_mix, mod, g_pre_ffn, mod, mod, w1, w2, g_post_ffn, mod)


def _pad_lanes(v, offset, width=PAD_W):
    return jnp.zeros((1, width), F32).at[0, offset:offset + v.shape[0]].set(v)


def _pad_rows(v, offset, rows=16):
    return jnp.zeros((rows, 1), F32).at[offset:offset + v.shape[0], 0].set(v)


def kernel(x_prompt, x_sample, c_prompt, c_sample, cache_k, cache_v, cache_logf, page_table, state_s5_re, state_s5_im, state_conv, state_ssm, w_ada, b_ada, g_pre_mix, g_post_mix, g_pre_ffn, g_post_ffn, w_in, w_out, s5_a_re, s5_a_im, s5_log_dt, s5_b_re, s5_b_im, s5_c_re, s5_c_im, s5_d, s5_w_glu, s5_b_glu, ssd_conv_w, ssd_conv_b, ssd_dt_bias, ssd_a_log, ssd_d, ssd_norm_g, fox_b_f, w_ff1, w_ff2):
    depth = w_ada.shape[0]
    n, length, _ = x_prompt.shape
    nb = x_sample.shape[0]
    t = n * length
    n_pool = cache_k.shape[1]
    chunks = length // S5_T
    tm_p = 512
    assert x_sample.shape[1] == 1 and length % tm_p == 0 and nb % 8 == 0

    rows = -(-(n + nb) // 8) * 8
    cond = jnp.concatenate([c_prompt, c_sample, jnp.zeros((rows - n - nb, D_MODEL), F32)], axis=0)
    mod = _ada_mod(cond, w_ada, b_ada)
    mod_p = mod[:, :n].reshape(depth * n, 1, 6 * D_MODEL)
    mod_s = mod[:, n:n + nb]

    s5krow, s5etr, s5eti, s5abr, s5abi, s5bbr, s5bbi = _s5_params(
        s5_a_re, s5_a_im, s5_log_dt, s5_b_re, s5_b_im, s5_c_re, s5_c_im)
    eye_g = jnp.eye(A_NG, dtype=F32)
    w_s5 = A_NG * A_P
    expand = (jnp.arange(PAD_W)[:, None] == jnp.arange(B_CH)[None, :] // B_HD).astype(F32)

    ck = cache_k.transpose(0, 1, 3, 4, 2)
    cv = cache_v.transpose(0, 1, 3, 4, 2)
    clf = cache_logf.transpose(0, 3, 1, 2)
    conv_t = state_conv.transpose(0, 2, 1, 3)
    ssm_in = state_ssm.reshape(depth, nb, B_CH, B_N)

    xp = x_prompt.reshape(t, D_MODEL)
    xs = x_sample.reshape(nb, D_MODEL)
    zeros_conv = jnp.zeros((n, 8, B_XBC), F32)
    zeros_ssm = jnp.zeros((n, B_HEADS, B_HD, B_N), F32)
    p_out = [[] for _ in range(7)]
    s_out = [[] for _ in range(7)]
    kv_all = None
    ssm_all = None

    o_dt = A_CH + B_CH + B_XBC
    o_f = o_dt + B_HEADS + 3 * C_CH
    w_all = jnp.concatenate(
        [w_in[..., :o_dt], w_in[..., o_dt + B_HEADS:o_f], w_in[..., o_dt:o_dt + B_HEADS], w_in[..., o_f:],
         jnp.zeros((depth, D_MODEL, PAD_W - B_HEADS - C_HEADS), F32)], axis=2).astype(BF16)
    w_o, w_g, w1, w2 = (w.astype(BF16) for w in (w_out, s5_w_glu, w_ff1, w_ff2))
    rows3 = lambda v: v.reshape(depth, 1, -1)
    g_pm, g_qm, g_pf, g_qf = rows3(g_pre_mix), rows3(g_post_mix), rows3(g_pre_ffn), rows3(g_post_ffn)
    b_g, cw, cb, ng, d_row = rows3(s5_b_glu), ssd_conv_w, rows3(ssd_conv_b), rows3(ssd_norm_g), rows3(s5_d)
    dsk = rows3(jnp.repeat(ssd_d, B_HD, axis=1))
    lanes = lambda v, off: jnp.zeros((depth, 1, PAD_W), F32).at[:, 0, off:off + v.shape[1]].set(v)
    subl = lambda v, off: jnp.zeros((depth, 16, 1), F32).at[:, off:off + v.shape[1], 0].set(v)
    dtb_c, dtb_r = lanes(ssd_dt_bias, 0), subl(ssd_dt_bias, 0)
    al_c, al_r = lanes(ssd_a_log, 0), subl(ssd_a_log, 0)
    bf_row, bf_col = lanes(fox_b_f, F_OFF), subl(fox_b_f, F_OFF)
    bre = jnp.einsum("lgcp,gh->lgchp", s5bbr, eye_g).reshape(depth, A_CH, w_s5).astype(BF16)
    bim = jnp.einsum("lgcp,gh->lgchp", s5bbi, eye_g).reshape(depth, A_CH, w_s5).astype(BF16)
    cre = jnp.einsum("lgcp,gh->lgphc", s5_c_re, eye_g).reshape(depth, w_s5, A_CH).astype(BF16)
    cim = jnp.einsum("lgcp,gh->lgphc", s5_c_im, eye_g).reshape(depth, w_s5, A_CH).astype(BF16)
    krow_rev = s5krow.reshape(depth, A_NG, A_GROUP, S5_T, A_GROUP)[:, :, :, ::-1, :]
    krow_rev = krow_rev.transpose(0, 3, 1, 2, 4).reshape(depth, S5_T, A_CH, A_GROUP)
    same_group = jnp.arange(A_CH)[:, None] // A_GROUP == jnp.arange(A_CH)[None, :] // A_GROUP
    klag = jnp.where(same_group, jnp.tile(krow_rev, (1, 1, 1, A_NG)), 0.0).reshape(depth, S5_T * A_CH, A_CH)
    kstack = jnp.concatenate([klag, jnp.zeros_like(klag)], axis=1).astype(BF16)
    etr = s5etr.transpose(0, 2, 1, 3).reshape(depth, S5_ROWS_E, w_s5)
    eti = s5eti.transpose(0, 2, 1, 3).reshape(depth, S5_ROWS_E, w_s5)
    abr, abi = s5abr.reshape(depth, 1, w_s5), s5abi.reshape(depth, 1, w_s5)

    for l in range(depth):
        tiles = length // tm_p
        ua, ub, z, xbc, _, qtb, kb, vtb, kt_all, vt_all, dtf, dtft = _in_proj(
            xp, mod_p, 1, l * n, tiles, g_pm, w_all, l, tm_p, l, depth, kv_all, False)
        kv_all = (kt_all, vt_all)
        ya0, ya1, hfr, hfi = _s5_prompt(ua, ub, kstack, bre, bim, cre, cim, etr, eti, d_row, l, n, length)
        yb, ssm_p = _ssd_prompt(xbc, z, dtf, dtft, zeros_conv, zeros_ssm, cw, cb, dtb_c, dtb_r,
                                al_c, al_r, dsk, ng, l, n, length)
        lfr, kx, cr = _fox_prep(dtf, dtft, bf_row, bf_col, l, n, length, tm_p)
        yc = _fox_flash(qtb, kb, kx, vtb, cr, n, length, tm_p)
        xp = _mix_ffn(xp, ya0, ya1, yb, yc, w_g, b_g, w_o, g_qm, g_pf, w1, w2, g_qf, l,
                      mod_p, 1, l * n, tiles, tm_p)
        p_out[2].append(lfr[:, F_OFF:F_OFF + C_HEADS].reshape(n, tiles, C_HEADS, tm_p)
                        .transpose(2, 0, 1, 3).reshape(C_HEADS, n, length))
        p_out[3].append(hfr.reshape(n, A_NG, A_P))
        p_out[4].append(hfi.reshape(n, A_NG, A_P))
        p_out[5].append(xbc.reshape(n, length, B_XBC)[:, length - (B_CONV - 1):])
        p_out[6].append(ssm_p)

        ua, ub, z, xbc, q, _, _, _, kt_s, vt_s, dtf, dtft, k, v = _in_proj(
            xs, mod_s, nb, l, 1, g_pm, w_all, l, nb, 0, 1, None, True)
        ya0, ya1, s5r, s5i = _s5_sample(ua, ub, state_s5_re[l].reshape(nb, -1), state_s5_im[l].reshape(nb, -1),
                                        abr, abi, bre, bim, cre, cim, d_row, l)
        yb, ssm_all = _ssd_sample(xbc, conv_t, z, dtf, ssm_in, ssm_all, l, cw, cb, dtb_c, al_c, expand, dsk, ng)
        yc, lfn = _fox_sample(page_table, l, q, k, v, dtf, bf_row, ck, cv, clf)
        xs = _mix_ffn(xs, ya0, ya1, yb, yc.reshape(nb, C_CH).astype(BF16), w_g, b_g, w_o, g_qm, g_pf, w1, w2,
                      g_qf, l, mod_s, nb, l, 1, nb)
        s_out[0].append(kt_s.reshape(C_HEADS, C_HD, nb))
        s_out[1].append(vt_s.reshape(C_HEADS, C_HD, nb))
        s_out[2].append(lfn[:, :, F_OFF:F_OFF + C_HEADS])
        s_out[3].append(s5r.reshape(nb, A_NG, A_P))
        s_out[4].append(s5i.reshape(nb, A_NG, A_P))
        s_out[5].append(jnp.stack([conv_t[l, 1], conv_t[l, 2], xbc], axis=0))

    plf, ps5r, ps5i, pconv, pssm = [jnp.stack(a) for a in p_out[2:]]
    sk, sv, slf, ss5r, ss5i, sconv = [jnp.stack(a) for a in s_out[:6]]
    kt_all, vt_all = kv_all
    pk = kt_all.reshape(depth, n, C_HEADS, C_HD, length).transpose(0, 1, 4, 2, 3)
    pv = vt_all.reshape(depth, n, C_HEADS, C_HD, length).transpose(0, 1, 4, 2, 3)
    plf = plf.transpose(0, 2, 3, 1)
    sk = sk.transpose(0, 3, 1, 2)[:, :, None]
    sv = sv.transpose(0, 3, 1, 2)[:, :, None]
    sconv = sconv.transpose(0, 2, 1, 3)
    sssm = ssm_all.reshape(depth, nb, B_HEADS, B_HD, B_N)
    return (xp.reshape(n, length, D_MODEL), xs.reshape(nb, 1, D_MODEL), pk, pv, plf, ps5r, ps5i, pconv, pssm,
            sk, sv, slf, ss5r, ss5i, sconv, sssm)
```

```python
import functools
import math

import jax
import jax.numpy as jnp
from jax import lax
from jax.experimental import pallas as pl
from jax.experimental.pallas import tpu as pltpu

F32 = jnp.float32
BF16 = jnp.bfloat16
HI = lax.Precision.HIGHEST

D_MODEL = 1024
A_CH = 256
A_GROUP = 16
A_NG = 16
A_P = 64
S5_T = 16
S5_ROW = S5_T * A_GROUP
S5_LEVELS = 8
S5_ROWS_E = 2 * S5_T + S5_LEVELS
B_HD = 64
B_CH = 384
B_HEADS = 6
B_NG = 2
B_N = 128
B_CONV = 4
B_XBC = 896
B_CHUNK = 128
C_HD = 64
C_CH = 384
C_HEADS = 6
PAGE = 128
D_FF = 4096
EPS = 1e-6
LOG2E = math.log2(math.e)
FLASH_TK = 512
SSD_STEP = 128
PAD_W = 128
F_OFF = 6
N_ALL = A_CH + B_CH + B_XBC + 3 * C_CH + PAD_W
VMEM_LIMIT = 56 * 1024 * 1024


def _cp(*sem):
    return pltpu.CompilerParams(dimension_semantics=sem, vmem_limit_bytes=VMEM_LIMIT)


def _sds(shape, dtype):
    return jax.ShapeDtypeStruct(shape, dtype)


def _dot(a, b, precision=None):
    return jnp.dot(a, b, preferred_element_type=F32, precision=precision)


def _dot_nt(a, b, precision=None):
    return lax.dot_general(a, b, (((1,), (1,)), ((), ())), preferred_element_type=F32, precision=precision)


def _dot_tn(a, b, precision=None):
    return lax.dot_general(a, b, (((0,), (0,)), ((), ())), preferred_element_type=F32, precision=precision)


def _split3(x):
    hi = x.astype(BF16)
    r1 = x - hi.astype(F32)
    mid = r1.astype(BF16)
    lo = (r1 - mid.astype(F32)).astype(BF16)
    return hi, mid, lo


def _sel_dot(sel, x):
    sel = sel.astype(BF16)
    return sum(_dot(sel, part) for part in _split3(x))


def _dot_sel(x, sel):
    sel = sel.astype(BF16)
    return sum(_dot(part, sel) for part in _split3(x))


def _silu(x):
    return x * jax.nn.sigmoid(x)


def _softplus(x):
    return jnp.maximum(x, 0.0) + jnp.log1p(jnp.exp(-jnp.abs(x)))


def _log_sigmoid(x):
    return jnp.minimum(x, 0.0) - jnp.log1p(jnp.exp(-jnp.abs(x)))


def _gelu_tanh(x):
    return 0.5 * x * (1.0 + jnp.tanh(math.sqrt(2.0 / math.pi) * (x + 0.044715 * (x * x * x))))


def _rms(x, g):
    return x * lax.rsqrt(jnp.mean(x * x, axis=-1, keepdims=True) + EPS) * g


def _iota(shape, dim):
    return lax.broadcasted_iota(jnp.int32, shape, dim)


def _ada_body(c_ref, w_ref, b_ref, o_ref):
    s = _silu(c_ref[...]).astype(BF16)
    o_ref[...] = _dot(s, w_ref[...].astype(BF16)) + b_ref[...]


def _ada_mod(cond, w_ada, b_ada):
    depth = w_ada.shape[0]
    rows = cond.shape[0]
    tn = 1536
    return pl.pallas_call(
        _ada_body,
        out_shape=_sds((depth, rows, 6 * D_MODEL), F32),
        grid=(depth, 6 * D_MODEL // tn),
        in_specs=[pl.BlockSpec((rows, D_MODEL), lambda l, j: (0, 0)),
                  pl.BlockSpec((None, D_MODEL, tn), lambda l, j: (l, 0, j)),
                  pl.BlockSpec((None, 1, tn), lambda l, j: (l, 0, j))],
        out_specs=pl.BlockSpec((None, rows, tn), lambda l, j: (l, 0, j)),
        compiler_params=_cp("arbitrary", "arbitrary"),
        name="ada_mod",
    )(cond, w_ada, b_ada.reshape(depth, 1, 6 * D_MODEL))


def _lspec(arr, layer, **kw):
    return pl.BlockSpec((None,) + arr.shape[1:], lambda *_: (layer,) + (0,) * (arr.ndim - 1), **kw)


def _mod_spec(rows, base, tiles_per_group, piece):
    return pl.BlockSpec((None, rows, D_MODEL), lambda i: (base + i // tiles_per_group, 0, piece))


def _inproj_body(*refs, n_in, rows_out):
    x_ref, g_ref, sh_ref, sc_ref, w_ref = refs[:5]
    (ua_ref, ub_ref, z_ref, xbc_ref, q_ref, qtb_ref, kb_ref, vtb_ref, kt_ref, vt_ref,
     dtf_ref, dtft_ref) = refs[n_in:n_in + 12]
    h = _rms(x_ref[...], g_ref[...]) * (1.0 + sc_ref[...]) + sh_ref[...]
    hb = h.astype(BF16)

    def mm(a, b):
        return _dot(hb, w_ref[:, a:b])

    o = 0
    u = mm(o, o + A_CH)
    ua_ref[...] = u[:, :A_CH // 2]
    ub_ref[...] = u[:, A_CH // 2:]
    o += A_CH
    z_ref[...] = mm(o, o + B_CH)
    o += B_CH
    xbc_ref[...] = mm(o, o + B_XBC)
    o += B_XBC
    q = mm(o, o + C_CH) * (C_HD ** -0.5)
    q_ref[...] = q.astype(BF16)
    qtb_ref[...] = (q * LOG2E).T.astype(BF16).reshape(qtb_ref.shape)
    o += C_CH
    k = mm(o, o + C_CH)
    kt_ref[...] = k.T
    kb_ref[...] = k.astype(BF16)
    o += C_CH
    v = mm(o, o + C_CH)
    vt = v.T
    vt_ref[...] = vt
    vtb = vt.astype(BF16)
    tk = vtb_ref.shape[-1]
    for j in range(vtb_ref.shape[1]):
        vtb_ref[:, j] = vtb[:, j * tk:(j + 1) * tk].reshape(C_HEADS // 2, 128, tk)
    o += C_CH
    dtf = mm(o, o + PAD_W)
    dtf_ref[...] = dtf
    dtft_ref[...] = dtf.T[:16, :]
    if rows_out:
        k_ref, v_ref = refs[n_in + 12:n_in + 14]
        k_ref[...] = k
        v_ref[...] = v


def _in_proj(x, mod, mod_rows, mod_base, tiles_per_group, g_pre, w_all, wlayer, tm, layer, depth, kv_all,
             rows_out):
    t = x.shape[0]
    groups = t // (tm * tiles_per_group)
    glen = tm * tiles_per_group
    row = lambda w: pl.BlockSpec((tm, w), lambda i: (i, 0))
    kv_spec = pl.BlockSpec((None, None, C_CH, tm),
                           lambda i: (layer, i // tiles_per_group, 0, i % tiles_per_group))
    pair_t = _sds((groups, C_HEADS // 2, tiles_per_group, 128, tm), BF16)
    pair_spec = pl.BlockSpec((None, C_HEADS // 2, None, 128, tm),
                             lambda i: (i // tiles_per_group, 0, i % tiles_per_group, 0, 0))
    tk = min(FLASH_TK, tm)
    sub = tm // tk
    vt_t = _sds((groups, C_HEADS // 2, tiles_per_group * sub, 128, tk), BF16)
    vt_spec = pl.BlockSpec((None, C_HEADS // 2, sub, 128, tk),
                           lambda i: (i // tiles_per_group, 0, i % tiles_per_group, 0, 0))
    outs = [_sds((t, A_CH // 2), F32), _sds((t, A_CH // 2), F32), _sds((t, B_CH), F32), _sds((t, B_XBC), F32),
            _sds((t, C_CH), BF16), pair_t, _sds((t, C_CH), BF16), vt_t,
            _sds((depth, groups, C_CH, glen), F32), _sds((depth, groups, C_CH, glen), F32),
            _sds((t, PAD_W), F32), _sds((16, t), F32)]
    out_specs = [row(A_CH // 2), row(A_CH // 2), row(B_CH), row(B_XBC), row(C_CH), pair_spec, row(C_CH),
                 vt_spec, kv_spec, kv_spec, row(PAD_W), pl.BlockSpec((16, tm), lambda i: (0, i))]
    if rows_out:
        outs += [_sds((t, C_CH), F32), _sds((t, C_CH), F32)]
        out_specs += [row(C_CH), row(C_CH)]
    in_specs = [row(D_MODEL),
                _lspec(g_pre, wlayer),
                _mod_spec(mod_rows, mod_base, tiles_per_group, 0),
                _mod_spec(mod_rows, mod_base, tiles_per_group, 1),
                _lspec(w_all, wlayer)]
    args = [x, g_pre, mod, mod, w_all]
    aliases = {}
    if kv_all is not None:
        in_specs += [pl.BlockSpec(memory_space=pl.ANY), pl.BlockSpec(memory_space=pl.ANY)]
        args += list(kv_all)
        aliases = {5: 8, 6: 9}
    return pl.pallas_call(
        functools.partial(_inproj_body, n_in=len(args), rows_out=rows_out),
        out_shape=outs,
        grid=(t // tm,),
        in_specs=in_specs,
        out_specs=out_specs,
        input_output_aliases=aliases,
        compiler_params=_cp("arbitrary"),
        name="in_proj",
    )(*args)


def _s5_param_body(ex_ref, ldt_ref, arr_ref, air_ref, arc_ref, aic_ref, btr_ref, bti_ref, ctr_ref, cti_ref,
                   krow_ref, etr_ref, eti_ref, abr_ref, abi_ref, bbr_ref, bbi_ref):
    dt = jnp.exp(ldt_ref[...])
    ar, ai = arr_ref[...], air_ref[...]
    mag = jnp.exp(ar * dt)
    abr, abi = mag * jnp.cos(ai * dt), mag * jnp.sin(ai * dt)
    den = ar * ar + ai * ai
    nr, ni = abr - 1.0, abi
    fr, fi = (nr * ar + ni * ai) / den, (ni * ar - nr * ai) / den
    btr, bti = btr_ref[...], bti_ref[...]
    bbr, bbi = fr * btr - fi * bti, fr * bti + fi * btr
    abr_ref[...] = abr
    abi_ref[...] = abi
    bbr_ref[...] = bbr
    bbi_ref[...] = bbi

    def epow(k, a_r, a_i):
        m = jnp.exp(k * a_r * dt)
        th = k * a_i * dt
        return m * jnp.cos(th), m * jnp.sin(th)

    er, ei = epow(ex_ref[...], ar, ai)
    etr_ref[...] = er
    eti_ref[...] = ei
    arc, aic = arc_ref[...], aic_ref[...]
    lag = jnp.right_shift(_iota((1, S5_ROW), 1), 4).astype(F32)
    elr, eli = epow(lag, arc, aic)
    ctr, cti = ctr_ref[...], cti_ref[...]
    bm_re, bm_im = ctr * elr - cti * eli, ctr * eli + cti * elr
    krow_ref[...] = _dot(bbr, bm_re, HI) - _dot(bbi, bm_im, HI)


def _s5_params(a_re, a_im, log_dt, b_re, b_im, c_re, c_im):
    depth = a_re.shape[0]
    steps = jnp.arange(S5_T, dtype=F32)
    ex = jnp.concatenate([S5_T - 1.0 - steps, steps + 1.0,
                          S5_T * 2.0 ** jnp.arange(S5_LEVELS, dtype=F32)]).reshape(S5_ROWS_E, 1)
    arr, air = a_re.reshape(depth, A_NG, 1, A_P), a_im.reshape(depth, A_NG, 1, A_P)
    arc, aic = a_re.reshape(depth, A_NG, A_P, 1), a_im.reshape(depth, A_NG, A_P, 1)
    ldt = log_dt.reshape(depth, A_NG, 1, 1)
    bt = lambda b: jnp.swapaxes(b, 2, 3)
    ct = lambda c: jnp.tile(jnp.swapaxes(c, 2, 3), (1, 1, 1, S5_T))
    blk = lambda r, c: pl.BlockSpec((None, None, r, c), lambda l, g: (l, g, 0, 0))
    outs = [_sds((depth, A_NG, A_GROUP, S5_ROW), F32),
            _sds((depth, A_NG, S5_ROWS_E, A_P), F32), _sds((depth, A_NG, S5_ROWS_E, A_P), F32),
            _sds((depth, A_NG, 1, A_P), F32), _sds((depth, A_NG, 1, A_P), F32),
            _sds((depth, A_NG, A_GROUP, A_P), F32), _sds((depth, A_NG, A_GROUP, A_P), F32)]
    out_specs = [blk(A_GROUP, S5_ROW), blk(S5_ROWS_E, A_P), blk(S5_ROWS_E, A_P), blk(1, A_P), blk(1, A_P),
                 blk(A_GROUP, A_P), blk(A_GROUP, A_P)]
    return pl.pallas_call(
        _s5_param_body,
        out_shape=outs,
        grid=(depth, A_NG),
        in_specs=[pl.BlockSpec((S5_ROWS_E, 1), lambda l, g: (0, 0)),
                  blk(1, 1), blk(1, A_P), blk(1, A_P), blk(A_P, 1), blk(A_P, 1),
                  blk(A_GROUP, A_P), blk(A_GROUP, A_P), blk(A_P, S5_ROW), blk(A_P, S5_ROW)],
        out_specs=out_specs,
        compiler_params=_cp("arbitrary", "arbitrary"),
        name="s5_params",
    )(ex, ldt, arr, air, arc, aic, bt(b_re), bt(b_im), ct(c_re), ct(c_im))


def _s5_prompt_body(ua_ref, ub_ref, kst_ref, bre_ref, bim_ref, cre_ref, cim_ref, etr_ref, eti_ref, d_ref,
                    ya_ref, yb_ref, hfr_ref, hfi_ref, ucat_scr, hr_scr, hi_scr, *, chunks, levels):
    def slab(ref, s):
        return ref[pl.ds(s, chunks, stride=S5_T), :]

    def local_state(s, carry):
        x = jnp.concatenate([slab(ua_ref, s), slab(ub_ref, s)], axis=1).astype(BF16)
        bur, bui = _dot(x, bre_ref[...]), _dot(x, bim_ref[...])
        er, ei = etr_ref[pl.ds(s, 1), :], eti_ref[pl.ds(s, 1), :]
        hr_scr[...] += er * bur - ei * bui
        hi_scr[...] += er * bui + ei * bur
        return carry

    hr_scr[...] = jnp.zeros_like(hr_scr)
    hi_scr[...] = jnp.zeros_like(hi_scr)
    lax.fori_loop(0, S5_T, local_state, 0)
    for s in range(S5_T):
        ucat_scr[:, A_CH * s:A_CH * s + A_CH // 2] = slab(ua_ref, s).astype(BF16)
        ucat_scr[:, A_CH * s + A_CH // 2:A_CH * (s + 1)] = slab(ub_ref, s).astype(BF16)
    hr, hi = hr_scr[...], hi_scr[...]
    cidx = _iota((chunks, 1), 0)
    for k in range(levels):
        d = 1 << k
        keep = cidx >= d
        sr = jnp.where(keep, pltpu.roll(hr, d, axis=0), 0.0)
        si = jnp.where(keep, pltpu.roll(hi, d, axis=0), 0.0)
        lr, li = etr_ref[2 * S5_T + k:2 * S5_T + k + 1, :], eti_ref[2 * S5_T + k:2 * S5_T + k + 1, :]
        hr, hi = hr + lr * sr - li * si, hi + lr * si + li * sr
    hfr_ref[...] = hr[chunks - 1:chunks, :]
    hfi_ref[...] = hi[chunks - 1:chunks, :]
    keep = cidx >= 1
    hr_scr[...] = jnp.where(keep, pltpu.roll(hr, 1, axis=0), 0.0)
    hi_scr[...] = jnp.where(keep, pltpu.roll(hi, 1, axis=0), 0.0)
    half = A_CH // 2

    def outputs(t, span):
        er, ei = etr_ref[pl.ds(S5_T + t, 1), :], eti_ref[pl.ds(S5_T + t, 1), :]
        pr, pi = hr_scr[...], hi_scr[...]
        gr = (er * pr - ei * pi).astype(BF16)
        gi = (er * pi + ei * pr).astype(BF16)
        first = pl.multiple_of((S5_T - 1 - t) * A_CH, A_CH)
        y = (_dot(ucat_scr[:, :A_CH * span], kst_ref[pl.ds(first, A_CH * span), :])
             + _dot(gr, cre_ref[...]) - _dot(gi, cim_ref[...]))
        dd = d_ref[...]
        ya_ref[pl.ds(t, chunks, stride=S5_T), :] = y[:, :half] + dd[:, :half] * slab(ua_ref, t)
        yb_ref[pl.ds(t, chunks, stride=S5_T), :] = y[:, half:] + dd[:, half:] * slab(ub_ref, t)

    def first_half(t, carry):
        outputs(t, S5_T // 2)
        return carry

    def second_half(t, carry):
        outputs(t, S5_T)
        return carry

    lax.fori_loop(0, S5_T // 2, first_half, 0)
    lax.fori_loop(S5_T // 2, S5_T, second_half, 0)


def _s5_prompt(ua, ub, kstack, bre, bim, cre, cim, etr, eti, d, layer, n, length):
    chunks = length // S5_T
    levels = max(1, (chunks - 1).bit_length())
    assert chunks & (chunks - 1) == 0 and levels <= S5_LEVELS
    half = A_CH // 2
    w = A_NG * A_P
    tok = pl.BlockSpec((length, half), lambda i: (i, 0))
    const = lambda r, c: pl.BlockSpec((r, c), lambda i: (0, 0))
    fin = pl.BlockSpec((None, 1, w), lambda i: (i, 0, 0))
    return pl.pallas_call(
        functools.partial(_s5_prompt_body, chunks=chunks, levels=levels),
        out_shape=[_sds((n * length, half), F32), _sds((n * length, half), F32),
                   _sds((n, 1, w), F32), _sds((n, 1, w), F32)],
        grid=(n,),
        in_specs=[tok, tok] + [_lspec(a, layer) for a in (kstack, bre, bim, cre, cim, etr, eti, d)],
        out_specs=[tok, tok, fin, fin],
        scratch_shapes=[pltpu.VMEM((chunks, S5_T * A_CH), BF16), pltpu.VMEM((chunks, w), F32),
                        pltpu.VMEM((chunks, w), F32)],
        compiler_params=_cp("arbitrary"),
        name="s5_prompt",
    )(ua, ub, kstack, bre, bim, cre, cim, etr, eti, d)


def _s5_sample_body(ua_ref, ub_ref, hr_ref, hi_ref, abr_ref, abi_ref, bre_ref, bim_ref, cre_ref, cim_ref,
                    d_ref, y0_ref, y1_ref, sr_ref, si_ref):
    u = jnp.concatenate([ua_ref[...], ub_ref[...]], axis=1)
    ub = u.astype(BF16)
    hr, hi = hr_ref[...], hi_ref[...]
    abr, abi = abr_ref[...], abi_ref[...]
    sr = abr * hr - abi * hi + _dot(ub, bre_ref[...])
    si = abr * hi + abi * hr + _dot(ub, bim_ref[...])
    sr_ref[...] = sr
    si_ref[...] = si
    y = _dot(sr.astype(BF16), cre_ref[...]) - _dot(si.astype(BF16), cim_ref[...]) + d_ref[...] * u
    y0_ref[...] = y[:, :A_CH // 2]
    y1_ref[...] = y[:, A_CH // 2:]


def _s5_sample(ua, ub, hr, hi, abr, abi, bre, bim, cre, cim, d, layer):
    b = ua.shape[0]
    w = A_NG * A_P
    full = lambda a: pl.BlockSpec(a.shape, lambda i: (0,) * a.ndim)
    return pl.pallas_call(
        _s5_sample_body,
        out_shape=[_sds((b, A_CH // 2), F32), _sds((b, A_CH // 2), F32), _sds((b, w), F32), _sds((b, w), F32)],
        grid=(1,),
        in_specs=[full(ua), full(ub), full(hr), full(hi)]
        + [_lspec(a, layer) for a in (abr, abi, bre, bim, cre, cim, d)],
        out_specs=[pl.BlockSpec((b, A_CH // 2), lambda i: (0, 0)), pl.BlockSpec((b, A_CH // 2), lambda i: (0, 0)),
                   pl.BlockSpec((b, w), lambda i: (0, 0)), pl.BlockSpec((b, w), lambda i: (0, 0))],
        compiler_params=_cp("arbitrary"),
        name="s5_sample",
    )(ua, ub, hr, hi, abr, abi, bre, bim, cre, cim, d)


def _ssd_prompt_body(xbc_ref, z_ref, dtc_ref, dtr_ref, cprev_ref, h0_ref, cw_ref, cb_ref,
                     dtbc_ref, dtbr_ref, alc_ref, alr_ref, dsk_ref, ng_ref,
                     y_ref, hf_ref, ext_scr, h_scr):
    c = pl.program_id(1)
    q = B_CHUNK

    @pl.when(c == 0)
    def _():
        ext_scr[0:8, :] = cprev_ref[...]
        h_scr[...] = h0_ref[...]

    for sub in range(xbc_ref.shape[0] // q):
        _ssd_chunk(sub, xbc_ref, z_ref, dtc_ref, dtr_ref, cw_ref, cb_ref, dtbc_ref, dtbr_ref, alc_ref, alr_ref,
                   dsk_ref, ng_ref, y_ref, ext_scr, h_scr)

    @pl.when(c == pl.num_programs(1) - 1)
    def _():
        hf_ref[...] = h_scr[...]


def _ssd_chunk(sub, xbc_ref, z_ref, dtc_ref, dtr_ref, cw_ref, cb_ref, dtbc_ref, dtbr_ref, alc_ref, alr_ref,
               dsk_ref, ng_ref, y_ref, ext_scr, h_scr):
    q = B_CHUNK
    rows = slice(sub * q, (sub + 1) * q)
    x = xbc_ref[rows, :]
    ext_scr[8:8 + q, :] = x
    cw = cw_ref[...]
    conv = (cb_ref[...] + cw[3:4] * x + cw[2:3] * ext_scr[7:7 + q, :]
            + cw[1:2] * ext_scr[6:6 + q, :] + cw[0:1] * ext_scr[5:5 + q, :])
    ext_scr[0:8, :] = x[q - 8:q, :]
    xc = _silu(conv)

    r_i, c_i = _iota((q, q), 0), _iota((q, q), 1)
    causal = r_i >= c_i
    dtc = _softplus(dtc_ref[rows, :] + dtbc_ref[...])
    da_c = dtc * (-jnp.exp(alc_ref[...]))
    acc_c = _sel_dot(causal, da_c)
    dtr = _softplus(dtr_ref[:, rows] + dtbr_ref[...])
    da_r = dtr * (-jnp.exp(alr_ref[...]))
    acc_r = _dot_sel(da_r, r_i <= c_i)

    bm = [xc[:, B_CH + B_N * g:B_CH + B_N * (g + 1)].astype(BF16) for g in range(B_NG)]
    cm = [xc[:, B_CH + B_N * (B_NG + g):B_CH + B_N * (B_NG + g + 1)].astype(BF16) for g in range(B_NG)]
    cbm = [_dot_nt(cm[g], bm[g]) for g in range(B_NG)]
    lane = _iota((q, 128), 1)
    heads_per_group = B_HEADS // B_NG
    ys = []
    for j in range(B_HEADS // 2):
        xs_pair = xc[:, 128 * j:128 * (j + 1)]
        dt_pair = jnp.where(lane < B_HD, dtc[:, 2 * j:2 * j + 1], dtc[:, 2 * j + 1:2 * j + 2])
        xdt = xs_pair * dt_pair
        xdt_b = xdt.astype(BF16)
        hprev_b = h_scr[2 * j:2 * j + 2].reshape(2 * B_HD, B_N).astype(BF16)
        outs = []
        for e in range(2):
            h = 2 * j + e
            g = h // heads_per_group
            ac = acc_c[:, h:h + 1]
            ar = acc_r[h:h + 1, :]
            lm = jnp.exp(jnp.where(causal, ac - ar, -jnp.inf))
            yd = _dot((cbm[g] * lm).astype(BF16), xdt_b)
            yo = _dot_nt(cm[g], hprev_b) * jnp.exp(ac)
            outs.append(yd + yo)
            alast = acc_r[h:h + 1, q - 1:q]
            st = _dot_tn((xdt * jnp.exp(alast - ac)).astype(BF16), bm[g])
            h_scr[h] = jnp.exp(alast) * h_scr[h] + st[B_HD * e:B_HD * (e + 1), :]
        ys.append(jnp.where(lane < B_HD, outs[0], outs[1]) + dsk_ref[:, 128 * j:128 * (j + 1)] * xs_pair)
    y = jnp.concatenate(ys, axis=1)
    y_ref[rows, :] = _rms(y * _silu(z_ref[rows, :]), ng_ref[...]).astype(y_ref.dtype)


def _ssd_prompt(xbc, z, dtf, dtft, cprev, h0, cw, cb, dtb_c, dtb_r, al_c, al_r, dsk, ng, layer, n, length):
    tq = min(SSD_STEP, length)
    nc = length // tq
    tok = lambda w: pl.BlockSpec((tq, w), lambda i, c: (i * nc + c, 0))
    return pl.pallas_call(
        _ssd_prompt_body,
        out_shape=[_sds((n * length, B_CH), BF16), _sds((n, B_HEADS, B_HD, B_N), F32)],
        grid=(n, nc),
        in_specs=[tok(B_XBC), tok(B_CH), tok(PAD_W),
                  pl.BlockSpec((16, tq), lambda i, c: (0, i * nc + c)),
                  pl.BlockSpec((None, 8, B_XBC), lambda i, c: (i, 0, 0)),
                  pl.BlockSpec((None, B_HEADS, B_HD, B_N), lambda i, c: (i, 0, 0, 0))]
        + [_lspec(a, layer) for a in (cw, cb, dtb_c, dtb_r, al_c, al_r, dsk, ng)],
        out_specs=[tok(B_CH), pl.BlockSpec((None, B_HEADS, B_HD, B_N), lambda i, c: (i, 0, 0, 0))],
        scratch_shapes=[pltpu.VMEM((8 + B_CHUNK, B_XBC), F32), pltpu.VMEM((B_HEADS, B_HD, B_N), F32)],
        compiler_params=_cp("arbitrary", "arbitrary"),
        name="ssd_prompt",
    )(xbc, z, dtf, dtft, cprev, h0, cw, cb, dtb_c, dtb_r, al_c, al_r, dsk, ng)


def _ssd_sample_body(xbc_ref, p0_ref, p1_ref, p2_ref, z_ref, dtf_ref, h0_ref, cw_ref, cb_ref,
                     dtb_ref, al_ref, ex_ref, dsk_ref, ng_ref, *rest, tb):
    y_ref, hn_ref, y_scr = rest[-3:]
    x = xbc_ref[...]
    cw = cw_ref[...]
    conv = cb_ref[...] + cw[0:1] * p0_ref[...] + cw[1:2] * p1_ref[...] + cw[2:3] * p2_ref[...] + cw[3:4] * x
    xc = _silu(conv)
    dt = _softplus(dtf_ref[...] + dtb_ref[...])
    e = jnp.exp(dt * (-jnp.exp(al_ref[...])))
    ex = ex_ref[...]
    dt_e = _dot_sel(dt, ex)
    e_e = _dot_sel(e, ex)
    xs = xc[:, :B_CH]
    xdt = xs * dt_e
    half = B_CH // B_NG
    row8, lane8 = _iota((8, B_CH), 0), _iota((8, B_CH), 1)
    r8 = _iota((8, B_N), 0)
    lane1 = _iota((1, B_CH), 1)
    for b in range(tb):
        xr = xdt[b:b + 1, :]
        lhs = jnp.where(row8 == 0, e_e[b:b + 1, :],
                        jnp.where(((row8 == 1) & (lane8 < half)) | ((row8 == 2) & (lane8 >= half)), xr, 0.0))
        rhs_e = jnp.where(r8 == 0, 1.0, 0.0)
        rhs_s = jnp.where(r8 == 1, xc[b:b + 1, B_CH:B_CH + B_N],
                          jnp.where(r8 == 2, xc[b:b + 1, B_CH + B_N:B_CH + 2 * B_N], 0.0))
        hn = _dot_tn(lhs, rhs_e, HI) * h0_ref[b] + _dot_tn(lhs, rhs_s, HI)
        hn_ref[b] = hn
        cmat = jnp.where(r8 == 0, xc[b:b + 1, B_CH + 2 * B_N:B_CH + 3 * B_N],
                         jnp.where(r8 == 1, xc[b:b + 1, B_CH + 3 * B_N:B_CH + 4 * B_N], 0.0))
        yt = _dot_nt(cmat.astype(BF16), hn.astype(BF16))
        y_scr[b:b + 1, :] = jnp.where(lane1 < half, yt[0:1, :], yt[1:2, :])
    y = y_scr[...] + dsk_ref[...] * xs
    y_ref[...] = _rms(y * _silu(z_ref[...]), ng_ref[...]).astype(y_ref.dtype)


def _ssd_sample(xbc, conv_t, z, dtf, ssm_in, ssm_out, layer, cw, cb, dtb, al, ex, dsk, ng):
    depth, b = ssm_in.shape[0], xbc.shape[0]
    tb = 8
    row = lambda w: pl.BlockSpec((tb, w), lambda i: (i, 0))
    const = lambda r, w: pl.BlockSpec((r, w), lambda i: (0, 0))
    prev = lambda k: pl.BlockSpec((None, None, tb, B_XBC), lambda i: (layer, k, i, 0))
    st = pl.BlockSpec((None, tb, B_CH, B_N), lambda i: (layer, i, 0, 0))
    in_specs = [row(B_XBC), prev(0), prev(1), prev(2), row(B_CH), row(PAD_W), st,
                _lspec(cw, layer), _lspec(cb, layer), _lspec(dtb, layer), _lspec(al, layer),
                const(PAD_W, B_CH), _lspec(dsk, layer), _lspec(ng, layer)]
    args = [xbc, conv_t, conv_t, conv_t, z, dtf, ssm_in, cw, cb, dtb, al, ex, dsk, ng]
    aliases = {}
    if ssm_out is not None:
        in_specs.append(pl.BlockSpec(memory_space=pl.ANY))
        args.append(ssm_out)
        aliases = {len(args) - 1: 1}
    return pl.pallas_call(
        functools.partial(_ssd_sample_body, tb=tb),
        out_shape=[_sds((b, B_CH), BF16), _sds((depth, b, B_CH, B_N), F32)],
        grid=(b // tb,),
        in_specs=in_specs,
        out_specs=[row(B_CH), st],
        scratch_shapes=[pltpu.VMEM((tb, B_CH), F32)],
        input_output_aliases=aliases,
        compiler_params=_cp("arbitrary"),
        name="ssd_sample",
    )(*args)


def _fox_prep_body(fc_ref, fr_ref, bfr_ref, bfc_ref, place_ref, ones_ref, lfr_ref, kx_ref, cr_ref,
                   carry_c, carry_r):
    j = pl.program_id(1)
    tl = fc_ref.shape[0]

    @pl.when(j == 0)
    def _():
        carry_c[...] = jnp.zeros_like(carry_c)
        carry_r[...] = jnp.zeros_like(carry_r)

    r_i, c_i = _iota((tl, tl), 0), _iota((tl, tl), 1)
    lfc = _log_sigmoid(fc_ref[...] + bfr_ref[...])
    cc = _sel_dot(r_i >= c_i, lfc) + carry_c[0:1, :]
    carry_c[...] = jnp.broadcast_to(cc[tl - 1:tl, :], carry_c.shape)
    parts = _split3(-LOG2E * cc)
    kx = ones_ref[...] + sum(_dot(parts[i], place_ref[i]) for i in range(3))
    kx_ref[...] = kx.astype(BF16)
    lfr = _log_sigmoid(fr_ref[...] + bfc_ref[...])
    lfr_ref[...] = lfr
    cr = _dot_sel(lfr, r_i <= c_i) + carry_r[:, 0:1]
    cr_ref[...] = cr
    carry_r[...] = jnp.broadcast_to(cr[:, tl - 1:tl], carry_r.shape)


def _bias_slot(pair, head, term):
    return 16 * pair + 3 * head + term


def _fox_prep(dtf, dtft, bf_row, bf_col, layer, n, length, tl):
    nt = length // tl
    t = n * length
    place = jnp.zeros((3, PAD_W, 128), F32)
    ones = jnp.zeros((1, 128), F32)
    for pair in range(C_HEADS // 2):
        for term in range(3):
            ones = ones.at[0, _bias_slot(pair, 2, term)].set(1.0)
            for head in range(2):
                place = place.at[term, F_OFF + 2 * pair + head, _bias_slot(pair, head, term)].set(1.0)
    tok = pl.BlockSpec((tl, PAD_W), lambda i, j: (i * nt + j, 0))
    rowt = pl.BlockSpec((None, 16, tl), lambda i, j: (i * nt + j, 0, 0))
    return pl.pallas_call(
        _fox_prep_body,
        out_shape=[_sds((n * nt, 16, tl), F32), _sds((t, 128), BF16), _sds((n * nt, 16, tl), F32)],
        grid=(n, nt),
        in_specs=[tok, pl.BlockSpec((16, tl), lambda i, j: (0, i * nt + j)),
                  _lspec(bf_row, layer), _lspec(bf_col, layer),
                  pl.BlockSpec((3, PAD_W, 128), lambda i, j: (0, 0, 0)),
                  pl.BlockSpec((1, 128), lambda i, j: (0, 0))],
        out_specs=[rowt, pl.BlockSpec((tl, 128), lambda i, j: (i * nt + j, 0)), rowt],
        scratch_shapes=[pltpu.VMEM((8, PAD_W), F32), pltpu.VMEM((16, 128), F32)],
        compiler_params=_cp("arbitrary", "arbitrary"),
        name="fox_prep",
    )(dtf, dtft, bf_row, bf_col, place.astype(BF16), ones)


def _fox_flash_body(qt_ref, k_ref, kx_ref, vt_ref, cr_ref, o_ref, qa_scr, m_scr, l_scr, acc_scr):
    hp, qi = pl.program_id(1), pl.program_id(2)
    tq = qt_ref.shape[1]
    tk = vt_ref.shape[2]
    row = _iota((128, tq), 0)
    qt = qt_ref[...]
    base = 16 * hp
    for e in range(2):
        cq = _split3(LOG2E * cr_ref[pl.ds(F_OFF + 2 * hp + e, 1), :])
        qx = jnp.where((row >= base + 3 * e) & (row < base + 3 * e + 3), 1.0, 0.0).astype(BF16)
        for term in range(3):
            qx = jnp.where(row == base + 6 + term, cq[term], qx)
        qa_scr[e, 0:128, :] = jnp.where((row < C_HD) == (e == 0), qt, jnp.zeros_like(qt))
        qa_scr[e, 128:256, :] = qx
    m_scr[...] = jnp.full(m_scr.shape, -jnp.inf, F32)
    l_scr[...] = jnp.zeros_like(l_scr)
    acc_scr[...] = jnp.zeros_like(acc_scr)

    def block(ki, first_key):
        rows = pl.ds(pl.multiple_of(ki * tk, tk), tk)
        ka = jnp.concatenate([k_ref[rows, :], kx_ref[rows, :]], axis=1)
        vt = vt_ref[ki]
        scores = [_dot(ka, qa_scr[e]) for e in range(2)]
        probs, alphas = [], []
        for e in range(2):
            s = scores[e]
            if first_key is not None:
                s = jnp.where(first_key + _iota((tk, tq), 0) <= _iota((tk, tq), 1), s, -jnp.inf)
            m_old = m_scr[e]
            m_new = jnp.maximum(m_old, jnp.max(s, axis=0, keepdims=True))
            alpha = jnp.exp2(m_old - m_new)
            p = jnp.exp2(s - m_new)
            l_scr[e] = alpha * l_scr[e] + jnp.sum(p, axis=0, keepdims=True)
            m_scr[e] = m_new
            probs.append(p.astype(BF16))
            alphas.append(alpha)
        for e in range(2):
            acc_scr[e] = alphas[e] * acc_scr[e] + _dot(vt, probs[e])

    def below_diagonal(ki, carry):
        block(ki, None)
        return carry

    sub = tq // tk
    lax.fori_loop(0, qi * sub, below_diagonal, 0)
    for j in range(sub):
        block(qi * sub + j, j * tk)
    ot = jnp.where(row < C_HD, acc_scr[0] / l_scr[0], acc_scr[1] / l_scr[1])
    o_ref[...] = ot.T.astype(o_ref.dtype)


def _fox_flash(qt, k, kx, vt, cr, n, length, tq):
    nq = length // tq
    nk, tk = vt.shape[2], vt.shape[4]
    return pl.pallas_call(
        _fox_flash_body,
        out_shape=_sds((n * length, C_CH), BF16),
        grid=(n, C_HEADS // 2, nq),
        in_specs=[pl.BlockSpec((None, None, None, 128, tq), lambda i, h, a: (i, h, a, 0, 0)),
                  pl.BlockSpec((length, 128), lambda i, h, a: (i, h)),
                  pl.BlockSpec((length, 128), lambda i, h, a: (i, 0)),
                  pl.BlockSpec((None, None, nk, 128, tk), lambda i, h, a: (i, h, 0, 0, 0)),
                  pl.BlockSpec((None, 16, tq), lambda i, h, a: (i * nq + a, 0, 0))],
        out_specs=pl.BlockSpec((tq, 128), lambda i, h, a: (i * nq + a, h)),
        scratch_shapes=[pltpu.VMEM((2, 256, tq), BF16), pltpu.VMEM((2, 1, tq), F32),
                        pltpu.VMEM((2, 1, tq), F32), pltpu.VMEM((2, 128, tq), F32)],
        compiler_params=_cp("arbitrary", "arbitrary", "arbitrary"),
        name="fox_flash",
    )(qt, k, kx, vt, cr)


def _fox_sample_body(pt_ref, q_ref, kn_ref, vn_ref, fn_ref, bf_ref, kt_hbm, vt_hbm, lft_hbm,
                     o_ref, lf_ref, kbuf, vbuf, fbuf, sems, lf_scr, s_scr, *, pages, layer):
    i = pl.program_id(0)
    slot = jnp.bitwise_and(i, 1)

    def page_copies(step, buf_slot, p):
        pg = pt_ref[step, p]
        group = pl.multiple_of(jnp.bitwise_and(pg, -8), 8)
        return (pltpu.make_async_copy(kt_hbm.at[layer, pg], kbuf.at[buf_slot, p], sems.at[buf_slot, 0, p]),
                pltpu.make_async_copy(vt_hbm.at[layer, pg], vbuf.at[buf_slot, p], sems.at[buf_slot, 1, p]),
                pltpu.make_async_copy(lft_hbm.at[layer, :, pl.ds(group, 8), :], fbuf.at[buf_slot, p],
                                      sems.at[buf_slot, 2, p]))

    def start_fetch(step, buf_slot):
        for p in range(pages):
            for copy in page_copies(step, buf_slot, p):
                copy.start()

    @pl.when(i == 0)
    def _():
        start_fetch(0, 0)

    @pl.when(i + 1 < pl.num_programs(0))
    def _():
        start_fetch(i + 1, 1 - slot)

    for p in range(pages):
        for copy in page_copies(i, slot, p):
            copy.wait()
    k_refs = [kbuf.at[slot, p] for p in range(pages)]
    v_refs = [vbuf.at[slot, p] for p in range(pages)]
    f_refs = [fbuf.at[slot, p] for p in range(pages)]
    row, lane = _iota((8, C_CH), 0), _iota((8, C_CH), 1)
    own = jnp.right_shift(lane, 6) == row
    qf = q_ref[...].astype(F32)
    qm = jnp.where(own, qf, 0.0)
    qcol = [jnp.broadcast_to(qf[:, 128 * j:128 * (j + 1)], (128, 128)).T for j in range(C_CH // 128)]
    lfn = _log_sigmoid(fn_ref[...] + bf_ref[...])
    lf_ref[...] = lfn
    carry = jnp.sum(jnp.where(_iota((8, PAD_W), 1) == F_OFF + _iota((8, PAD_W), 0), lfn, 0.0),
                    axis=1, keepdims=True)
    for p in range(pages):
        r = jnp.bitwise_and(pt_ref[i, p], 7)
        for h in range(C_HEADS):
            lf_scr[8 * p + h:8 * p + h + 1, :] = f_refs[p][h, pl.ds(r, 1), :]
        lf_scr[8 * p + C_HEADS:8 * p + 8, :] = jnp.zeros((8 - C_HEADS, PAGE), F32)
    lf_all = lf_scr[...]
    later = _iota((PAGE, PAGE), 0) > _iota((PAGE, PAGE), 1)
    suffix = _dot_sel(lf_all, later)
    total = jnp.sum(lf_all, axis=1, keepdims=True)
    carries = [None] * pages
    for p in reversed(range(pages)):
        carries[p] = carry
        carry = carry + total[8 * p:8 * p + 8, :]
    for p in range(pages):
        for h in range(C_HEADS):
            qh = qcol[h // 2][C_HD * (h % 2):C_HD * (h % 2 + 1), :]
            s_scr[8 * p + h:8 * p + h + 1, :] = jnp.sum(k_refs[p][h] * qh, axis=0, keepdims=True)
        s_scr[8 * p + C_HEADS:8 * p + 8, :] = jnp.zeros((8 - C_HEADS, PAGE), F32)
    s = s_scr[...] + suffix + jnp.concatenate(carries, axis=0)
    s_new = jnp.sum(qm * kn_ref[...], axis=1, keepdims=True)
    row_max = jnp.max(s, axis=1, keepdims=True)
    m = s_new
    for p in range(pages):
        m = jnp.maximum(m, row_max[8 * p:8 * p + 8, :])
    pr = jnp.exp(s - jnp.concatenate([m] * pages, axis=0))
    p_new = jnp.exp(s_new - m)
    row_sum = jnp.sum(pr, axis=1, keepdims=True)
    l = p_new
    for p in range(pages):
        l = l + row_sum[8 * p:8 * p + 8, :]
    outs = []
    for j in range(C_HEADS // 2):
        pair = []
        for h in (2 * j, 2 * j + 1):
            acc = jnp.zeros((C_HD, PAGE), F32)
            for p in range(pages):
                acc = acc + v_refs[p][h] * pr[8 * p + h:8 * p + h + 1, :]
            pair.append(acc)
        outs.append(jnp.sum(jnp.concatenate(pair, axis=0).T, axis=0, keepdims=True))
    spread = lambda c: jnp.sum(jnp.where(own, c, 0.0), axis=0, keepdims=True)
    o_ref[...] = (jnp.concatenate(outs, axis=1) + spread(p_new) * vn_ref[...]) / spread(l)


def _fox_sample(page_table, layer, q, k_new, v_new, f_new, bf_row, cache_kt, cache_vt, cache_lft):
    b = q.shape[0]
    pages = page_table.shape[1]
    assert cache_lft.shape[2] % 8 == 0
    tok = lambda w: pl.BlockSpec((None, 1, w), lambda i, pt: (i, 0, 0))
    hbm = pl.BlockSpec(memory_space=pl.ANY)
    in_specs = [tok(C_CH), tok(C_CH), tok(C_CH), tok(PAD_W), _lspec(bf_row, layer), hbm, hbm, hbm]
    grid_spec = pltpu.PrefetchScalarGridSpec(
        num_scalar_prefetch=1, grid=(b,), in_specs=in_specs,
        out_specs=[tok(C_CH), tok(PAD_W)],
        scratch_shapes=[pltpu.VMEM((2, pages, C_HEADS, C_HD, PAGE), F32),
                        pltpu.VMEM((2, pages, C_HEADS, C_HD, PAGE), F32),
                        pltpu.VMEM((2, pages, C_HEADS, 8, PAGE), F32),
                        pltpu.SemaphoreType.DMA((2, 3, pages)),
                        pltpu.VMEM((8 * pages, PAGE), F32), pltpu.VMEM((8 * pages, PAGE), F32)])
    return pl.pallas_call(
        functools.partial(_fox_sample_body, pages=pages, layer=layer),
        out_shape=[_sds((b, 1, C_CH), F32), _sds((b, 1, PAD_W), F32)],
        grid_spec=grid_spec,
        compiler_params=_cp("arbitrary"),
        name="fox_sample",
    )(page_table, q.reshape(b, 1, C_CH), k_new.reshape(b, 1, C_CH), v_new.reshape(b, 1, C_CH),
      f_new.reshape(b, 1, PAD_W), bf_row, cache_kt, cache_vt, cache_lft)


def _outproj_body(x_ref, ya0_ref, ya1_ref, yb_ref, yc_ref, wg_ref, bg_ref, wo_ref, gp_ref, g1_ref, o_ref):
    y = _gelu_tanh(jnp.concatenate([ya0_ref[...], ya1_ref[...]], axis=1))
    gate = jax.nn.sigmoid(_dot(y.astype(BF16), wg_ref[...]) + bg_ref[...])
    ya = (y * gate).astype(BF16)
    mix = (_dot(ya, wo_ref[0:A_CH, :]) + _dot(yb_ref[...], wo_ref[A_CH:A_CH + B_CH, :])
           + _dot(yc_ref[...], wo_ref[A_CH + B_CH:, :]))
    o_ref[...] = x_ref[...] + g1_ref[...] * _rms(mix, gp_ref[...])


def _ffn_body(x_ref, gpre_ref, sh_ref, sc_ref, w1_ref, w2_ref, gpost_ref, g2_ref, o_ref, *, tf):
    x = x_ref[...]
    hb = (_rms(x, gpre_ref[...]) * (1.0 + sc_ref[...]) + sh_ref[...]).astype(BF16)
    acc = jnp.zeros(x.shape, F32)
    for j in range(D_FF // tf):
        a = jnp.maximum(_dot(hb, w1_ref[:, j * tf:(j + 1) * tf]), 0.0)
        acc = acc + _dot((a * a).astype(BF16), w2_ref[j * tf:(j + 1) * tf, :])
    o_ref[...] = x + g2_ref[...] * _rms(acc, gpost_ref[...])


def _mix_ffn_body(x_ref, ya0_ref, ya1_ref, yb_ref, yc_ref, wg_ref, bg_ref, wo_ref, gqm_ref, g1_ref,
                  gpf_ref, sh_ref, sc_ref, w1_ref, w2_ref, gqf_ref, g2_ref, o_ref, mid_scr, *, tf):
    _outproj_body(x_ref, ya0_ref, ya1_ref, yb_ref, yc_ref, wg_ref, bg_ref, wo_ref, gqm_ref, g1_ref, mid_scr)
    _ffn_body(mid_scr, gpf_ref, sh_ref, sc_ref, w1_ref, w2_ref, gqf_ref, g2_ref, o_ref, tf=tf)


def _mix_ffn(x, ya0, ya1, yb, yc, w_glu, b_glu, w_out, g_post_mix, g_pre_ffn, w1, w2, g_post_ffn, layer,
             mod, mod_rows, mod_base, tiles_per_group, tm):
    t = x.shape[0]
    row = lambda w: pl.BlockSpec((tm, w), lambda i: (i, 0))
    once = dict(pipeline_mode=pl.Buffered(1))
    piece = lambda j: _mod_spec(mod_rows, mod_base, tiles_per_group, j)
    return pl.pallas_call(
        functools.partial(_mix_ffn_body, tf=1024),
        out_shape=_sds((t, D_MODEL), F32),
        grid=(t // tm,),
        in_specs=[row(D_MODEL), row(A_CH // 2), row(A_CH // 2), row(B_CH), row(C_CH), _lspec(w_glu, layer),
                  _lspec(b_glu, layer), _lspec(w_out, layer, **once), _lspec(g_post_mix, layer), piece(2),
                  _lspec(g_pre_ffn, layer), piece(3), piece(4), _lspec(w1, layer, **once),
                  _lspec(w2, layer, **once), _lspec(g_post_ffn, layer), piece(5)],
        out_specs=row(D_MODEL),
        scratch_shapes=[pltpu.VMEM((tm, D_MODEL), F32)],
        compiler_params=_cp("arbitrary"),
        name="mix_ffn",
    )(x, ya0, ya1, yb, yc, w_glu, b_glu, w_out, g_post_mix, mod, g_pre_ffn, mod, mod, w1, w2, g_post_ffn, mod)


def _pad_lanes(v, offset, width=PAD_W):
    return jnp.zeros((1, width), F32).at[0, offset:offset + v.shape[0]].set(v)


def _pad_rows(v, offset, rows=16):
    return jnp.zeros((rows, 1), F32).at[offset:offset + v.shape[0], 0].set(v)


def kernel(x_prompt, x_sample, c_prompt, c_sample, cache_k, cache_v, cache_logf, page_table, state_s5_re, state_s5_im, state_conv, state_ssm, w_ada, b_ada, g_pre_mix, g_post_mix, g_pre_ffn, g_post_ffn, w_in, w_out, s5_a_re, s5_a_im, s5_log_dt, s5_b_re, s5_b_im, s5_c_re, s5_c_im, s5_d, s5_w_glu, s5_b_glu, ssd_conv_w, ssd_conv_b, ssd_dt_bias, ssd_a_log, ssd_d, ssd_norm_g, fox_b_f, w_ff1, w_ff2):
    depth = w_ada.shape[0]
    n, length, _ = x_prompt.shape
    nb = x_sample.shape[0]
    t = n * length
    n_pool = cache_k.shape[1]
    chunks = length // S5_T
    tm_p = 512
    assert x_sample.shape[1] == 1 and length % tm_p == 0 and nb % 8 == 0

    rows = -(-(n + nb) // 8) * 8
    cond = jnp.concatenate([c_prompt, c_sample, jnp.zeros((rows - n - nb, D_MODEL), F32)], axis=0)
    mod = _ada_mod(cond, w_ada, b_ada)
    mod_p = mod[:, :n].reshape(depth * n, 1, 6 * D_MODEL)
    mod_s = mod[:, n:n + nb]

    s5krow, s5etr, s5eti, s5abr, s5abi, s5bbr, s5bbi = _s5_params(
        s5_a_re, s5_a_im, s5_log_dt, s5_b_re, s5_b_im, s5_c_re, s5_c_im)
    eye_g = jnp.eye(A_NG, dtype=F32)
    w_s5 = A_NG * A_P
    expand = (jnp.arange(PAD_W)[:, None] == jnp.arange(B_CH)[None, :] // B_HD).astype(F32)

    ck = cache_k.transpose(0, 1, 3, 4, 2)
    cv = cache_v.transpose(0, 1, 3, 4, 2)
    clf = cache_logf.transpose(0, 3, 1, 2)
    conv_t = state_conv.transpose(0, 2, 1, 3)
    ssm_in = state_ssm.reshape(depth, nb, B_CH, B_N)

    xp = x_prompt.reshape(t, D_MODEL)
    xs = x_sample.reshape(nb, D_MODEL)
    zeros_conv = jnp.zeros((n, 8, B_XBC), F32)
    zeros_ssm = jnp.zeros((n, B_HEADS, B_HD, B_N), F32)
    p_out = [[] for _ in range(7)]
    s_out = [[] for _ in range(7)]
    kv_all = None
    ssm_all = None

    o_dt = A_CH + B_CH + B_XBC
    o_f = o_dt + B_HEADS + 3 * C_CH
    w_all = jnp.concatenate(
        [w_in[..., :o_dt], w_in[..., o_dt + B_HEADS:o_f], w_in[..., o_dt:o_dt + B_HEADS], w_in[..., o_f:],
         jnp.zeros((depth, D_MODEL, PAD_W - B_HEADS - C_HEADS), F32)], axis=2).astype(BF16)
    w_o, w_g, w1, w2 = (w.astype(BF16) for w in (w_out, s5_w_glu, w_ff1, w_ff2))
    rows3 = lambda v: v.reshape(depth, 1, -1)
    g_pm, g_qm, g_pf, g_qf = rows3(g_pre_mix), rows3(g_post_mix), rows3(g_pre_ffn), rows3(g_post_ffn)
    b_g, cw, cb, ng, d_row = rows3(s5_b_glu), ssd_conv_w, rows3(ssd_conv_b), rows3(ssd_norm_g), rows3(s5_d)
    dsk = rows3(jnp.repeat(ssd_d, B_HD, axis=1))
    lanes = lambda v, off: jnp.zeros((depth, 1, PAD_W), F32).at[:, 0, off:off + v.shape[1]].set(v)
    subl = lambda v, off: jnp.zeros((depth, 16, 1), F32).at[:, off:off + v.shape[1], 0].set(v)
    dtb_c, dtb_r = lanes(ssd_dt_bias, 0), subl(ssd_dt_bias, 0)
    al_c, al_r = lanes(ssd_a_log, 0), subl(ssd_a_log, 0)
    bf_row, bf_col = lanes(fox_b_f, F_OFF), subl(fox_b_f, F_OFF)
    bre = jnp.einsum("lgcp,gh->lgchp", s5bbr, eye_g).reshape(depth, A_CH, w_s5).astype(BF16)
    bim = jnp.einsum("lgcp,gh->lgchp", s5bbi, eye_g).reshape(depth, A_CH, w_s5).astype(BF16)
    cre = jnp.einsum("lgcp,gh->lgphc", s5_c_re, eye_g).reshape(depth, w_s5, A_CH).astype(BF16)
    cim = jnp.einsum("lgcp,gh->lgphc", s5_c_im, eye_g).reshape(depth, w_s5, A_CH).astype(BF16)
    krow_rev = s5krow.reshape(depth, A_NG, A_GROUP, S5_T, A_GROUP)[:, :, :, ::-1, :]
    krow_rev = krow_rev.transpose(0, 3, 1, 2, 4).reshape(depth, S5_T, A_CH, A_GROUP)
    same_group = jnp.arange(A_CH)[:, None] // A_GROUP == jnp.arange(A_CH)[None, :] // A_GROUP
    klag = jnp.where(same_group, jnp.tile(krow_rev, (1, 1, 1, A_NG)), 0.0).reshape(depth, S5_T * A_CH, A_CH)
    kstack = jnp.concatenate([klag, jnp.zeros_like(klag)], axis=1).astype(BF16)
    etr = s5etr.transpose(0, 2, 1, 3).reshape(depth, S5_ROWS_E, w_s5)
    eti = s5eti.transpose(0, 2, 1, 3).reshape(depth, S5_ROWS_E, w_s5)
    abr, abi = s5abr.reshape(depth, 1, w_s5), s5abi.reshape(depth, 1, w_s5)

    for l in range(depth):
        tiles = length // tm_p
        ua, ub, z, xbc, _, qtb, kb, vtb, kt_all, vt_all, dtf, dtft = _in_proj(
            xp, mod_p, 1, l * n, tiles, g_pm, w_all, l, tm_p, l, depth, kv_all, False)
        kv_all = (kt_all, vt_all)
        ya0, ya1, hfr, hfi = _s5_prompt(ua, ub, kstack, bre, bim, cre, cim, etr, eti, d_row, l, n, length)
        yb, ssm_p = _ssd_prompt(xbc, z, dtf, dtft, zeros_conv, zeros_ssm, cw, cb, dtb_c, dtb_r,
                                al_c, al_r, dsk, ng, l, n, length)
        lfr, kx, cr = _fox_prep(dtf, dtft, bf_row, bf_col, l, n, length, tm_p)
        yc = _fox_flash(qtb, kb, kx, vtb, cr, n, length, tm_p)
        xp = _mix_ffn(xp, ya0, ya1, yb, yc, w_g, b_g, w_o, g_qm, g_pf, w1, w2, g_qf, l,
                      mod_p, 1, l * n, tiles, tm_p)
        p_out[2].append(lfr[:, F_OFF:F_OFF + C_HEADS].reshape(n, tiles, C_HEADS, tm_p)
                        .transpose(2, 0, 1, 3).reshape(C_HEADS, n, length))
        p_out[3].append(hfr.reshape(n, A_NG, A_P))
        p_out[4].append(hfi.reshape(n, A_NG, A_P))
        p_out[5].append(xbc.reshape(n, length, B_XBC)[:, length - (B_CONV - 1):])
        p_out[6].append(ssm_p)

        ua, ub, z, xbc, q, _, _, _, kt_s, vt_s, dtf, dtft, k, v = _in_proj(
            xs, mod_s, nb, l, 1, g_pm, w_all, l, nb, 0, 1, None, True)
        ya0, ya1, s5r, s5i = _s5_sample(ua, ub, state_s5_re[l].reshape(nb, -1), state_s5_im[l].reshape(nb, -1),
                                        abr, abi, bre, bim, cre, cim, d_row, l)
        yb, ssm_all = _ssd_sample(xbc, conv_t, z, dtf, ssm_in, ssm_all, l, cw, cb, dtb_c, al_c, expand, dsk, ng)
        yc, lfn = _fox_sample(page_table, l, q, k, v, dtf, bf_row, ck, cv, clf)
        xs = _mix_ffn(xs, ya0, ya1, yb, yc.reshape(nb, C_CH).astype(BF16), w_g, b_g, w_o, g_qm, g_pf, w1, w2,
                      g_qf, l, mod_s, nb, l, 1, nb)
        s_out[0].append(kt_s.reshape(C_HEADS, C_HD, nb))
        s_out[1].append(vt_s.reshape(C_HEADS, C_HD, nb))
        s_out[2].append(lfn[:, :, F_OFF:F_OFF + C_HEADS])
        s_out[3].append(s5r.reshape(nb, A_NG, A_P))
        s_out[4].append(s5i.reshape(nb, A_NG, A_P))
        s_out[5].append(jnp.stack([conv_t[l, 1], conv_t[l, 2], xbc], axis=0))

    plf, ps5r, ps5i, pconv, pssm = [jnp.stack(a) for a in p_out[2:]]
    sk, sv, slf, ss5r, ss5i, sconv = [jnp.stack(a) for a in s_out[:6]]
    kt_all, vt_all = kv_all
    pk = kt_all.reshape(depth, n, C_HEADS, C_HD, length).transpose(0, 1, 4, 2, 3)
    pv = vt_all.reshape(depth, n, C_HEADS, C_HD, length).transpose(0, 1, 4, 2, 3)
    plf = plf.transpose(0, 2, 3, 1)
    sk = sk.transpose(0, 3, 1, 2)[:, :, None]
    sv = sv.transpose(0, 3, 1, 2)[:, :, None]
    sconv = sconv.transpose(0, 2, 1, 3)
    sssm = ssm_all.reshape(depth, nb, B_HEADS, B_HD, B_N)
    return (xp.reshape(n, length, D_MODEL), xs.reshape(nb, 1, D_MODEL), pk, pv, plf, ps5r, ps5i, pconv, pssm,
            sk, sv, slf, ss5r, ss5i, sconv, sssm)
```
